```python
import math
import jax, jax.numpy as jnp
from jax import lax
import numpy as np

D_MODEL = 2048
BATCH = 4
SEQ = 2048
DEPTH = 2

N_A_LAYERS = DEPTH // 2
N_B_LAYERS = DEPTH - N_A_LAYERS

RWKV_HEAD = 64
RWKV_HEADS = D_MODEL // RWKV_HEAD
D_DECAY_LORA = max(32, int(round(1.8 * D_MODEL ** 0.5 / 32)) * 32)
D_AAA_LORA = max(32, int(round(1.8 * D_MODEL ** 0.5 / 32)) * 32)
D_GATE_LORA = max(32, int(round(0.6 * D_MODEL ** 0.8 / 32)) * 32)
GN_EPS = 64e-5

MOBA_HEAD_DIM = 128
MOBA_HEADS = D_MODEL // MOBA_HEAD_DIM
MOBA_BLOCK = 256
MOBA_TOPK = 3
MOBA_Q_CHUNK = 32
NEG = -1e30

D_FF = ((8 * D_MODEL // 3 + 255) // 256) * 256

ALPHA = (2 * DEPTH) ** 0.25
BETA = (8 * DEPTH) ** -0.25
LN_EPS = 1e-5

kernel_name = "yoco_rwkv7_moba_deepnorm"


def _layernorm(x, w, b):
    xf = x.astype(jnp.float32)
    mu = jnp.mean(xf, axis=-1, keepdims=True)
    var = jnp.mean(jnp.square(xf - mu), axis=-1, keepdims=True)
    y = (xf - mu) * lax.rsqrt(var + LN_EPS)
    return (y * w + b).astype(x.dtype)


def _swiglu(x, w_gate, w_up, w_down):
    return (jax.nn.silu(x @ w_gate) * (x @ w_up)) @ w_down


def _rwkv7_step(state, inp):
    r_t, w_t, k_t, v_t, a_t, b_t = inp
    sa = jnp.einsum('bhij,bhj->bhi', state, a_t)
    state = (state * w_t[:, :, None, :]
             + sa[..., None] * b_t[:, :, None, :]
             + v_t[..., None] * k_t[:, :, None, :])
    y = jnp.einsum('bhij,bhj->bhi', state, r_t)
    return state, y


def _rwkv7_scan(r, w, k, v, a, b):
    B, S, H, N = r.shape
    seq = tuple(jnp.moveaxis(t.astype(jnp.float32), 1, 0) for t in (r, w, k, v, a, b))
    state0 = jnp.zeros((B, H, N, N), jnp.float32)
    _, y = lax.scan(_rwkv7_step, state0, seq)
    return jnp.moveaxis(y, 0, 1)


def _rwkv7_time_mix(x, mix, w_r, w_k, w_v, w_o, w0, w1, w2, a0, a1, a2,
                    g1, g2, k_k, k_a, r_k, lnx_w, lnx_b):
    B, S, C = x.shape
    H, N = RWKV_HEADS, RWKV_HEAD
    x_prev = jnp.pad(x, ((0, 0), (1, 0), (0, 0)))[:, :-1]
    xx = x_prev - x
    xr = x + xx * mix[0]
    xw = x + xx * mix[1]
    xk = x + xx * mix[2]
    xv = x + xx * mix[3]
    xa = x + xx * mix[4]
    xg = x + xx * mix[5]
    r = xr @ w_r
    k = xk @ w_k
    v = xv @ w_v
    w_log = -jax.nn.softplus(-(w0 + jnp.tanh(xw @ w1) @ w2)) - 0.5
    decay = jnp.exp(-jnp.exp(w_log.astype(jnp.float32)))
    a = jax.nn.sigmoid(a0 + (xa @ a1) @ a2)
    g = jax.nn.sigmoid(xg @ g1) @ g2
    kk = (k * k_k).astype(jnp.float32).reshape(B, S, H, N)
    kk = kk / jnp.maximum(jnp.sqrt(jnp.sum(kk * kk, axis=-1, keepdims=True)), 1e-12)
    k = k * (1.0 + (a - 1.0) * k_a)
    rh = r.reshape(B, S, H, N).astype(jnp.float32)
    kh = k.reshape(B, S, H, N).astype(jnp.float32)
    vh = v.reshape(B, S, H, N).astype(jnp.float32)
    ah = a.reshape(B, S, H, N).astype(jnp.float32)
    y = _rwkv7_scan(rh, decay.reshape(B, S, H, N), kh, vh, -kk, kk * ah)
    mu = jnp.mean(y, axis=-1, keepdims=True)
    var = jnp.mean(jnp.square(y - mu), axis=-1, keepdims=True)
    yn = ((y - mu) * lax.rsqrt(var + GN_EPS)).reshape(B, S, C) * lnx_w + lnx_b
    bonus = (jnp.sum(rh * kh * r_k, axis=-1, keepdims=True) * vh).reshape(B, S, C)
    return ((yn + bonus).astype(x.dtype) * g) @ w_o


def _shared_kv(h, kv_w_k, kv_w_v):
    B, S, C = h.shape
    H, Dh, BLK = MOBA_HEADS, MOBA_HEAD_DIM, MOBA_BLOCK
    n_blocks = -(-S // BLK)
    pad = n_blocks * BLK - S
    k = (h @ kv_w_k).reshape(B, S, H, Dh).transpose(0, 2, 1, 3)
    v = (h @ kv_w_v).reshape(B, S, H, Dh).transpose(0, 2, 1, 3)
    k_pad = jnp.pad(k, ((0, 0), (0, 0), (0, pad), (0, 0)))
    v_pad = jnp.pad(v, ((0, 0), (0, 0), (0, pad), (0, 0)))
    k_mean = jnp.mean(k_pad.reshape(B, H, n_blocks, BLK, Dh).astype(jnp.float32),
                      axis=3).astype(h.dtype)
    return k_pad, v_pad, k_mean


def _moba_attention(x, w_q, w_o, k_pad, v_pad, k_mean):
    B, S, C = x.shape
    H, Dh, BLK, QC = MOBA_HEADS, MOBA_HEAD_DIM, MOBA_BLOCK, MOBA_Q_CHUNK
    n_blocks = k_pad.shape[2] // BLK
    top_k = max(1, min(MOBA_TOPK, n_blocks - 1))
    scale = Dh ** -0.5
    q = (x @ w_q).reshape(B, S, H, Dh).transpose(0, 2, 1, 3)
    k_blocks = k_pad.reshape(B, H, n_blocks, BLK, Dh)
    v_blocks = v_pad.reshape(B, H, n_blocks, BLK, Dh)
    b_ix = jnp.arange(B)[:, None, None, None]
    h_ix = jnp.arange(H)[None, :, None, None]

    def chunk(c):
        t0 = c * QC
        blk = t0 // BLK
        q_c = lax.dynamic_slice_in_dim(q, t0, QC, axis=2)
        gate = jnp.einsum('bhqd,bhnd->bhqn', q_c, k_mean).astype(jnp.float32)
        gate = jnp.where(jnp.arange(n_blocks) < blk, gate, NEG)
        _, sel = lax.top_k(gate, top_k)
        sel_valid = sel < blk
        k_sel = k_blocks[b_ix, h_ix, sel]
        v_sel = v_blocks[b_ix, h_ix, sel]
        s_sel = jnp.einsum('bhqd,bhqnkd->bhqnk', q_c, k_sel).astype(jnp.float32) * scale
        s_sel = jnp.where(sel_valid[..., None], s_sel, NEG).reshape(B, H, QC, top_k * BLK)
        k_own = lax.dynamic_slice_in_dim(k_pad, blk * BLK, BLK, axis=2)
        v_own = lax.dynamic_slice_in_dim(v_pad, blk * BLK, BLK, axis=2)
        s_own = jnp.einsum('bhqd,bhkd->bhqk', q_c, k_own).astype(jnp.float32) * scale
        q_pos = t0 + jnp.arange(QC)
        k_pos = blk * BLK + jnp.arange(BLK)
        s_own = jnp.where(k_pos[None, :] <= q_pos[:, None], s_own, NEG)
        p = jax.nn.softmax(jnp.concatenate([s_sel, s_own], axis=-1), axis=-1).astype(x.dtype)
        p_sel = p[..., :top_k * BLK].reshape(B, H, QC, top_k, BLK)
        p_own = p[..., top_k * BLK:]
        return (jnp.einsum('bhqnk,bhqnkd->bhqd', p_sel, v_sel)
                + jnp.einsum('bhqk,bhkd->bhqd', p_own, v_own))

    out = lax.map(chunk, jnp.arange(S // QC))
    out = out.transpose(1, 0, 3, 2, 4).reshape(B, S, H * Dh)
    return out @ w_o


def setup_inputs(seed: int = 0) -> dict:
    key = jax.random.key(seed)
    ks = iter(jax.random.split(key, 40))
    f32 = jnp.float32
    D, F, H, N = D_MODEL, D_FF, RWKV_HEADS, RWKV_HEAD
    A, Bn = N_A_LAYERS, N_B_LAYERS
    HD = MOBA_HEADS * MOBA_HEAD_DIM

    def dense(shape, fan_in, scale=1.0):
        return jax.random.normal(next(ks), shape, f32) * (scale * fan_in ** -0.5)

    def noise(shape, s):
        return jax.random.normal(next(ks), shape, f32) * s

    ratio = jnp.arange(D, dtype=f32) / (D - 1)
    w0_base = -6.0 + 5.0 * ratio ** 0.85 + 0.5
    return {
        "x": jax.random.normal(next(ks), (BATCH, SEQ, D), f32),
        "a_mix": jax.random.uniform(next(ks), (A, 6, D), f32, 0.1, 0.9),
        "a_w_r": dense((A, D, D), D),
        "a_w_k": dense((A, D, D), D),
        "a_w_v": dense((A, D, D), D),
        "a_w_o": dense((A, D, D), D, BETA),
        "a_w0": w0_base[None, :] + noise((A, D), 0.1),
        "a_w1": dense((A, D, D_DECAY_LORA), D, 0.1),
        "a_w2": dense((A, D_DECAY_LORA, D), D_DECAY_LORA, 0.1),
        "a_a0": noise((A, D), 0.1),
        "a_a1": dense((A, D, D_AAA_LORA), D, 0.1),
        "a_a2": dense((A, D_AAA_LORA, D), D_AAA_LORA, 0.1),
        "a_g1": dense((A, D, D_GATE_LORA), D),
        "a_g2": dense((A, D_GATE_LORA, D), D_GATE_LORA),
        "a_k_k": 0.85 + noise((A, D), 0.02),
        "a_k_a": 1.0 + noise((A, D), 0.02),
        "a_r_k": -0.04 + noise((A, H, N), 0.02),
        "a_lnx_w": 1.0 + noise((A, D), 0.02),
        "a_lnx_b": noise((A, D), 0.02),
        "kv_w_k": dense((D, HD), D),
        "kv_w_v": dense((D, HD), D),
        "b_w_q": dense((Bn, D, HD), D),
        "b_w_o": dense((Bn, HD, D), HD, BETA),
        "ffn_w_gate": dense((DEPTH, D, F), D),
        "ffn_w_up": dense((DEPTH, D, F), D),
        "ffn_w_down": dense((DEPTH, F, D), F, BETA),
        "ln1_w": 1.0 + noise((DEPTH, D), 0.02),
        "ln1_b": noise((DEPTH, D), 0.02),
        "ln2_w": 1.0 + noise((DEPTH, D), 0.02),
        "ln2_b": noise((DEPTH, D), 0.02),
    }


def reference(x, a_mix, a_w_r, a_w_k, a_w_v, a_w_o, a_w0, a_w1, a_w2, a_a0, a_a1, a_a2,
              a_g1, a_g2, a_k_k, a_k_a, a_r_k, a_lnx_w, a_lnx_b, kv_w_k, kv_w_v,
              b_w_q, b_w_o, ffn_w_gate, ffn_w_up, ffn_w_down, ln1_w, ln1_b, ln2_w, ln2_b):
    h = x
    k_pad = v_pad = k_mean = None
    for layer in range(DEPTH):
        if layer < N_A_LAYERS:
            i = layer
            mixed = _rwkv7_time_mix(h, a_mix[i], a_w_r[i], a_w_k[i], a_w_v[i], a_w_o[i],
                                    a_w0[i], a_w1[i], a_w2[i], a_a0[i], a_a1[i], a_a2[i],
                                    a_g1[i], a_g2[i], a_k_k[i], a_k_a[i], a_r_k[i],
                                    a_lnx_w[i], a_lnx_b[i])
        else:
            j = layer - N_A_LAYERS
            mixed = _moba_attention(h, b_w_q[j], b_w_o[j], k_pad, v_pad, k_mean)
        h = _layernorm(ALPHA * h + mixed, ln1_w[layer], ln1_b[layer])
        h = _layernorm(ALPHA * h + _swiglu(h, ffn_w_gate[layer], ffn_w_up[layer], ffn_w_down[layer]),
                       ln2_w[layer], ln2_b[layer])
        if layer == N_A_LAYERS - 1:
            k_pad, v_pad, k_mean = _shared_kv(h, kv_w_k, kv_w_v)
    return h
```

```python
import functools

import jax
import jax.numpy as jnp
from jax import lax
from jax.experimental import pallas as pl
from jax.experimental.pallas import tpu as pltpu

D_MODEL = 2048
RWKV_HEAD = 64
MOBA_HEAD_DIM = 128
MOBA_HEADS = D_MODEL // MOBA_HEAD_DIM
MOBA_BLOCK = 256
MOBA_TOPK = 3
NEG = -1e30
GN_EPS = 64e-5
LN_EPS = 1e-5
DEPTH = 2
ALPHA = (2 * DEPTH) ** 0.25
LORA_PAD = 128

VMEM_LIMIT_BYTES = 56 * 1024 * 1024

TM = 512
SCAN_CHUNK = 64
SCAN_HEADS = 4

F32 = jnp.float32
BF16 = jnp.bfloat16
HIGHEST = lax.Precision.HIGHEST


def _params(*semantics):
    return pltpu.CompilerParams(dimension_semantics=semantics, vmem_limit_bytes=VMEM_LIMIT_BYTES)


def _dot(a, b):
    return jnp.dot(a.astype(BF16), b.astype(BF16), preferred_element_type=F32)


def _dot_nt(a, b):
    return lax.dot_general(a.astype(BF16), b.astype(BF16), (((1,), (1,)), ((), ())),
                           preferred_element_type=F32)


def _dot_f32(a, b):
    return jnp.dot(a, b, preferred_element_type=F32, precision=HIGHEST)


def _layernorm(t, w, b):
    mu = jnp.mean(t, axis=-1, keepdims=True)
    d = t - mu
    var = jnp.mean(d * d, axis=-1, keepdims=True)
    return d * lax.rsqrt(var + LN_EPS) * w + b


def _token_shift_delta(x_ref, prev_ref, seq_start):
    x = x_ref[...]
    rolled = pltpu.roll(x, 1, 0)
    prev_last = jnp.where(seq_start, 0.0, prev_ref[7:8, :])
    row = lax.broadcasted_iota(jnp.int32, x.shape, 0)
    x_prev = jnp.where(row == 0, prev_last, rolled)
    return x, x_prev - x


def _rkv_kernel(x_ref, prev_ref, mix_ref, w_ref, o_ref, xm_ref, *, tiles_per_seq):
    @pl.when(pl.program_id(2) == 0)
    def _():
        seq_start = (pl.program_id(0) % tiles_per_seq) == 0
        x, xx = _token_shift_delta(x_ref, prev_ref, seq_start)
        xm_ref[...] = (x + xx * mix_ref[0]).astype(BF16)

    o_ref[0] = jnp.dot(xm_ref[...], w_ref[0], preferred_element_type=F32)


def _rkv_proj(h, mix_rkv, w_rkv, seq_len, tm=TM, tn=1024):
    m, c = h.shape
    return pl.pallas_call(
        functools.partial(_rkv_kernel, tiles_per_seq=seq_len // tm),
        grid=(m // tm, 3, c // tn),
        in_specs=[
            pl.BlockSpec((tm, c), lambda i, p, j: (i, 0)),
            pl.BlockSpec((8, c), lambda i, p, j: (jnp.maximum(i * (tm // 8) - 1, 0), 0)),
            pl.BlockSpec((1, 1, c), lambda i, p, j: (p, 0, 0)),
            pl.BlockSpec((1, c, tn), lambda i, p, j: (p, 0, j)),
        ],
        out_specs=pl.BlockSpec((1, tm, tn), lambda i, p, j: (p, i, j)),
        out_shape=jax.ShapeDtypeStruct((3, m, c), F32),
        scratch_shapes=[pltpu.VMEM((tm, c), BF16)],
        compiler_params=_params("parallel", "arbitrary", "arbitrary"),
        name="rwkv_rkv_proj",
    )(h, h, mix_rkv, w_rkv)


def _lora_kernel(x_ref, prev_ref, mix_ref, w1_ref, w2_ref, w0_ref, a1_ref, a2_ref, a0_ref,
                 g1_ref, g2_ref, ld_ref, a_ref, g_ref, *, tiles_per_seq):
    seq_start = (pl.program_id(0) % tiles_per_seq) == 0
    x, xx = _token_shift_delta(x_ref, prev_ref, seq_start)
    xw = x + xx * mix_ref[0:1, :]
    xa = x + xx * mix_ref[1:2, :]
    xg = x + xx * mix_ref[2:3, :]
    w_pre = w0_ref[...] + _dot(jnp.tanh(_dot(xw, w1_ref[...])), w2_ref[...])
    w_log = -jax.nn.softplus(-w_pre) - 0.5
    ld_ref[...] = -jnp.exp(w_log)
    a_ref[...] = jax.nn.sigmoid(a0_ref[...] + _dot(_dot(xa, a1_ref[...]), a2_ref[...]))
    g_ref[...] = _dot(jax.nn.sigmoid(_dot(xg, g1_ref[...])), g2_ref[...])


def _lora_proj(h, mix_wag, w1, w2, w0, a1, a2, a0, g1, g2, seq_len, tm=TM):
    m, c = h.shape
    full = lambda arr: pl.BlockSpec(arr.shape, lambda i: (0,) * arr.ndim)
    row = pl.BlockSpec((tm, c), lambda i: (i, 0))
    return pl.pallas_call(
        functools.partial(_lora_kernel, tiles_per_seq=seq_len // tm),
        grid=(m // tm,),
        in_specs=[row, pl.BlockSpec((8, c), lambda i: (jnp.maximum(i * (tm // 8) - 1, 0), 0)),
                  full(mix_wag), full(w1), full(w2), full(w0), full(a1), full(a2), full(a0),
                  full(g1), full(g2)],
        out_specs=[row, row, row],
        out_shape=[jax.ShapeDtypeStruct((m, c), F32)] * 3,
        compiler_params=_params("parallel"),
        name="rwkv_lora_proj",
    )(h, h, mix_wag, w1, w2, w0, a1, a2, a0, g1, g2)


def _scan_kernel(r_ref, k_ref, v_ref, ld_ref, a_ref, g_ref, kk_ref, ka_ref, rk_ref, lnw_ref,
                 lnb_ref, o_ref, st_ref, *, chunk, heads):
    L, G, N = chunk, heads, RWKV_HEAD
    GN, GL = G * N, G * L

    @pl.when(pl.program_id(2) == 0)
    def _():
        st_ref[...] = jnp.zeros_like(st_ref)

    r, k, v, ld, a, g = r_ref[0], k_ref[0], v_ref[0], ld_ref[0], a_ref[0], g_ref[0]

    lane_head = lax.broadcasted_iota(jnp.int32, (L, GN), 1) // N
    seg_ones = (lax.broadcasted_iota(jnp.int32, (GN, GN), 0) // N
                == lax.broadcasted_iota(jnp.int32, (GN, GN), 1) // N)
    seg_ones_f = seg_ones.astype(F32)
    erow = lax.broadcasted_iota(jnp.int32, (GL, GN), 0) // L
    ecol = lax.broadcasted_iota(jnp.int32, (GL, GN), 1) // N
    expand_mask = erow == ecol
    prow = lax.broadcasted_iota(jnp.int32, (GL, GL), 0)
    pcol = lax.broadcasted_iota(jnp.int32, (GL, GL), 1)
    same_head = (prow // L) == (pcol // L)
    strict_mask = same_head & (pcol < prow)
    incl_mask = same_head & (pcol <= prow)
    tri_incl = (lax.broadcasted_iota(jnp.int32, (L, L), 1)
                <= lax.broadcasted_iota(jnp.int32, (L, L), 0)).astype(F32)

    def seg_sum(x):
        return _dot_f32(x, seg_ones_f)

    def tile_rows(x):
        return jnp.concatenate([x] * G, axis=0)

    def expand(x):
        return jnp.where(expand_mask, tile_rows(x), 0.0)

    def fold(x_exp):
        out = jnp.zeros((L, GN), F32)
        for hh in range(G):
            out = out + jnp.where(lane_head == hh, x_exp[hh * L:(hh + 1) * L, :], 0.0)
        return out

    kk = k * kk_ref[...]
    kk = kk / jnp.maximum(jnp.sqrt(seg_sum(kk * kk)), 1e-12)
    k_mod = k * (1.0 + (a - 1.0) * ka_ref[...])
    a_vec = -kk
    b_vec = kk * a

    cs = _dot_f32(tri_incl, ld)
    cs_last = cs[L - 1:L, :]
    e_neg = jnp.exp(-cs)
    e_tail = jnp.exp(cs_last - cs)
    at = a_vec * jnp.exp(cs - ld)
    rt = r * jnp.exp(cs)
    bt = b_vec * e_neg
    kt = k_mod * e_neg
    bh = b_vec * e_tail
    kh = k_mod * e_tail

    at_e, rt_e = expand(at), expand(rt)
    bt_t, kt_t = tile_rows(bt), tile_rows(kt)
    a_ab = jnp.where(strict_mask, _dot_nt(at_e, bt_t), 0.0)
    a_ak = jnp.where(strict_mask, _dot_nt(at_e, kt_t), 0.0)
    a_rb = jnp.where(incl_mask, _dot_nt(rt_e, bt_t), 0.0)
    a_rk = jnp.where(incl_mask, _dot_nt(rt_e, kt_t), 0.0)

    eye = (prow == pcol).astype(F32)
    t_inv = eye + a_ab
    p = a_ab
    for _ in range(L.bit_length() - 2):
        p = _dot(p, p)
        t_inv = t_inv + _dot(t_inv, p)

    s0 = st_ref[...]
    v_t = tile_rows(v)
    rhs = tile_rows(_dot_nt(at, s0)) + _dot(a_ak, v_t)
    u = fold(_dot(t_inv, rhs))
    y = fold(_dot(a_rb, tile_rows(u)) + _dot(a_rk, v_t)) + _dot_nt(rt, s0)

    uv = jnp.concatenate([u, v], axis=0)
    bk = jnp.concatenate([bh, kh], axis=0)
    upd = _dot(uv.T, bk)
    st_ref[...] = s0 * jnp.exp(cs_last) + jnp.where(seg_ones, upd, 0.0)

    inv_n = 1.0 / N
    mu = seg_sum(y) * inv_n
    d = y - mu
    var = seg_sum(d * d) * inv_n
    yn = d * lax.rsqrt(var + GN_EPS) * lnw_ref[...] + lnb_ref[...]
    bonus = seg_sum(r * k_mod * rk_ref[...]) * v
    o_ref[0] = ((yn + bonus) * g).astype(o_ref.dtype)


def _rwkv_scan(rkv, ld, a, g, k_k, k_a, r_k, lnx_w, lnx_b, batch, seq_len,
               chunk=SCAN_CHUNK, heads=SCAN_HEADS):
    c = ld.shape[-1]
    gn = heads * RWKV_HEAD
    rkv4 = rkv.reshape(3, batch, seq_len, c)
    tok = lambda arr: arr.reshape(batch, seq_len, c)
    seq = pl.BlockSpec((1, chunk, gn), lambda b, hg, t: (b, t, hg))
    vec = pl.BlockSpec((1, gn), lambda b, hg, t: (0, hg))
    rkv_spec = lambda idx: pl.BlockSpec((None, 1, chunk, gn), lambda b, hg, t: (idx, b, t, hg))
    return pl.pallas_call(
        functools.partial(_scan_kernel, chunk=chunk, heads=heads),
        grid=(batch, c // gn, seq_len // chunk),
        in_specs=[rkv_spec(0), rkv_spec(1), rkv_spec(2), seq, seq, seq, vec, vec, vec, vec, vec],
        out_specs=seq,
        out_shape=jax.ShapeDtypeStruct((batch, seq_len, c), BF16),
        scratch_shapes=[pltpu.VMEM((gn, gn), F32)],
        compiler_params=_params("parallel", "parallel", "arbitrary"),
        name="rwkv_scan",
    )(rkv4, rkv4, rkv4, tok(ld), tok(a), tok(g), k_k, k_a, r_k, lnx_w, lnx_b)


def _proj_ln_kernel(z_ref, w_ref, h_ref, lnw_ref, lnb_ref, o_ref):
    y = jnp.dot(z_ref[...], w_ref[...], preferred_element_type=F32)
    o_ref[...] = _layernorm(ALPHA * h_ref[...] + y, lnw_ref[...], lnb_ref[...])


def _proj_ln(z, w, h, ln_w, ln_b, tm=TM):
    m, kdim = z.shape
    c = w.shape[1]
    vec = pl.BlockSpec((1, c), lambda i: (0, 0))
    return pl.pallas_call(
        _proj_ln_kernel,
        grid=(m // tm,),
        in_specs=[pl.BlockSpec((tm, kdim), lambda i: (i, 0)),
                  pl.BlockSpec((kdim, c), lambda i: (0, 0)),
                  pl.BlockSpec((tm, c), lambda i: (i, 0)), vec, vec],
        out_specs=pl.BlockSpec((tm, c), lambda i: (i, 0)),
        out_shape=jax.ShapeDtypeStruct((m, c), F32),
        compiler_params=_params("parallel"),
        name="proj_residual_ln",
    )(z, w, h, ln_w, ln_b)


def _ffn_kernel(h_ref, wg_ref, wu_ref, wd_ref, lnw_ref, lnb_ref, o_ref, xb_ref, acc_ref):
    f = pl.program_id(1)

    @pl.when(f == 0)
    def _():
        xb_ref[...] = h_ref[...].astype(BF16)
        acc_ref[...] = jnp.zeros_like(acc_ref)

    x = xb_ref[...]
    gate = jnp.dot(x, wg_ref[...], preferred_element_type=F32)
    up = jnp.dot(x, wu_ref[...], preferred_element_type=F32)
    act = (gate * jax.nn.sigmoid(gate)) * up
    acc_ref[...] += jnp.dot(act.astype(BF16), wd_ref[...], preferred_element_type=F32)

    @pl.when(f == pl.num_programs(1) - 1)
    def _():
        o_ref[...] = _layernorm(ALPHA * h_ref[...] + acc_ref[...], lnw_ref[...], lnb_ref[...])


def _ffn_ln(h, w_gate, w_up, w_down, ln_w, ln_b, tm=TM, tf=512):
    m, c = h.shape
    d_ff = w_gate.shape[1]
    vec = pl.BlockSpec((1, c), lambda i, f: (0, 0))
    return pl.pallas_call(
        _ffn_kernel,
        grid=(m // tm, d_ff // tf),
        in_specs=[pl.BlockSpec((tm, c), lambda i, f: (i, 0)),
                  pl.BlockSpec((c, tf), lambda i, f: (0, f)),
                  pl.BlockSpec((c, tf), lambda i, f: (0, f)),
                  pl.BlockSpec((tf, c), lambda i, f: (f, 0)), vec, vec],
        out_specs=pl.BlockSpec((tm, c), lambda i, f: (i, 0)),
        out_shape=jax.ShapeDtypeStruct((m, c), F32),
        scratch_shapes=[pltpu.VMEM((tm, c), BF16), pltpu.VMEM((tm, c), F32)],
        compiler_params=_params("parallel", "arbitrary"),
        name="swiglu_ffn_ln",
    )(h, w_gate, w_up, w_down, ln_w, ln_b)


def _kv_kernel(x_ref, w_ref, o_ref, mean_ref, xb_ref):
    @pl.when(pl.program_id(1) == 0)
    def _():
        xb_ref[...] = x_ref[...].astype(BF16)

    y = jnp.dot(xb_ref[...], w_ref[...], preferred_element_type=F32)
    o_ref[...] = y.astype(o_ref.dtype)
    tm, tn = y.shape
    nblk = tm // MOBA_BLOCK
    mean_ref[0] = jnp.sum(y.reshape(nblk, MOBA_BLOCK, tn), axis=1) * (1.0 / MOBA_BLOCK)


def _kv_proj(h, w_kv, tm=TM, tn=1024):
    m, c = h.shape
    n = w_kv.shape[1]
    nblk = tm // MOBA_BLOCK
    return pl.pallas_call(
        _kv_kernel,
        grid=(m // tm, n // tn),
        in_specs=[pl.BlockSpec((tm, c), lambda i, j: (i, 0)),
                  pl.BlockSpec((c, tn), lambda i, j: (0, j))],
        out_specs=[pl.BlockSpec((tm, tn), lambda i, j: (i, j)),
                   pl.BlockSpec((1, nblk, tn), lambda i, j: (i, 0, j))],
        out_shape=[jax.ShapeDtypeStruct((m, n), BF16),
                   jax.ShapeDtypeStruct((m // tm, nblk, n), F32)],
        scratch_shapes=[pltpu.VMEM((tm, c), BF16)],
        compiler_params=_params("parallel", "arbitrary"),
        name="shared_kv_proj",
    )(h, w_kv)


def _q_kernel(x_ref, w_ref, o_ref, xb_ref):
    @pl.when(pl.program_id(1) == 0)
    def _():
        xb_ref[...] = x_ref[...].astype(BF16)

    o_ref[...] = jnp.dot(xb_ref[...], w_ref[...], preferred_element_type=F32)


def _q_proj(h, w_q, tm=TM, tn=1024):
    m, c = h.shape
    n = w_q.shape[1]
    return pl.pallas_call(
        _q_kernel,
        grid=(m // tm, n // tn),
        in_specs=[pl.BlockSpec((tm, c), lambda i, j: (i, 0)),
                  pl.BlockSpec((c, tn), lambda i, j: (0, j))],
        out_specs=pl.BlockSpec((tm, tn), lambda i, j: (i, j)),
        out_shape=jax.ShapeDtypeStruct((m, n), F32),
        scratch_shapes=[pltpu.VMEM((tm, c), BF16)],
        compiler_params=_params("parallel", "arbitrary"),
        name="moba_q_proj",
    )(h, w_q)


def _moba_kernel(q_ref, k_ref, v_ref, km_ref, o_ref, *, n_blocks, top_k):
    blk = pl.program_id(2)
    BLK, DH = MOBA_BLOCK, MOBA_HEAD_DIM
    scale = DH ** -0.5
    q = q_ref[...]
    qb = q.astype(BF16)

    gate = lax.dot_general(q, km_ref[0], (((1,), (1,)), ((), ())),
                           preferred_element_type=F32, precision=HIGHEST)
    col = lax.broadcasted_iota(jnp.int32, gate.shape, 1)
    rank = jnp.zeros(gate.shape, jnp.int32)
    for mm in range(n_blocks):
        gm = gate[:, mm:mm + 1]
        ahead = (gm > gate) | ((gm == gate) & (mm < col))
        rank = rank + jnp.where(ahead, 1, 0) * (mm < blk).astype(jnp.int32)
    sel = jnp.where((col < blk) & (rank < top_k), 1.0, 0.0)

    start = pl.multiple_of(blk * BLK, BLK)
    s = _dot_nt(qb, k_ref[pl.ds(start, BLK), :]) * scale
    qi = lax.broadcasted_iota(jnp.int32, s.shape, 0)
    ki = lax.broadcasted_iota(jnp.int32, s.shape, 1)
    s = jnp.where(ki <= qi, s, NEG)
    m0 = jnp.max(s, axis=-1, keepdims=True)
    p = jnp.exp(s - m0)
    l0 = jnp.sum(p, axis=-1, keepdims=True)
    acc0 = _dot(p, v_ref[pl.ds(start, BLK), :])

    def body(n, carry):
        m_i, l_i, acc = carry
        off = pl.multiple_of(n * BLK, BLK)
        sn = _dot_nt(qb, k_ref[pl.ds(off, BLK), :]) * scale
        chosen = jnp.sum(jnp.where(col == n, sel, 0.0), axis=-1, keepdims=True)
        sn = jnp.where(chosen > 0.0, sn, NEG)
        m_new = jnp.maximum(m_i, jnp.max(sn, axis=-1, keepdims=True))
        corr = jnp.exp(m_i - m_new)
        pn = jnp.exp(sn - m_new)
        l_new = corr * l_i + jnp.sum(pn, axis=-1, keepdims=True)
        acc_new = corr * acc + _dot(pn, v_ref[pl.ds(off, BLK), :])
        return m_new, l_new, acc_new

    m_f, l_f, acc_f = lax.fori_loop(0, blk, body, (m0, l0, acc0))
    o_ref[...] = (acc_f / l_f).astype(o_ref.dtype)


def _moba_attention(q, kv, k_mean, batch, seq_len):
    m, hd = q.shape
    n_blocks = seq_len // MOBA_BLOCK
    top_k = max(1, min(MOBA_TOPK, n_blocks - 1))
    H, DH, BLK = MOBA_HEADS, MOBA_HEAD_DIM, MOBA_BLOCK
    return pl.pallas_call(
        functools.partial(_moba_kernel, n_blocks=n_blocks, top_k=top_k),
        grid=(batch, H, n_blocks),
        in_specs=[pl.BlockSpec((BLK, DH), lambda b, h, t: (b * n_blocks + t, h)),
                  pl.BlockSpec((seq_len, DH), lambda b, h, t: (b, h)),
                  pl.BlockSpec((seq_len, DH), lambda b, h, t: (b, H + h)),
                  pl.BlockSpec((1, n_blocks, DH), lambda b, h, t: (b, 0, h))],
        out_specs=pl.BlockSpec((BLK, DH), lambda b, h, t: (b * n_blocks + t, h)),
        out_shape=jax.ShapeDtypeStruct((m, hd), BF16),
        compiler_params=_params("parallel", "parallel", "arbitrary"),
        name="moba_attention",
    )(q, kv, kv, k_mean)


def _pad_lora(w_in, w_out):
    rank = w_in.shape[1]
    pad = -rank % LORA_PAD
    return (jnp.pad(w_in, ((0, 0), (0, pad))).astype(BF16),
            jnp.pad(w_out, ((0, pad), (0, 0))).astype(BF16))


def kernel(x, a_mix, a_w_r, a_w_k, a_w_v, a_w_o, a_w0, a_w1, a_w2, a_a0, a_a1, a_a2, a_g1, a_g2,
           a_k_k, a_k_a, a_r_k, a_lnx_w, a_lnx_b, kv_w_k, kv_w_v, b_w_q, b_w_o, ffn_w_gate,
           ffn_w_up, ffn_w_down, ln1_w, ln1_b, ln2_w, ln2_b):
    batch, seq_len, c = x.shape
    n_a = a_mix.shape[0]
    n_layers = ffn_w_gate.shape[0]
    assert seq_len % MOBA_BLOCK == 0 and seq_len % TM == 0 and c == D_MODEL
    h = x.reshape(batch * seq_len, c)
    row = lambda vec: vec.reshape(1, c)
    kv = k_mean = None

    for layer in range(n_layers):
        if layer < n_a:
            i = layer
            mix = a_mix[i]
            w_rkv = jnp.stack([a_w_r[i], a_w_k[i], a_w_v[i]]).astype(BF16)
            rkv = _rkv_proj(h, jnp.stack([mix[0], mix[2], mix[3]]).reshape(3, 1, c), w_rkv,
                            seq_len)
            w1, w2 = _pad_lora(a_w1[i], a_w2[i])
            a1, a2 = _pad_lora(a_a1[i], a_a2[i])
            g1, g2 = _pad_lora(a_g1[i], a_g2[i])
            ld, a_gate, g = _lora_proj(h, jnp.stack([mix[1], mix[4], mix[5]]), w1, w2, row(a_w0[i]),
                                       a1, a2, row(a_a0[i]), g1, g2, seq_len)
            z = _rwkv_scan(rkv, ld, a_gate, g, row(a_k_k[i]), row(a_k_a[i]), row(a_r_k[i]),
                           row(a_lnx_w[i]), row(a_lnx_b[i]), batch, seq_len)
            z = z.reshape(batch * seq_len, c)
            w_o = a_w_o[i]
        else:
            j = layer - n_a
            q = _q_proj(h, b_w_q[j].astype(BF16))
            z = _moba_attention(q, kv, k_mean, batch, seq_len)
            w_o = b_w_o[j]
        h = _proj_ln(z, w_o.astype(BF16), h, row(ln1_w[layer]), row(ln1_b[layer]))
        h = _ffn_ln(h, ffn_w_gate[layer].astype(BF16), ffn_w_up[layer].astype(BF16),
                    ffn_w_down[layer].astype(BF16), row(ln2_w[layer]), row(ln2_b[layer]))
        if layer == n_a - 1:
            w_kv = jnp.concatenate([kv_w_k, kv_w_v], axis=1).astype(BF16)
            kv, k_mean = _kv_proj(h, w_kv)
            k_mean = k_mean.reshape(batch, seq_len // MOBA_BLOCK, -1)
    return h.reshape(batch, seq_len, c)
```

```python
import functools

import jax
import jax.numpy as jnp
from jax import lax
from jax.experimental import pallas as pl
from jax.experimental.pallas import tpu as pltpu

D_MODEL = 2048
RWKV_HEAD = 64
MOBA_HEAD_DIM = 128
MOBA_HEADS = D_MODEL // MOBA_HEAD_DIM
MOBA_BLOCK = 256
MOBA_TOPK = 3
NEG = -1e30
GN_EPS = 64e-5
LN_EPS = 1e-5
DEPTH = 2
ALPHA = (2 * DEPTH) ** 0.25
LORA_PAD = 128

VMEM_LIMIT_BYTES = 56 * 1024 * 1024

TM = 512
SCAN_CHUNK = 64
SCAN_HEADS = 4
EW_ROWS, EW_LANES = 512, 512

F32 = jnp.float32
BF16 = jnp.bfloat16
HIGHEST = lax.Precision.HIGHEST


def _params(*semantics):
    return pltpu.CompilerParams(dimension_semantics=semantics, vmem_limit_bytes=VMEM_LIMIT_BYTES)


def _dot(a, b):
    return jnp.dot(a.astype(BF16), b.astype(BF16), preferred_element_type=F32)


def _dot_nt(a, b):
    return lax.dot_general(a.astype(BF16), b.astype(BF16), (((1,), (1,)), ((), ())),
                           preferred_element_type=F32)


def _seg_sum(x, seg_ones):
    hi = x.astype(BF16)
    lo = (x - hi.astype(F32)).astype(BF16)
    return (jnp.dot(hi, seg_ones, preferred_element_type=F32)
            + jnp.dot(lo, seg_ones, preferred_element_type=F32))


def _seg_ones(lanes):
    head = jnp.arange(lanes) // RWKV_HEAD
    return (head[:, None] == head[None, :]).astype(BF16)


def _layernorm(t, w, b):
    mu = jnp.mean(t, axis=-1, keepdims=True)
    d = t - mu
    var = jnp.mean(d * d, axis=-1, keepdims=True)
    return d * lax.rsqrt(var + LN_EPS) * w + b


def _token_shift_delta(x_ref, prev_ref, seq_start):
    x = x_ref[...]
    rolled = pltpu.roll(x, 1, 0)
    prev_last = jnp.where(seq_start, 0.0, prev_ref[7:8, :])
    row = lax.broadcasted_iota(jnp.int32, x.shape, 0)
    x_prev = jnp.where(row == 0, prev_last, rolled)
    return x, x_prev - x


def _rkv_kernel(x_ref, prev_ref, mix_ref, w_ref, o_ref, xm_ref, *, tiles_per_seq):
    @pl.when(pl.program_id(2) == 0)
    def _():
        seq_start = (pl.program_id(0) % tiles_per_seq) == 0
        x, xx = _token_shift_delta(x_ref, prev_ref, seq_start)
        xm_ref[...] = (x + xx * mix_ref[0]).astype(BF16)

    o_ref[0] = jnp.dot(xm_ref[...], w_ref[0], preferred_element_type=F32)


def _rkv_proj(h, mix_rkv, w_rkv, seq_len, tm=TM, tn=1024):
    m, c = h.shape
    return pl.pallas_call(
        functools.partial(_rkv_kernel, tiles_per_seq=seq_len // tm),
        grid=(m // tm, 3, c // tn),
        in_specs=[
            pl.BlockSpec((tm, c), lambda i, p, j: (i, 0)),
            pl.BlockSpec((8, c), lambda i, p, j: (jnp.maximum(i * (tm // 8) - 1, 0), 0)),
            pl.BlockSpec((1, 1, c), lambda i, p, j: (p, 0, 0)),
            pl.BlockSpec((1, c, tn), lambda i, p, j: (p, 0, j)),
        ],
        out_specs=pl.BlockSpec((1, tm, tn), lambda i, p, j: (p, i, j)),
        out_shape=jax.ShapeDtypeStruct((3, m, c), F32),
        scratch_shapes=[pltpu.VMEM((tm, c), BF16)],
        compiler_params=_params("parallel", "arbitrary", "arbitrary"),
        name="rwkv_rkv_proj",
    )(h, h, mix_rkv, w_rkv)


def _lora_kernel(x_ref, prev_ref, mix_ref, w1_ref, w2_ref, w0_ref, a1_ref, a2_ref, a0_ref,
                 g1_ref, g2_ref, ld_ref, a_ref, g_ref, *, tiles_per_seq):
    seq_start = (pl.program_id(0) % tiles_per_seq) == 0
    x, xx = _token_shift_delta(x_ref, prev_ref, seq_start)
    xw = x + xx * mix_ref[0:1, :]
    xa = x + xx * mix_ref[1:2, :]
    xg = x + xx * mix_ref[2:3, :]
    w_pre = w0_ref[...] + _dot(jnp.tanh(_dot(xw, w1_ref[...])), w2_ref[...])
    w_log = -jax.nn.softplus(-w_pre) - 0.5
    ld_ref[...] = -jnp.exp(w_log)
    a_ref[...] = jax.nn.sigmoid(a0_ref[...] + _dot(_dot(xa, a1_ref[...]), a2_ref[...]))
    g_ref[...] = _dot(jax.nn.sigmoid(_dot(xg, g1_ref[...])), g2_ref[...])


def _lora_proj(h, mix_wag, w1, w2, w0, a1, a2, a0, g1, g2, seq_len, tm=TM):
    m, c = h.shape
    full = lambda arr: pl.BlockSpec(arr.shape, lambda i: (0,) * arr.ndim)
    row = pl.BlockSpec((tm, c), lambda i: (i, 0))
    return pl.pallas_call(
        functools.partial(_lora_kernel, tiles_per_seq=seq_len // tm),
        grid=(m // tm,),
        in_specs=[row, pl.BlockSpec((8, c), lambda i: (jnp.maximum(i * (tm // 8) - 1, 0), 0)),
                  full(mix_wag), full(w1), full(w2), full(w0), full(a1), full(a2), full(a0),
                  full(g1), full(g2)],
        out_specs=[row, row, row],
        out_shape=[jax.ShapeDtypeStruct((m, c), F32)] * 3,
        compiler_params=_params("parallel"),
        name="rwkv_lora_proj",
    )(h, h, mix_wag, w1, w2, w0, a1, a2, a0, g1, g2)


def _scan_prep_kernel(r_ref, k_ref, v_ref, ld_ref, a_ref, kk_ref, ka_ref, rk_ref, ones_ref,
                      at_ref, rt_ref, bt_ref, kt_ref, bh_ref, kh_ref, vb_ref, bonus_ref,
                      gam_ref, *, chunk):
    r, k, v, ld, a = r_ref[...], k_ref[...], v_ref[...], ld_ref[...], a_ref[...]
    rows, lanes = r.shape
    n_chunks = rows // chunk
    ones = ones_ref[...]

    kk = k * kk_ref[...]
    kk = kk / jnp.maximum(jnp.sqrt(_seg_sum(kk * kk, ones)), 1e-12)
    k_mod = k * (1.0 + (a - 1.0) * ka_ref[...])
    b_vec = kk * a

    pos = lax.broadcasted_iota(jnp.int32, (rows, lanes), 0) % chunk
    cs = ld
    step = 1
    while step < chunk:
        cs = cs + jnp.where(pos >= step, pltpu.roll(cs, step, 0), 0.0)
        step *= 2
    cs3 = cs.reshape(n_chunks, chunk, lanes)
    cs_last = cs3[:, chunk - 1:chunk, :]
    e_tail = jnp.exp(cs_last - cs3).reshape(rows, lanes)
    e_neg = jnp.exp(-cs)

    at_ref[...] = (-kk * jnp.exp(cs - ld)).astype(BF16)
    rt_ref[...] = (r * jnp.exp(cs)).astype(BF16)
    bt_ref[...] = (b_vec * e_neg).astype(BF16)
    kt_ref[...] = (k_mod * e_neg).astype(BF16)
    bh_ref[...] = (b_vec * e_tail).astype(BF16)
    kh_ref[...] = (k_mod * e_tail).astype(BF16)
    vb_ref[...] = v.astype(BF16)
    bonus_ref[...] = _seg_sum(r * k_mod * rk_ref[...], ones) * v
    gam_ref[...] = jnp.exp(cs_last).reshape(n_chunks, lanes)


def _scan_prep(rkv, ld, a, k_k, k_a, r_k, chunk, rows=EW_ROWS, lanes=EW_LANES):
    _, m, c = rkv.shape
    tile = pl.BlockSpec((rows, lanes), lambda i, j: (i, j))
    rkv_spec = lambda idx: pl.BlockSpec((None, rows, lanes), lambda i, j: (idx, i, j))
    vec = pl.BlockSpec((1, lanes), lambda i, j: (0, j))
    ones = _seg_ones(lanes)
    bf = jax.ShapeDtypeStruct((m, c), BF16)
    return pl.pallas_call(
        functools.partial(_scan_prep_kernel, chunk=chunk),
        grid=(m // rows, c // lanes),
        in_specs=[rkv_spec(0), rkv_spec(1), rkv_spec(2), tile, tile, vec, vec, vec,
                  pl.BlockSpec((lanes, lanes), lambda i, j: (0, 0))],
        out_specs=[tile] * 8 + [pl.BlockSpec((rows // chunk, lanes), lambda i, j: (i, j))],
        out_shape=[bf] * 7 + [jax.ShapeDtypeStruct((m, c), F32),
                              jax.ShapeDtypeStruct((m // chunk, c), F32)],
        compiler_params=_params("parallel", "parallel"),
        name="rwkv_scan_prep",
    )(rkv, rkv, rkv, ld, a, k_k, k_a, r_k, ones)


def _scan_kernel(at_ref, rt_ref, bt_ref, kt_ref, bh_ref, kh_ref, v_ref, gam_ref, y_ref, st_ref,
                 *, chunk, heads):
    L, G, N = chunk, heads, RWKV_HEAD
    GN, GL = G * N, G * L

    @pl.when(pl.program_id(2) == 0)
    def _():
        st_ref[...] = jnp.zeros_like(st_ref)

    at, rt, bt, kt = at_ref[0], rt_ref[0], bt_ref[0], kt_ref[0]
    bh, kh, v = bh_ref[0], kh_ref[0], v_ref[0]

    lane_head = lax.broadcasted_iota(jnp.int32, (L, GN), 1) // N
    seg_mask = (lax.broadcasted_iota(jnp.int32, (GN, GN), 0) // N
                == lax.broadcasted_iota(jnp.int32, (GN, GN), 1) // N)
    expand_mask = (lax.broadcasted_iota(jnp.int32, (GL, GN), 0) // L
                   == lax.broadcasted_iota(jnp.int32, (GL, GN), 1) // N)
    prow = lax.broadcasted_iota(jnp.int32, (GL, GL), 0)
    pcol = lax.broadcasted_iota(jnp.int32, (GL, GL), 1)
    same_head = (prow // L) == (pcol // L)
    strict_mask = same_head & (pcol < prow)
    incl_mask = same_head & (pcol <= prow)

    def tile_rows(x):
        return jnp.concatenate([x] * G, axis=0)

    def expand(x):
        return jnp.where(expand_mask, tile_rows(x), jnp.zeros((), x.dtype))

    def fold(x_exp):
        out = jnp.zeros((L, GN), F32)
        for hh in range(G):
            out = out + jnp.where(lane_head == hh, x_exp[hh * L:(hh + 1) * L, :], 0.0)
        return out

    at_e, rt_e = expand(at), expand(rt)
    bt_t, kt_t, v_t = tile_rows(bt), tile_rows(kt), tile_rows(v)
    a_ab = jnp.where(strict_mask, _dot_nt(at_e, bt_t), 0.0)
    a_ak = jnp.where(strict_mask, _dot_nt(at_e, kt_t), 0.0)
    a_rb = jnp.where(incl_mask, _dot_nt(rt_e, bt_t), 0.0)
    a_rk = jnp.where(incl_mask, _dot_nt(rt_e, kt_t), 0.0)

    t_inv = (prow == pcol).astype(F32) + a_ab
    p = a_ab
    for _ in range(L.bit_length() - 2):
        p = _dot(p, p)
        t_inv = t_inv + _dot(t_inv, p)

    s0 = st_ref[...]
    s0b = s0.astype(BF16)
    rhs = tile_rows(_dot_nt(at, s0b)) + _dot(a_ak, v_t)
    u = fold(_dot(t_inv, rhs))
    y = fold(_dot(a_rb, tile_rows(u)) + _dot(a_rk, v_t)) + _dot_nt(rt, s0b)
    y_ref[0] = y

    uv = jnp.concatenate([u, v.astype(F32)], axis=0)
    bk = jnp.concatenate([bh, kh], axis=0)
    upd = _dot(uv.T, bk)
    st_ref[...] = s0 * gam_ref[0] + jnp.where(seg_mask, upd, 0.0)


def _rwkv_scan(ops, gamma, batch, seq_len, chunk=SCAN_CHUNK, heads=SCAN_HEADS):
    c = gamma.shape[-1]
    gn = heads * RWKV_HEAD
    n_chunks = seq_len // chunk
    seq = pl.BlockSpec((1, chunk, gn), lambda b, hg, t: (b, t, hg))
    gam = pl.BlockSpec((1, 1, gn), lambda b, hg, t: (b * n_chunks + t, 0, hg))
    return pl.pallas_call(
        functools.partial(_scan_kernel, chunk=chunk, heads=heads),
        grid=(batch, c // gn, n_chunks),
        in_specs=[seq] * 7 + [gam],
        out_specs=seq,
        out_shape=jax.ShapeDtypeStruct((batch, seq_len, c), F32),
        scratch_shapes=[pltpu.VMEM((gn, gn), F32)],
        compiler_params=_params("parallel", "parallel", "arbitrary"),
        name="rwkv_scan",
    )(*[o.reshape(batch, seq_len, c) for o in ops], gamma.reshape(batch * n_chunks, 1, c))


def _gn_gate_kernel(y_ref, bonus_ref, g_ref, lnw_ref, lnb_ref, ones_ref, z_ref):
    y = y_ref[...]
    ones = ones_ref[...]
    inv_n = 1.0 / RWKV_HEAD
    mu = _seg_sum(y, ones) * inv_n
    d = y - mu
    var = _seg_sum(d * d, ones) * inv_n
    yn = d * lax.rsqrt(var + GN_EPS) * lnw_ref[...] + lnb_ref[...]
    z_ref[...] = ((yn + bonus_ref[...]) * g_ref[...]).astype(z_ref.dtype)


def _gn_gate(y, bonus, g, lnx_w, lnx_b, rows=EW_ROWS, lanes=EW_LANES):
    m, c = y.shape
    tile = pl.BlockSpec((rows, lanes), lambda i, j: (i, j))
    vec = pl.BlockSpec((1, lanes), lambda i, j: (0, j))
    return pl.pallas_call(
        _gn_gate_kernel,
        grid=(m // rows, c // lanes),
        in_specs=[tile, tile, tile, vec, vec, pl.BlockSpec((lanes, lanes), lambda i, j: (0, 0))],
        out_specs=tile,
        out_shape=jax.ShapeDtypeStruct((m, c), BF16),
        compiler_params=_params("parallel", "parallel"),
        name="rwkv_gn_gate",
    )(y, bonus, g, lnx_w, lnx_b, _seg_ones(lanes))


def _proj_ln_kernel(z_ref, w_ref, h_ref, lnw_ref, lnb_ref, o_ref):
    y = jnp.dot(z_ref[...], w_ref[...], preferred_element_type=F32)
    o_ref[...] = _layernorm(ALPHA * h_ref[...] + y, lnw_ref[...], lnb_ref[...])


def _proj_ln(z, w, h, ln_w, ln_b, tm=TM):
    m, kdim = z.shape
    c = w.shape[1]
    vec = pl.BlockSpec((1, c), lambda i: (0, 0))
    return pl.pallas_call(
        _proj_ln_kernel,
        grid=(m // tm,),
        in_specs=[pl.BlockSpec((tm, kdim), lambda i: (i, 0)),
                  pl.BlockSpec((kdim, c), lambda i: (0, 0)),
                  pl.BlockSpec((tm, c), lambda i: (i, 0)), vec, vec],
        out_specs=pl.BlockSpec((tm, c), lambda i: (i, 0)),
        out_shape=jax.ShapeDtypeStruct((m, c), F32),
        compiler_params=_params("parallel"),
        name="proj_residual_ln",
    )(z, w, h, ln_w, ln_b)


def _ffn_kernel(h_ref, wg_ref, wu_ref, wd_ref, lnw_ref, lnb_ref, o_ref, xb_ref, acc_ref):
    f = pl.program_id(1)

    @pl.when(f == 0)
    def _():
        xb_ref[...] = h_ref[...].astype(BF16)
        acc_ref[...] = jnp.zeros_like(acc_ref)

    x = xb_ref[...]
    gate = jnp.dot(x, wg_ref[...], preferred_element_type=F32)
    up = jnp.dot(x, wu_ref[...], preferred_element_type=F32)
    act = (gate * jax.nn.sigmoid(gate)) * up
    acc_ref[...] += jnp.dot(act.astype(BF16), wd_ref[...], preferred_element_type=F32)

    @pl.when(f == pl.num_programs(1) - 1)
    def _():
        o_ref[...] = _layernorm(ALPHA * h_ref[...] + acc_ref[...], lnw_ref[...], lnb_ref[...])


def _ffn_ln(h, w_gate, w_up, w_down, ln_w, ln_b, tm=TM, tf=512):
    m, c = h.shape
    d_ff = w_gate.shape[1]
    vec = pl.BlockSpec((1, c), lambda i, f: (0, 0))
    return pl.pallas_call(
        _ffn_kernel,
        grid=(m // tm, d_ff // tf),
        in_specs=[pl.BlockSpec((tm, c), lambda i, f: (i, 0)),
                  pl.BlockSpec((c, tf), lambda i, f: (0, f)),
                  pl.BlockSpec((c, tf), lambda i, f: (0, f)),
                  pl.BlockSpec((tf, c), lambda i, f: (f, 0)), vec, vec],
        out_specs=pl.BlockSpec((tm, c), lambda i, f: (i, 0)),
        out_shape=jax.ShapeDtypeStruct((m, c), F32),
        scratch_shapes=[pltpu.VMEM((tm, c), BF16), pltpu.VMEM((tm, c), F32)],
        compiler_params=_params("parallel", "arbitrary"),
        name="swiglu_ffn_ln",
    )(h, w_gate, w_up, w_down, ln_w, ln_b)


def _kv_kernel(x_ref, wk_ref, wvt_ref, k_ref, mean_ref, vt_ref, xb_ref):
    @pl.when(pl.program_id(1) == 0)
    def _():
        xb_ref[...] = x_ref[...].astype(BF16)

    xb = xb_ref[...]
    y = jnp.dot(xb, wk_ref[...], preferred_element_type=F32)
    k_ref[...] = y.astype(k_ref.dtype)
    tm, tn = y.shape
    nblk = tm // MOBA_BLOCK
    mean_ref[0] = jnp.sum(y.reshape(nblk, MOBA_BLOCK, tn), axis=1) * (1.0 / MOBA_BLOCK)
    vt = lax.dot_general(wvt_ref[...], xb, (((1,), (1,)), ((), ())), preferred_element_type=F32)
    for blk in range(nblk):
        vt_ref[blk] = vt[:, blk * MOBA_BLOCK:(blk + 1) * MOBA_BLOCK].astype(vt_ref.dtype)


def _kv_proj(h, w_k, w_v_t, tm=TM, tn=1024):
    m, c = h.shape
    n = w_k.shape[1]
    nblk = tm // MOBA_BLOCK
    return pl.pallas_call(
        _kv_kernel,
        grid=(m // tm, n // tn),
        in_specs=[pl.BlockSpec((tm, c), lambda i, j: (i, 0)),
                  pl.BlockSpec((c, tn), lambda i, j: (0, j)),
                  pl.BlockSpec((tn, c), lambda i, j: (j, 0))],
        out_specs=[pl.BlockSpec((tm, tn), lambda i, j: (i, j)),
                   pl.BlockSpec((1, nblk, tn), lambda i, j: (i, 0, j)),
                   pl.BlockSpec((nblk, tn, MOBA_BLOCK), lambda i, j: (i, j, 0))],
        out_shape=[jax.ShapeDtypeStruct((m, n), BF16),
                   jax.ShapeDtypeStruct((m // tm, nblk, n), F32),
                   jax.ShapeDtypeStruct((m // MOBA_BLOCK, n, MOBA_BLOCK), BF16)],
        scratch_shapes=[pltpu.VMEM((tm, c), BF16)],
        compiler_params=_params("parallel", "arbitrary"),
        name="shared_kv_proj",
    )(h, w_k, w_v_t)


def _q_kernel(x_ref, w_ref, o_ref, xb_ref):
    @pl.when(pl.program_id(1) == 0)
    def _():
        xb_ref[...] = x_ref[...].astype(BF16)

    o_ref[...] = jnp.dot(xb_ref[...], w_ref[...], preferred_element_type=F32)


def _q_proj(h, w_q, tm=TM, tn=1024):
    m, c = h.shape
    n = w_q.shape[1]
    return pl.pallas_call(
        _q_kernel,
        grid=(m // tm, n // tn),
        in_specs=[pl.BlockSpec((tm, c), lambda i, j: (i, 0)),
                  pl.BlockSpec((c, tn), lambda i, j: (0, j))],
        out_specs=pl.BlockSpec((tm, tn), lambda i, j: (i, j)),
        out_shape=jax.ShapeDtypeStruct((m, n), F32),
        scratch_shapes=[pltpu.VMEM((tm, c), BF16)],
        compiler_params=_params("parallel", "arbitrary"),
        name="moba_q_proj",
    )(h, w_q)


def _moba_kernel(q_ref, k_ref, vt_ref, km_ref, o_ref, sel_ref, *, n_blocks, top_k):
    blk = pl.program_id(2)
    BLK, DH = MOBA_BLOCK, MOBA_HEAD_DIM
    q = q_ref[...]
    qb = (q * DH ** -0.5).astype(BF16)

    gate = lax.dot_general(km_ref[0], q, (((1,), (1,)), ((), ())),
                           preferred_element_type=F32, precision=HIGHEST)
    bi = lax.broadcasted_iota(jnp.int32, gate.shape, 0)
    rank = jnp.zeros(gate.shape, jnp.int32)
    for mm in range(n_blocks):
        gm = gate[mm:mm + 1, :]
        ahead = (gm > gate) | ((gm == gate) & (mm < bi))
        rank = rank + jnp.where(ahead, 1, 0) * (mm < blk).astype(jnp.int32)
    sel_ref[...] = jnp.where((bi < blk) & (rank < top_k), 1.0, 0.0)

    start = pl.multiple_of(blk * BLK, BLK)
    s = _dot_nt(k_ref[pl.ds(start, BLK), :], qb)
    ki = lax.broadcasted_iota(jnp.int32, s.shape, 0)
    qi = lax.broadcasted_iota(jnp.int32, s.shape, 1)
    s = jnp.where(ki <= qi, s, NEG)
    m0 = jnp.max(s, axis=0, keepdims=True)
    p = jnp.exp(s - m0)
    l0 = jnp.sum(p, axis=0, keepdims=True)
    acc0 = jnp.dot(vt_ref[blk], p.astype(BF16), preferred_element_type=F32)

    def body(n, carry):
        m_i, l_i, acc = carry
        off = pl.multiple_of(n * BLK, BLK)
        sn = _dot_nt(k_ref[pl.ds(off, BLK), :], qb)
        sn = jnp.where(sel_ref[pl.ds(n, 1), :] > 0.0, sn, NEG)
        m_new = jnp.maximum(m_i, jnp.max(sn, axis=0, keepdims=True))
        corr = jnp.exp(m_i - m_new)
        pn = jnp.exp(sn - m_new)
        l_new = corr * l_i + jnp.sum(pn, axis=0, keepdims=True)
        acc_new = corr * acc + jnp.dot(vt_ref[n], pn.astype(BF16), preferred_element_type=F32)
        return m_new, l_new, acc_new

    m_f, l_f, acc_f = lax.fori_loop(0, blk, body, (m0, l0, acc0))
    o_ref[...] = (acc_f / l_f).T.astype(o_ref.dtype)


def _moba_attention(q, k, v_t, k_mean, batch, seq_len):
    m, hd = q.shape
    n_blocks = seq_len // MOBA_BLOCK
    top_k = max(1, min(MOBA_TOPK, n_blocks - 1))
    H, DH, BLK = MOBA_HEADS, MOBA_HEAD_DIM, MOBA_BLOCK
    return pl.pallas_call(
        functools.partial(_moba_kernel, n_blocks=n_blocks, top_k=top_k),
        grid=(batch, H, n_blocks),
        in_specs=[pl.BlockSpec((BLK, DH), lambda b, h, t: (b * n_blocks + t, h)),
                  pl.BlockSpec((seq_len, DH), lambda b, h, t: (b, h)),
                  pl.BlockSpec((n_blocks, DH, BLK), lambda b, h, t: (b, h, 0)),
                  pl.BlockSpec((1, n_blocks, DH), lambda b, h, t: (b, 0, h))],
        out_specs=pl.BlockSpec((BLK, DH), lambda b, h, t: (b * n_blocks + t, h)),
        out_shape=jax.ShapeDtypeStruct((m, hd), BF16),
        scratch_shapes=[pltpu.VMEM((n_blocks, BLK), F32)],
        compiler_params=_params("parallel", "parallel", "arbitrary"),
        name="moba_attention",
    )(q, k, v_t, k_mean)


def _pad_lora(w_in, w_out):
    rank = w_in.shape[1]
    pad = -rank % LORA_PAD
    return (jnp.pad(w_in, ((0, 0), (0, pad))).astype(BF16),
            jnp.pad(w_out, ((0, pad), (0, 0))).astype(BF16))


def kernel(x, a_mix, a_w_r, a_w_k, a_w_v, a_w_o, a_w0, a_w1, a_w2, a_a0, a_a1, a_a2, a_g1, a_g2,
           a_k_k, a_k_a, a_r_k, a_lnx_w, a_lnx_b, kv_w_k, kv_w_v, b_w_q, b_w_o, ffn_w_gate,
           ffn_w_up, ffn_w_down, ln1_w, ln1_b, ln2_w, ln2_b):
    batch, seq_len, c = x.shape
    n_a = a_mix.shape[0]
    n_layers = ffn_w_gate.shape[0]
    assert seq_len % MOBA_BLOCK == 0 and seq_len % TM == 0 and c == D_MODEL
    h = x.reshape(batch * seq_len, c)
    row = lambda vec: vec.reshape(1, c)
    k_all = v_t = k_mean = None

    for layer in range(n_layers):
        if layer < n_a:
            i = layer
            mix = a_mix[i]
            w_rkv = jnp.stack([a_w_r[i], a_w_k[i], a_w_v[i]]).astype(BF16)
            rkv = _rkv_proj(h, jnp.stack([mix[0], mix[2], mix[3]]).reshape(3, 1, c), w_rkv,
                            seq_len)
            w1, w2 = _pad_lora(a_w1[i], a_w2[i])
            a1, a2 = _pad_lora(a_a1[i], a_a2[i])
            g1, g2 = _pad_lora(a_g1[i], a_g2[i])
            ld, a_gate, g = _lora_proj(h, jnp.stack([mix[1], mix[4], mix[5]]), w1, w2, row(a_w0[i]),
                                       a1, a2, row(a_a0[i]), g1, g2, seq_len)
            *ops, bonus, gamma = _scan_prep(rkv, ld, a_gate, row(a_k_k[i]), row(a_k_a[i]),
                                            row(a_r_k[i]), SCAN_CHUNK)
            y = _rwkv_scan(ops, gamma, batch, seq_len).reshape(batch * seq_len, c)
            z = _gn_gate(y, bonus, g, row(a_lnx_w[i]), row(a_lnx_b[i]))
            w_o = a_w_o[i]
        else:
            j = layer - n_a
            q = _q_proj(h, b_w_q[j].astype(BF16))
            z = _moba_attention(q, k_all, v_t, k_mean, batch, seq_len)
            w_o = b_w_o[j]
        h = _proj_ln(z, w_o.astype(BF16), h, row(ln1_w[layer]), row(ln1_b[layer]))
        h = _ffn_ln(h, ffn_w_gate[layer].astype(BF16), ffn_w_up[layer].astype(BF16),
                    ffn_w_down[layer].astype(BF16), row(ln2_w[layer]), row(ln2_b[layer]))
        if layer == n_a - 1:
            k_all, k_mean, v_t = _kv_proj(h, kv_w_k.astype(BF16), kv_w_v.T.astype(BF16))
            k_mean = k_mean.reshape(batch, seq_len // MOBA_BLOCK, -1)
    return h.reshape(batch, seq_len, c)
```

```python
import functools

import jax
import jax.numpy as jnp
from jax import lax
from jax.experimental import pallas as pl
from jax.experimental.pallas import tpu as pltpu

D_MODEL = 2048
RWKV_HEAD = 64
MOBA_HEAD_DIM = 128
MOBA_HEADS = D_MODEL // MOBA_HEAD_DIM
MOBA_BLOCK = 256
MOBA_TOPK = 3
NEG = -1e30
GN_EPS = 64e-5
LN_EPS = 1e-5
DEPTH = 2
ALPHA = (2 * DEPTH) ** 0.25
LORA_PAD = 128

VMEM_LIMIT_BYTES = 56 * 1024 * 1024

TM = 512
SCAN_CHUNK = 64
SCAN_HEADS = 4
SCAN_GROUPS = 4
MOBA_HEADS_PER_STEP = 4
EW_ROWS, EW_LANES = 512, 512

F32 = jnp.float32
BF16 = jnp.bfloat16
HIGHEST = lax.Precision.HIGHEST


def _params(*semantics):
    return pltpu.CompilerParams(dimension_semantics=semantics, vmem_limit_bytes=VMEM_LIMIT_BYTES)


def _dot(a, b):
    return jnp.dot(a.astype(BF16), b.astype(BF16), preferred_element_type=F32)


def _dot_nt(a, b):
    return lax.dot_general(a.astype(BF16), b.astype(BF16), (((1,), (1,)), ((), ())),
                           preferred_element_type=F32)


def _seg_sum(x, seg_ones):
    hi = x.astype(BF16)
    lo = (x - hi.astype(F32)).astype(BF16)
    return (jnp.dot(hi, seg_ones, preferred_element_type=F32)
            + jnp.dot(lo, seg_ones, preferred_element_type=F32))


def _seg_ones(lanes):
    head = jnp.arange(lanes) // RWKV_HEAD
    return (head[:, None] == head[None, :]).astype(BF16)


def _layernorm(t, w, b):
    mu = jnp.mean(t, axis=-1, keepdims=True)
    d = t - mu
    var = jnp.mean(d * d, axis=-1, keepdims=True)
    return d * lax.rsqrt(var + LN_EPS) * w + b


def _token_shift_delta(x_ref, prev_ref, seq_start):
    x = x_ref[...]
    rolled = pltpu.roll(x, 1, 0)
    prev_last = jnp.where(seq_start, 0.0, prev_ref[7:8, :])
    row = lax.broadcasted_iota(jnp.int32, x.shape, 0)
    x_prev = jnp.where(row == 0, prev_last, rolled)
    return x, x_prev - x


def _rkv_kernel(x_ref, prev_ref, mix_ref, w_ref, o_ref, xm_ref, *, tiles_per_seq):
    @pl.when(pl.program_id(2) == 0)
    def _():
        seq_start = (pl.program_id(0) % tiles_per_seq) == 0
        x, xx = _token_shift_delta(x_ref, prev_ref, seq_start)
        xm_ref[...] = (x + xx * mix_ref[0]).astype(BF16)

    o_ref[0] = jnp.dot(xm_ref[...], w_ref[0], preferred_element_type=F32)


def _rkv_proj(h, mix_rkv, w_rkv, seq_len, tm=TM, tn=1024):
    m, c = h.shape
    return pl.pallas_call(
        functools.partial(_rkv_kernel, tiles_per_seq=seq_len // tm),
        grid=(m // tm, 3, c // tn),
        in_specs=[
            pl.BlockSpec((tm, c), lambda i, p, j: (i, 0)),
            pl.BlockSpec((8, c), lambda i, p, j: (jnp.maximum(i * (tm // 8) - 1, 0), 0)),
            pl.BlockSpec((1, 1, c), lambda i, p, j: (p, 0, 0)),
            pl.BlockSpec((1, c, tn), lambda i, p, j: (p, 0, j)),
        ],
        out_specs=pl.BlockSpec((1, tm, tn), lambda i, p, j: (p, i, j)),
        out_shape=jax.ShapeDtypeStruct((3, m, c), F32),
        scratch_shapes=[pltpu.VMEM((tm, c), BF16)],
        compiler_params=_params("parallel", "arbitrary", "arbitrary"),
        name="rwkv_rkv_proj",
    )(h, h, mix_rkv, w_rkv)


def _lora_kernel(x_ref, prev_ref, mix_ref, w1_ref, w2_ref, w0_ref, a1_ref, a2_ref, a0_ref,
                 g1_ref, g2_ref, ld_ref, a_ref, g_ref, *, tiles_per_seq):
    seq_start = (pl.program_id(0) % tiles_per_seq) == 0
    x, xx = _token_shift_delta(x_ref, prev_ref, seq_start)
    xw = x + xx * mix_ref[0:1, :]
    xa = x + xx * mix_ref[1:2, :]
    xg = x + xx * mix_ref[2:3, :]
    w_pre = w0_ref[...] + _dot(jnp.tanh(_dot(xw, w1_ref[...])), w2_ref[...])
    w_log = -jax.nn.softplus(-w_pre) - 0.5
    ld_ref[...] = -jnp.exp(w_log)
    a_ref[...] = jax.nn.sigmoid(a0_ref[...] + _dot(_dot(xa, a1_ref[...]), a2_ref[...]))
    g_ref[...] = _dot(jax.nn.sigmoid(_dot(xg, g1_ref[...])), g2_ref[...])


def _lora_proj(h, mix_wag, w1, w2, w0, a1, a2, a0, g1, g2, seq_len, tm=TM):
    m, c = h.shape
    full = lambda arr: pl.BlockSpec(arr.shape, lambda i: (0,) * arr.ndim)
    row = pl.BlockSpec((tm, c), lambda i: (i, 0))
    return pl.pallas_call(
        functools.partial(_lora_kernel, tiles_per_seq=seq_len // tm),
        grid=(m // tm,),
        in_specs=[row, pl.BlockSpec((8, c), lambda i: (jnp.maximum(i * (tm // 8) - 1, 0), 0)),
                  full(mix_wag), full(w1), full(w2), full(w0), full(a1), full(a2), full(a0),
                  full(g1), full(g2)],
        out_specs=[row, row, row],
        out_shape=[jax.ShapeDtypeStruct((m, c), F32)] * 3,
        compiler_params=_params("parallel"),
        name="rwkv_lora_proj",
    )(h, h, mix_wag, w1, w2, w0, a1, a2, a0, g1, g2)


def _scan_prep_kernel(r_ref, k_ref, v_ref, ld_ref, a_ref, kk_ref, ka_ref, rk_ref, ones_ref,
                      at_ref, rt_ref, bt_ref, kt_ref, bh_ref, kh_ref, vb_ref, bonus_ref,
                      gam_ref, *, chunk):
    r, k, v, ld, a = r_ref[...], k_ref[...], v_ref[...], ld_ref[...], a_ref[...]
    rows, lanes = r.shape
    n_chunks = rows // chunk
    ones = ones_ref[...]

    kk = k * kk_ref[...]
    kk = kk / jnp.maximum(jnp.sqrt(_seg_sum(kk * kk, ones)), 1e-12)
    k_mod = k * (1.0 + (a - 1.0) * ka_ref[...])
    b_vec = kk * a

    pos = lax.broadcasted_iota(jnp.int32, (rows, lanes), 0) % chunk
    cs = ld
    step = 1
    while step < chunk:
        cs = cs + jnp.where(pos >= step, pltpu.roll(cs, step, 0), 0.0)
        step *= 2
    cs3 = cs.reshape(n_chunks, chunk, lanes)
    cs_last = cs3[:, chunk - 1:chunk, :]
    e_tail = jnp.exp(cs_last - cs3).reshape(rows, lanes)
    e_neg = jnp.exp(-cs)

    at_ref[...] = (-kk * jnp.exp(cs - ld)).astype(BF16)
    rt_ref[...] = (r * jnp.exp(cs)).astype(BF16)
    bt_ref[...] = (b_vec * e_neg).astype(BF16)
    kt_ref[...] = (k_mod * e_neg).astype(BF16)
    bh_ref[...] = (b_vec * e_tail).astype(BF16)
    kh_ref[...] = (k_mod * e_tail).astype(BF16)
    vb_ref[...] = v.astype(BF16)
    bonus_ref[...] = _seg_sum(r * k_mod * rk_ref[...], ones) * v
    gam_ref[...] = jnp.exp(cs_last).reshape(n_chunks, lanes)


def _scan_prep(rkv, ld, a, k_k, k_a, r_k, chunk, rows=EW_ROWS, lanes=EW_LANES):
    _, m, c = rkv.shape
    tile = pl.BlockSpec((rows, lanes), lambda i, j: (i, j))
    rkv_spec = lambda idx: pl.BlockSpec((None, rows, lanes), lambda i, j: (idx, i, j))
    vec = pl.BlockSpec((1, lanes), lambda i, j: (0, j))
    ones = _seg_ones(lanes)
    bf = jax.ShapeDtypeStruct((m, c), BF16)
    return pl.pallas_call(
        functools.partial(_scan_prep_kernel, chunk=chunk),
        grid=(m // rows, c // lanes),
        in_specs=[rkv_spec(0), rkv_spec(1), rkv_spec(2), tile, tile, vec, vec, vec,
                  pl.BlockSpec((lanes, lanes), lambda i, j: (0, 0))],
        out_specs=[tile] * 8 + [pl.BlockSpec((rows // chunk, lanes), lambda i, j: (i, j))],
        out_shape=[bf] * 7 + [jax.ShapeDtypeStruct((m, c), F32),
                              jax.ShapeDtypeStruct((m // chunk, c), F32)],
        compiler_params=_params("parallel", "parallel"),
        name="rwkv_scan_prep",
    )(rkv, rkv, rkv, ld, a, k_k, k_a, r_k, ones)


def _scan_kernel(at_ref, rt_ref, bt_ref, kt_ref, bh_ref, kh_ref, v_ref, gam_ref, y_ref, st_ref,
                 *, chunk, heads, groups):
    L, G, N = chunk, heads, RWKV_HEAD
    GN, GL = G * N, G * L

    @pl.when(pl.program_id(2) == 0)
    def _():
        st_ref[...] = jnp.zeros_like(st_ref)

    lane_head = lax.broadcasted_iota(jnp.int32, (L, GN), 1) // N
    seg_mask = (lax.broadcasted_iota(jnp.int32, (GN, GN), 0) // N
                == lax.broadcasted_iota(jnp.int32, (GN, GN), 1) // N)
    expand_mask = (lax.broadcasted_iota(jnp.int32, (GL, GN), 0) // L
                   == lax.broadcasted_iota(jnp.int32, (GL, GN), 1) // N)
    prow = lax.broadcasted_iota(jnp.int32, (GL, GL), 0)
    pcol = lax.broadcasted_iota(jnp.int32, (GL, GL), 1)
    same_head = (prow // L) == (pcol // L)
    strict_mask = same_head & (pcol < prow)
    incl_mask = same_head & (pcol <= prow)

    def tile_rows(x):
        return jnp.concatenate([x] * G, axis=0)

    def expand(x):
        return jnp.where(expand_mask, tile_rows(x), jnp.zeros((), x.dtype))

    def fold(x_exp):
        out = jnp.zeros((L, GN), F32)
        for hh in range(G):
            out = out + jnp.where(lane_head == hh, x_exp[hh * L:(hh + 1) * L, :], 0.0)
        return out

    eye = (prow == pcol).astype(F32)

    R = range(groups)
    lanes = [slice(gi * GN, (gi + 1) * GN) for gi in R]
    load = lambda ref: [ref[0, :, lanes[gi]] for gi in R]
    at, rt, bt, kt, bh, kh, v = (load(ref) for ref in
                                 (at_ref, rt_ref, bt_ref, kt_ref, bh_ref, kh_ref, v_ref))
    s0 = [st_ref[gi] for gi in R]
    gam = [gam_ref[0, :, lanes[gi]] for gi in R]

    at_e = [expand(x) for x in at]
    rt_e = [expand(x) for x in rt]
    bt_t = [tile_rows(x) for x in bt]
    kt_t = [tile_rows(x) for x in kt]
    v_t = [tile_rows(x) for x in v]
    a_ab = [jnp.where(strict_mask, _dot_nt(at_e[gi], bt_t[gi]), 0.0) for gi in R]
    a_ak = [jnp.where(strict_mask, _dot_nt(at_e[gi], kt_t[gi]), 0.0) for gi in R]
    a_rb = [jnp.where(incl_mask, _dot_nt(rt_e[gi], bt_t[gi]), 0.0) for gi in R]
    a_rk = [jnp.where(incl_mask, _dot_nt(rt_e[gi], kt_t[gi]), 0.0) for gi in R]

    t_inv = [eye + a_ab[gi] for gi in R]
    p = a_ab
    for _ in range(L.bit_length() - 2):
        p = [_dot(p[gi], p[gi]) for gi in R]
        t_inv = [t_inv[gi] + _dot(t_inv[gi], p[gi]) for gi in R]

    s0b = [x.astype(BF16) for x in s0]
    rhs = [tile_rows(_dot_nt(at[gi], s0b[gi])) + _dot(a_ak[gi], v_t[gi]) for gi in R]
    u = [fold(_dot(t_inv[gi], rhs[gi])) for gi in R]
    y = [fold(_dot(a_rb[gi], tile_rows(u[gi])) + _dot(a_rk[gi], v_t[gi]))
         + _dot_nt(rt[gi], s0b[gi]) for gi in R]
    uv = [jnp.concatenate([u[gi], v[gi].astype(F32)], axis=0) for gi in R]
    bk = [jnp.concatenate([bh[gi], kh[gi]], axis=0) for gi in R]
    upd = [_dot(uv[gi].T, bk[gi]) for gi in R]
    for gi in R:
        y_ref[0, :, lanes[gi]] = y[gi]
        st_ref[gi] = s0[gi] * gam[gi] + jnp.where(seg_mask, upd[gi], 0.0)


def _rwkv_scan(ops, gamma, batch, seq_len, chunk=SCAN_CHUNK, heads=SCAN_HEADS,
               groups=SCAN_GROUPS):
    c = gamma.shape[-1]
    gn = heads * RWKV_HEAD
    lanes = groups * gn
    n_chunks = seq_len // chunk
    seq = pl.BlockSpec((1, chunk, lanes), lambda b, hg, t: (b, t, hg))
    gam = pl.BlockSpec((1, 1, lanes), lambda b, hg, t: (b * n_chunks + t, 0, hg))
    return pl.pallas_call(
        functools.partial(_scan_kernel, chunk=chunk, heads=heads, groups=groups),
        grid=(batch, c // lanes, n_chunks),
        in_specs=[seq] * 7 + [gam],
        out_specs=seq,
        out_shape=jax.ShapeDtypeStruct((batch, seq_len, c), F32),
        scratch_shapes=[pltpu.VMEM((groups, gn, gn), F32)],
        compiler_params=_params("parallel", "parallel", "arbitrary"),
        name="rwkv_scan",
    )(*[o.reshape(batch, seq_len, c) for o in ops], gamma.reshape(batch * n_chunks, 1, c))


def _gn_gate_kernel(y_ref, bonus_ref, g_ref, lnw_ref, lnb_ref, ones_ref, z_ref):
    y = y_ref[...]
    ones = ones_ref[...]
    inv_n = 1.0 / RWKV_HEAD
    mu = _seg_sum(y, ones) * inv_n
    d = y - mu
    var = _seg_sum(d * d, ones) * inv_n
    yn = d * lax.rsqrt(var + GN_EPS) * lnw_ref[...] + lnb_ref[...]
    z_ref[...] = ((yn + bonus_ref[...]) * g_ref[...]).astype(z_ref.dtype)


def _gn_gate(y, bonus, g, lnx_w, lnx_b, rows=EW_ROWS, lanes=EW_LANES):
    m, c = y.shape
    tile = pl.BlockSpec((rows, lanes), lambda i, j: (i, j))
    vec = pl.BlockSpec((1, lanes), lambda i, j: (0, j))
    return pl.pallas_call(
        _gn_gate_kernel,
        grid=(m // rows, c // lanes),
        in_specs=[tile, tile, tile, vec, vec, pl.BlockSpec((lanes, lanes), lambda i, j: (0, 0))],
        out_specs=tile,
        out_shape=jax.ShapeDtypeStruct((m, c), BF16),
        compiler_params=_params("parallel", "parallel"),
        name="rwkv_gn_gate",
    )(y, bonus, g, lnx_w, lnx_b, _seg_ones(lanes))


def _proj_ln_kernel(z_ref, w_ref, h_ref, lnw_ref, lnb_ref, o_ref):
    y = jnp.dot(z_ref[...], w_ref[...], preferred_element_type=F32)
    o_ref[...] = _layernorm(ALPHA * h_ref[...] + y, lnw_ref[...], lnb_ref[...])


def _proj_ln(z, w, h, ln_w, ln_b, tm=TM):
    m, kdim = z.shape
    c = w.shape[1]
    vec = pl.BlockSpec((1, c), lambda i: (0, 0))
    return pl.pallas_call(
        _proj_ln_kernel,
        grid=(m // tm,),
        in_specs=[pl.BlockSpec((tm, kdim), lambda i: (i, 0)),
                  pl.BlockSpec((kdim, c), lambda i: (0, 0)),
                  pl.BlockSpec((tm, c), lambda i: (i, 0)), vec, vec],
        out_specs=pl.BlockSpec((tm, c), lambda i: (i, 0)),
        out_shape=jax.ShapeDtypeStruct((m, c), F32),
        compiler_params=_params("parallel"),
        name="proj_residual_ln",
    )(z, w, h, ln_w, ln_b)


def _ffn_kernel(h_ref, wg_ref, wu_ref, wd_ref, lnw_ref, lnb_ref, o_ref, xb_ref, acc_ref):
    f = pl.program_id(1)

    @pl.when(f == 0)
    def _():
        xb_ref[...] = h_ref[...].astype(BF16)
        acc_ref[...] = jnp.zeros_like(acc_ref)

    x = xb_ref[...]
    gate = jnp.dot(x, wg_ref[...], preferred_element_type=F32)
    up = jnp.dot(x, wu_ref[...], preferred_element_type=F32)
    act = (gate * jax.nn.sigmoid(gate)) * up
    acc_ref[...] += jnp.dot(act.astype(BF16), wd_ref[...], preferred_element_type=F32)

    @pl.when(f == pl.num_programs(1) - 1)
    def _():
        o_ref[...] = _layernorm(ALPHA * h_ref[...] + acc_ref[...], lnw_ref[...], lnb_ref[...])


def _ffn_ln(h, w_gate, w_up, w_down, ln_w, ln_b, tm=TM, tf=512):
    m, c = h.shape
    d_ff = w_gate.shape[1]
    vec = pl.BlockSpec((1, c), lambda i, f: (0, 0))
    return pl.pallas_call(
        _ffn_kernel,
        grid=(m // tm, d_ff // tf),
        in_specs=[pl.BlockSpec((tm, c), lambda i, f: (i, 0)),
                  pl.BlockSpec((c, tf), lambda i, f: (0, f)),
                  pl.BlockSpec((c, tf), lambda i, f: (0, f)),
                  pl.BlockSpec((tf, c), lambda i, f: (f, 0)), vec, vec],
        out_specs=pl.BlockSpec((tm, c), lambda i, f: (i, 0)),
        out_shape=jax.ShapeDtypeStruct((m, c), F32),
        scratch_shapes=[pltpu.VMEM((tm, c), BF16), pltpu.VMEM((tm, c), F32)],
        compiler_params=_params("parallel", "arbitrary"),
        name="swiglu_ffn_ln",
    )(h, w_gate, w_up, w_down, ln_w, ln_b)


def _kv_kernel(x_ref, wk_ref, wvt_ref, k_ref, mean_ref, vt_ref, xb_ref):
    @pl.when(pl.program_id(1) == 0)
    def _():
        xb_ref[...] = x_ref[...].astype(BF16)

    xb = xb_ref[...]
    y = jnp.dot(xb, wk_ref[...], preferred_element_type=F32)
    k_ref[...] = y.astype(k_ref.dtype)
    tm, tn = y.shape
    nblk = tm // MOBA_BLOCK
    mean_ref[0] = jnp.sum(y.reshape(nblk, MOBA_BLOCK, tn), axis=1) * (1.0 / MOBA_BLOCK)
    vt = lax.dot_general(wvt_ref[...], xb, (((1,), (1,)), ((), ())), preferred_element_type=F32)
    for blk in range(nblk):
        vt_ref[blk] = vt[:, blk * MOBA_BLOCK:(blk + 1) * MOBA_BLOCK].astype(vt_ref.dtype)


def _kv_proj(h, w_k, w_v_t, tm=TM, tn=1024):
    m, c = h.shape
    n = w_k.shape[1]
    nblk = tm // MOBA_BLOCK
    return pl.pallas_call(
        _kv_kernel,
        grid=(m // tm, n // tn),
        in_specs=[pl.BlockSpec((tm, c), lambda i, j: (i, 0)),
                  pl.BlockSpec((c, tn), lambda i, j: (0, j)),
                  pl.BlockSpec((tn, c), lambda i, j: (j, 0))],
        out_specs=[pl.BlockSpec((tm, tn), lambda i, j: (i, j)),
                   pl.BlockSpec((1, nblk, tn), lambda i, j: (i, 0, j)),
                   pl.BlockSpec((nblk, tn, MOBA_BLOCK), lambda i, j: (i, j, 0))],
        out_shape=[jax.ShapeDtypeStruct((m, n), BF16),
                   jax.ShapeDtypeStruct((m // tm, nblk, n), F32),
                   jax.ShapeDtypeStruct((m // MOBA_BLOCK, n, MOBA_BLOCK), BF16)],
        scratch_shapes=[pltpu.VMEM((tm, c), BF16)],
        compiler_params=_params("parallel", "arbitrary"),
        name="shared_kv_proj",
    )(h, w_k, w_v_t)


def _q_kernel(x_ref, w_ref, o_ref, xb_ref):
    @pl.when(pl.program_id(1) == 0)
    def _():
        xb_ref[...] = x_ref[...].astype(BF16)

    o_ref[...] = jnp.dot(xb_ref[...], w_ref[...], preferred_element_type=F32)


def _q_proj(h, w_q, tm=TM, tn=1024):
    m, c = h.shape
    n = w_q.shape[1]
    return pl.pallas_call(
        _q_kernel,
        grid=(m // tm, n // tn),
        in_specs=[pl.BlockSpec((tm, c), lambda i, j: (i, 0)),
                  pl.BlockSpec((c, tn), lambda i, j: (0, j))],
        out_specs=pl.BlockSpec((tm, tn), lambda i, j: (i, j)),
        out_shape=jax.ShapeDtypeStruct((m, n), F32),
        scratch_shapes=[pltpu.VMEM((tm, c), BF16)],
        compiler_params=_params("parallel", "arbitrary"),
        name="moba_q_proj",
    )(h, w_q)


def _moba_kernel(q_ref, k_ref, vt_ref, km_ref, o_ref, sel_ref, *, n_blocks, top_k, heads):
    blk = pl.program_id(2)
    BLK, DH = MOBA_BLOCK, MOBA_HEAD_DIM
    start = pl.multiple_of(blk * BLK, BLK)
    ki = lax.broadcasted_iota(jnp.int32, (BLK, BLK), 0)
    qi = lax.broadcasted_iota(jnp.int32, (BLK, BLK), 1)
    bi = lax.broadcasted_iota(jnp.int32, (n_blocks, BLK), 0)
    cols = [slice(hh * DH, (hh + 1) * DH) for hh in range(heads)]

    R = range(heads)
    q = [q_ref[:, cols[h]] for h in R]
    qb = [(q[h] * DH ** -0.5).astype(BF16) for h in R]

    gate = [lax.dot_general(km_ref[0, :, cols[h]], q[h], (((1,), (1,)), ((), ())),
                            preferred_element_type=F32, precision=HIGHEST) for h in R]
    rank = [jnp.zeros((n_blocks, BLK), jnp.int32) for h in R]
    for mm in range(n_blocks):
        past = (mm < blk).astype(jnp.int32)
        for h in R:
            gm = gate[h][mm:mm + 1, :]
            ahead = (gm > gate[h]) | ((gm == gate[h]) & (mm < bi))
            rank[h] = rank[h] + jnp.where(ahead, 1, 0) * past
    for h in R:
        sel_ref[h] = jnp.where((bi < blk) & (rank[h] < top_k), 1.0, 0.0)

    s = [_dot_nt(k_ref[pl.ds(start, BLK), cols[h]], qb[h]) for h in R]
    s = [jnp.where(ki <= qi, s[h], NEG) for h in R]
    m0 = [jnp.max(s[h], axis=0, keepdims=True) for h in R]
    p = [jnp.exp(s[h] - m0[h]) for h in R]
    l0 = [jnp.sum(p[h], axis=0, keepdims=True) for h in R]
    acc0 = [jnp.dot(vt_ref[blk, cols[h], :], p[h].astype(BF16), preferred_element_type=F32)
            for h in R]

    def body(n, carry):
        m_i, l_i, acc = carry
        off = pl.multiple_of(n * BLK, BLK)
        sn = [_dot_nt(k_ref[pl.ds(off, BLK), cols[h]], qb[h]) for h in R]
        sn = [jnp.where(sel_ref[h, pl.ds(n, 1), :] > 0.0, sn[h], NEG) for h in R]
        m_new = [jnp.maximum(m_i[h], jnp.max(sn[h], axis=0, keepdims=True)) for h in R]
        corr = [jnp.exp(m_i[h] - m_new[h]) for h in R]
        pn = [jnp.exp(sn[h] - m_new[h]) for h in R]
        l_new = [corr[h] * l_i[h] + jnp.sum(pn[h], axis=0, keepdims=True) for h in R]
        pv = [jnp.dot(vt_ref[n, cols[h], :], pn[h].astype(BF16), preferred_element_type=F32)
              for h in R]
        acc_new = [corr[h] * acc[h] + pv[h] for h in R]
        return m_new, l_new, acc_new

    _, l_f, acc_f = lax.fori_loop(0, blk, body, (m0, l0, acc0))
    for h in R:
        o_ref[:, cols[h]] = (acc_f[h] / l_f[h]).T.astype(o_ref.dtype)


def _moba_attention(q, k, v_t, k_mean, batch, seq_len, heads=MOBA_HEADS_PER_STEP):
    m, hd = q.shape
    n_blocks = seq_len // MOBA_BLOCK
    top_k = max(1, min(MOBA_TOPK, n_blocks - 1))
    BLK, W = MOBA_BLOCK, heads * MOBA_HEAD_DIM
    return pl.pallas_call(
        functools.partial(_moba_kernel, n_blocks=n_blocks, top_k=top_k, heads=heads),
        grid=(batch, hd // W, n_blocks),
        in_specs=[pl.BlockSpec((BLK, W), lambda b, h, t: (b * n_blocks + t, h)),
                  pl.BlockSpec((seq_len, W), lambda b, h, t: (b, h)),
                  pl.BlockSpec((n_blocks, W, BLK), lambda b, h, t: (b, h, 0)),
                  pl.BlockSpec((1, n_blocks, W), lambda b, h, t: (b, 0, h))],
        out_specs=pl.BlockSpec((BLK, W), lambda b, h, t: (b * n_blocks + t, h)),
        out_shape=jax.ShapeDtypeStruct((m, hd), BF16),
        scratch_shapes=[pltpu.VMEM((heads, n_blocks, BLK), F32)],
        compiler_params=_params("parallel", "parallel", "arbitrary"),
        name="moba_attention",
    )(q, k, v_t, k_mean)


def _pad_lora(w_in, w_out):
    rank = w_in.shape[1]
    pad = -rank % LORA_PAD
    return (jnp.pad(w_in, ((0, 0), (0, pad))).astype(BF16),
            jnp.pad(w_out, ((0, pad), (0, 0))).astype(BF16))


def kernel(x, a_mix, a_w_r, a_w_k, a_w_v, a_w_o, a_w0, a_w1, a_w2, a_a0, a_a1, a_a2, a_g1, a_g2,
           a_k_k, a_k_a, a_r_k, a_lnx_w, a_lnx_b, kv_w_k, kv_w_v, b_w_q, b_w_o, ffn_w_gate,
           ffn_w_up, ffn_w_down, ln1_w, ln1_b, ln2_w, ln2_b):
    batch, seq_len, c = x.shape
    n_a = a_mix.shape[0]
    n_layers = ffn_w_gate.shape[0]
    assert seq_len % MOBA_BLOCK == 0 and seq_len % TM == 0 and c == D_MODEL
    h = x.reshape(batch * seq_len, c)
    row = lambda vec: vec.reshape(1, c)
    k_all = v_t = k_mean = None

    for layer in range(n_layers):
        if layer < n_a:
            i = layer
            mix = a_mix[i]
            w_rkv = jnp.stack([a_w_r[i], a_w_k[i], a_w_v[i]]).astype(BF16)
            rkv = _rkv_proj(h, jnp.stack([mix[0], mix[2], mix[3]]).reshape(3, 1, c), w_rkv,
                            seq_len)
            w1, w2 = _pad_lora(a_w1[i], a_w2[i])
            a1, a2 = _pad_lora(a_a1[i], a_a2[i])
            g1, g2 = _pad_lora(a_g1[i], a_g2[i])
            ld, a_gate, g = _lora_proj(h, jnp.stack([mix[1], mix[4], mix[5]]), w1, w2, row(a_w0[i]),
                                       a1, a2, row(a_a0[i]), g1, g2, seq_len)
            *ops, bonus, gamma = _scan_prep(rkv, ld, a_gate, row(a_k_k[i]), row(a_k_a[i]),
                                            row(a_r_k[i]), SCAN_CHUNK)
            y = _rwkv_scan(ops, gamma, batch, seq_len).reshape(batch * seq_len, c)
            z = _gn_gate(y, bonus, g, row(a_lnx_w[i]), row(a_lnx_b[i]))
            w_o = a_w_o[i]
        else:
            j = layer - n_a
            q = _q_proj(h, b_w_q[j].astype(BF16))
            z = _moba_attention(q, k_all, v_t, k_mean, batch, seq_len)
            w_o = b_w_o[j]
        h = _proj_ln(z, w_o.astype(BF16), h, row(ln1_w[layer]), row(ln1_b[layer]))
        h = _ffn_ln(h, ffn_w_gate[layer].astype(BF16), ffn_w_up[layer].astype(BF16),
                    ffn_w_down[layer].astype(BF16), row(ln2_w[layer]), row(ln2_b[layer]))
        if layer == n_a - 1:
            k_all, k_mean, v_t = _kv_proj(h, kv_w_k.astype(BF16), kv_w_v.T.astype(BF16))
            k_mean = k_mean.reshape(batch, seq_len // MOBA_BLOCK, -1)
    return h.reshape(batch, seq_len, c)
```

```python
import functools

import jax
import jax.numpy as jnp
from jax import lax
from jax.experimental import pallas as pl
from jax.experimental.pallas import tpu as pltpu

D_MODEL = 2048
RWKV_HEAD = 64
MOBA_HEAD_DIM = 128
MOBA_HEADS = D_MODEL // MOBA_HEAD_DIM
MOBA_BLOCK = 256
MOBA_TOPK = 3
NEG = -1e30
GN_EPS = 64e-5
LN_EPS = 1e-5
DEPTH = 2
ALPHA = (2 * DEPTH) ** 0.25
LORA_PAD = 128

VMEM_LIMIT_BYTES = 56 * 1024 * 1024

TM = 512
SCAN_CHUNK = 64
SCAN_HEADS = 4
SCAN_GROUPS = 4
MOBA_HEADS_PER_STEP = 4
EW_ROWS, EW_LANES = 512, 512
SEG_LANES = 256

F32 = jnp.float32
BF16 = jnp.bfloat16
HIGHEST = lax.Precision.HIGHEST


def _params(*semantics):
    return pltpu.CompilerParams(dimension_semantics=semantics, vmem_limit_bytes=VMEM_LIMIT_BYTES)


def _dot(a, b):
    return jnp.dot(a.astype(BF16), b.astype(BF16), preferred_element_type=F32)


def _dot_nt(a, b):
    return lax.dot_general(a.astype(BF16), b.astype(BF16), (((1,), (1,)), ((), ())),
                           preferred_element_type=F32)


def _seg_sum(x, seg_ones, split=True):
    width = seg_ones.shape[0]
    parts = []
    for c0 in range(0, x.shape[1], width):
        xs = x[:, c0:c0 + width]
        hi = xs.astype(BF16)
        acc = jnp.dot(hi, seg_ones, preferred_element_type=F32)
        if split:
            lo = (xs - hi.astype(F32)).astype(BF16)
            acc = acc + jnp.dot(lo, seg_ones, preferred_element_type=F32)
        parts.append(acc)
    return parts[0] if len(parts) == 1 else jnp.concatenate(parts, axis=1)


def _seg_ones():
    head = jnp.arange(SEG_LANES) // RWKV_HEAD
    return (head[:, None] == head[None, :]).astype(BF16)


def _layernorm(t, w, b):
    mu = jnp.mean(t, axis=-1, keepdims=True)
    d = t - mu
    var = jnp.mean(d * d, axis=-1, keepdims=True)
    return d * lax.rsqrt(var + LN_EPS) * w + b


def _token_shift_delta(x_ref, prev_ref, seq_start):
    x = x_ref[...]
    rolled = pltpu.roll(x, 1, 0)
    prev_last = jnp.where(seq_start, 0.0, prev_ref[7:8, :])
    row = lax.broadcasted_iota(jnp.int32, x.shape, 0)
    x_prev = jnp.where(row == 0, prev_last, rolled)
    return x, x_prev - x


def _front_kernel(x_ref, prev_ref, mix_ref, wr_ref, wk_ref, wv_ref, w1_ref, w2_ref, w0_ref,
                  a1_ref, a2_ref, a0_ref, g1_ref, g2_ref, kk_ref, ka_ref, rk_ref, ones_ref,
                  at_ref, rt_ref, bt_ref, kt_ref, bh_ref, kh_ref, vb_ref, bonus_ref, g_ref,
                  gam_ref, xr_ref, xk_ref, xv_ref, hw_ref, ha_ref, hg_ref,
                  *, tiles_per_seq, chunk):
    @pl.when(pl.program_id(1) == 0)
    def _():
        seq_start = (pl.program_id(0) % tiles_per_seq) == 0
        x, xx = _token_shift_delta(x_ref, prev_ref, seq_start)
        mixed = lambda idx: x + xx * mix_ref[idx:idx + 1, :]
        xr_ref[...] = mixed(0).astype(BF16)
        xk_ref[...] = mixed(2).astype(BF16)
        xv_ref[...] = mixed(3).astype(BF16)
        hw_ref[...] = jnp.tanh(_dot(mixed(1), w1_ref[...])).astype(BF16)
        ha_ref[...] = _dot(mixed(4), a1_ref[...]).astype(BF16)
        hg_ref[...] = jax.nn.sigmoid(_dot(mixed(5), g1_ref[...])).astype(BF16)

    dot = functools.partial(jnp.dot, preferred_element_type=F32)
    r = dot(xr_ref[...], wr_ref[...])
    k = dot(xk_ref[...], wk_ref[...])
    v = dot(xv_ref[...], wv_ref[...])
    w_log = -jax.nn.softplus(-(w0_ref[...] + dot(hw_ref[...], w2_ref[...]))) - 0.5
    ld = -jnp.exp(w_log)
    a = jax.nn.sigmoid(a0_ref[...] + dot(ha_ref[...], a2_ref[...]))
    g_ref[...] = dot(hg_ref[...], g2_ref[...])

    rows, lanes = r.shape
    n_chunks = rows // chunk
    ones = ones_ref[...]
    kk = k * kk_ref[...]
    kk = kk / jnp.maximum(jnp.sqrt(_seg_sum(kk * kk, ones)), 1e-12)
    k_mod = k * (1.0 + (a - 1.0) * ka_ref[...])
    b_vec = kk * a

    pos = lax.broadcasted_iota(jnp.int32, (rows, lanes), 0) % chunk
    cs = ld
    step = 1
    while step < chunk:
        cs = cs + jnp.where(pos >= step, pltpu.roll(cs, step, 0), 0.0)
        step *= 2
    cs3 = cs.reshape(n_chunks, chunk, lanes)
    cs_last = cs3[:, chunk - 1:chunk, :]
    e_tail = jnp.exp(cs_last - cs3).reshape(rows, lanes)
    e_neg = jnp.exp(-cs)

    at_ref[...] = (-kk * jnp.exp(cs - ld)).astype(BF16)
    rt_ref[...] = (r * jnp.exp(cs)).astype(BF16)
    bt_ref[...] = (b_vec * e_neg).astype(BF16)
    kt_ref[...] = (k_mod * e_neg).astype(BF16)
    bh_ref[...] = (b_vec * e_tail).astype(BF16)
    kh_ref[...] = (k_mod * e_tail).astype(BF16)
    vb_ref[...] = v.astype(BF16)
    bonus_ref[...] = _seg_sum(r * k_mod * rk_ref[...], ones, split=False) * v
    gam_ref[...] = jnp.exp(cs_last).reshape(n_chunks, lanes)


def _rwkv_front(h, mix, w_r, w_k, w_v, w1, w2, w0, a1, a2, a0, g1, g2, k_k, k_a, r_k,
                seq_len, chunk, rows=EW_ROWS, lanes=EW_LANES):
    m, c = h.shape
    full = lambda arr: pl.BlockSpec(arr.shape, lambda i, j: (0,) * arr.ndim)
    cols = lambda arr: pl.BlockSpec((arr.shape[0], lanes), lambda i, j: (0, j))
    tile = pl.BlockSpec((rows, lanes), lambda i, j: (i, j))
    ones = _seg_ones()
    bf = jax.ShapeDtypeStruct((m, c), BF16)
    f32 = jax.ShapeDtypeStruct((m, c), F32)
    return pl.pallas_call(
        functools.partial(_front_kernel, tiles_per_seq=seq_len // rows, chunk=chunk),
        grid=(m // rows, c // lanes),
        in_specs=[pl.BlockSpec((rows, c), lambda i, j: (i, 0)),
                  pl.BlockSpec((8, c), lambda i, j: (jnp.maximum(i * (rows // 8) - 1, 0), 0)),
                  full(mix), cols(w_r), cols(w_k), cols(w_v),
                  full(w1), cols(w2), cols(w0), full(a1), cols(a2), cols(a0),
                  full(g1), cols(g2), cols(k_k), cols(k_a), cols(r_k), full(ones)],
        out_specs=[tile] * 9 + [pl.BlockSpec((rows // chunk, lanes), lambda i, j: (i, j))],
        out_shape=[bf] * 7 + [f32, f32, jax.ShapeDtypeStruct((m // chunk, c), F32)],
        scratch_shapes=[pltpu.VMEM((rows, c), BF16)] * 3
        + [pltpu.VMEM((rows, w1.shape[1]), BF16), pltpu.VMEM((rows, a1.shape[1]), BF16),
           pltpu.VMEM((rows, g1.shape[1]), BF16)],
        compiler_params=_params("parallel", "arbitrary"),
        name="rwkv_front",
    )(h, h, mix, w_r, w_k, w_v, w1, w2, w0, a1, a2, a0, g1, g2, k_k, k_a, r_k, ones)


def _scan_kernel(at_ref, rt_ref, bt_ref, kt_ref, bh_ref, kh_ref, v_ref, gam_ref, y_ref, st_ref,
                 *, chunk, heads, groups):
    L, G, N = chunk, heads, RWKV_HEAD
    GN, GL = G * N, G * L
    assert L == N

    @pl.when(pl.program_id(2) == 0)
    def _():
        st_ref[...] = jnp.zeros_like(st_ref)

    lane_head = lax.broadcasted_iota(jnp.int32, (L, GN), 1) // N
    seg_mask = (lax.broadcasted_iota(jnp.int32, (GN, GN), 0) // N
                == lax.broadcasted_iota(jnp.int32, (GN, GN), 1) // N)
    expand_mask = (lax.broadcasted_iota(jnp.int32, (GL, GN), 0) // L
                   == lax.broadcasted_iota(jnp.int32, (GL, GN), 1) // N)
    prow = lax.broadcasted_iota(jnp.int32, (GL, GL), 0)
    pcol = lax.broadcasted_iota(jnp.int32, (GL, GL), 1)
    same_head = (prow // L) == (pcol // L)
    strict_mask = same_head & (pcol < prow)
    incl_mask = same_head & (pcol <= prow)

    def tile_rows(x):
        return jnp.concatenate([x] * G, axis=0)

    def expand(x):
        return jnp.where(expand_mask, tile_rows(x), jnp.zeros((), x.dtype))

    def fold(x_exp):
        out = jnp.zeros((L, GN), F32)
        for hh in range(G):
            out = out + jnp.where(lane_head == hh, x_exp[hh * L:(hh + 1) * L, :], 0.0)
        return out

    eye_c = (lax.broadcasted_iota(jnp.int32, (L, GN), 1) % L
             == lax.broadcasted_iota(jnp.int32, (L, GN), 0)).astype(F32)

    R = range(groups)
    lanes = [slice(gi * GN, (gi + 1) * GN) for gi in R]
    load = lambda ref: [ref[0, :, lanes[gi]] for gi in R]
    at, rt, bt, kt, bh, kh, v = (load(ref) for ref in
                                 (at_ref, rt_ref, bt_ref, kt_ref, bh_ref, kh_ref, v_ref))
    s0 = [st_ref[gi] for gi in R]
    gam = [gam_ref[0, :, lanes[gi]] for gi in R]

    at_e = [expand(x) for x in at]
    rt_e = [expand(x) for x in rt]
    bt_t = [tile_rows(x) for x in bt]
    kt_t = [tile_rows(x) for x in kt]
    v_t = [tile_rows(x) for x in v]
    a_ab = [jnp.where(strict_mask, _dot_nt(at_e[gi], bt_t[gi]), 0.0) for gi in R]
    a_ak = [jnp.where(strict_mask, _dot_nt(at_e[gi], kt_t[gi]), 0.0) for gi in R]
    a_rb = [jnp.where(incl_mask, _dot_nt(rt_e[gi], bt_t[gi]), 0.0) for gi in R]
    a_rk = [jnp.where(incl_mask, _dot_nt(rt_e[gi], kt_t[gi]), 0.0) for gi in R]

    def compact_dot(x, w_bd):
        return jnp.dot(x.astype(BF16), w_bd, preferred_element_type=F32)

    p_c = [fold(a_ab[gi]) for gi in R]
    t_c = [eye_c + p_c[gi] for gi in R]
    p_bd = [a_ab[gi].astype(BF16) for gi in R]
    p_c = [compact_dot(p_c[gi], p_bd[gi]) for gi in R]
    p_bd = [expand(p_c[gi].astype(BF16)) for gi in R]
    n_levels = L.bit_length() - 2
    for level in range(1, n_levels):
        tp = [compact_dot(jnp.concatenate([t_c[gi], p_c[gi]], axis=0), p_bd[gi]) for gi in R]
        t_c = [t_c[gi] + tp[gi][:L] for gi in R]
        p_c = [tp[gi][L:] for gi in R]
        p_bd = [expand(p_c[gi].astype(BF16)) for gi in R]
    t_c = [t_c[gi] + compact_dot(t_c[gi], p_bd[gi]) for gi in R]
    t_inv = [expand(t_c[gi].astype(BF16)) for gi in R]

    s0b = [x.astype(BF16) for x in s0]
    rhs = [tile_rows(_dot_nt(at[gi], s0b[gi])) + _dot(a_ak[gi], v_t[gi]) for gi in R]
    u = [fold(_dot(t_inv[gi], rhs[gi])) for gi in R]
    y = [fold(_dot(a_rb[gi], tile_rows(u[gi])) + _dot(a_rk[gi], v_t[gi]))
         + _dot_nt(rt[gi], s0b[gi]) for gi in R]
    uv = [jnp.concatenate([u[gi], v[gi].astype(F32)], axis=0) for gi in R]
    bk = [jnp.concatenate([bh[gi], kh[gi]], axis=0) for gi in R]
    upd = [_dot(uv[gi].T, bk[gi]) for gi in R]
    for gi in R:
        y_ref[0, :, lanes[gi]] = y[gi]
        st_ref[gi] = s0[gi] * gam[gi] + jnp.where(seg_mask, upd[gi], 0.0)


def _rwkv_scan(ops, gamma, batch, seq_len, chunk=SCAN_CHUNK, heads=SCAN_HEADS,
               groups=SCAN_GROUPS):
    c = gamma.shape[-1]
    gn = heads * RWKV_HEAD
    lanes = groups * gn
    n_chunks = seq_len // chunk
    seq = pl.BlockSpec((1, chunk, lanes), lambda b, hg, t: (b, t, hg))
    gam = pl.BlockSpec((1, 1, lanes), lambda b, hg, t: (b * n_chunks + t, 0, hg))
    return pl.pallas_call(
        functools.partial(_scan_kernel, chunk=chunk, heads=heads, groups=groups),
        grid=(batch, c // lanes, n_chunks),
        in_specs=[seq] * 7 + [gam],
        out_specs=seq,
        out_shape=jax.ShapeDtypeStruct((batch, seq_len, c), F32),
        scratch_shapes=[pltpu.VMEM((groups, gn, gn), F32)],
        compiler_params=_params("parallel", "parallel", "arbitrary"),
        name="rwkv_scan",
    )(*[o.reshape(batch, seq_len, c) for o in ops], gamma.reshape(batch * n_chunks, 1, c))


def _gn_gate_kernel(y_ref, bonus_ref, g_ref, lnw_ref, lnb_ref, ones_ref, z_ref):
    y = y_ref[...]
    ones = ones_ref[...]
    inv_n = 1.0 / RWKV_HEAD
    mu = _seg_sum(y, ones) * inv_n
    d = y - mu
    var = _seg_sum(d * d, ones) * inv_n
    yn = d * lax.rsqrt(var + GN_EPS) * lnw_ref[...] + lnb_ref[...]
    z_ref[...] = ((yn + bonus_ref[...]) * g_ref[...]).astype(z_ref.dtype)


def _gn_gate(y, bonus, g, lnx_w, lnx_b, rows=EW_ROWS, lanes=EW_LANES):
    m, c = y.shape
    tile = pl.BlockSpec((rows, lanes), lambda i, j: (i, j))
    vec = pl.BlockSpec((1, lanes), lambda i, j: (0, j))
    return pl.pallas_call(
        _gn_gate_kernel,
        grid=(m // rows, c // lanes),
        in_specs=[tile, tile, tile, vec, vec,
                  pl.BlockSpec((SEG_LANES, SEG_LANES), lambda i, j: (0, 0))],
        out_specs=tile,
        out_shape=jax.ShapeDtypeStruct((m, c), BF16),
        compiler_params=_params("parallel", "parallel"),
        name="rwkv_gn_gate",
    )(y, bonus, g, lnx_w, lnx_b, _seg_ones())


def _proj_ln_kernel(z_ref, w_ref, h_ref, lnw_ref, lnb_ref, o_ref):
    y = jnp.dot(z_ref[...], w_ref[...], preferred_element_type=F32)
    o_ref[...] = _layernorm(ALPHA * h_ref[...] + y, lnw_ref[...], lnb_ref[...])


def _proj_ln(z, w, h, ln_w, ln_b, tm=TM):
    m, kdim = z.shape
    c = w.shape[1]
    vec = pl.BlockSpec((1, c), lambda i: (0, 0))
    return pl.pallas_call(
        _proj_ln_kernel,
        grid=(m // tm,),
        in_specs=[pl.BlockSpec((tm, kdim), lambda i: (i, 0)),
                  pl.BlockSpec((kdim, c), lambda i: (0, 0)),
                  pl.BlockSpec((tm, c), lambda i: (i, 0)), vec, vec],
        out_specs=pl.BlockSpec((tm, c), lambda i: (i, 0)),
        out_shape=jax.ShapeDtypeStruct((m, c), F32),
        compiler_params=_params("parallel"),
        name="proj_residual_ln",
    )(z, w, h, ln_w, ln_b)


def _ffn_kernel(h_ref, wg_ref, wu_ref, wd_ref, lnw_ref, lnb_ref, o_ref, xb_ref, acc_ref):
    f = pl.program_id(1)

    @pl.when(f == 0)
    def _():
        xb_ref[...] = h_ref[...].astype(BF16)
        acc_ref[...] = jnp.zeros_like(acc_ref)

    x = xb_ref[...]
    gate = jnp.dot(x, wg_ref[...], preferred_element_type=F32)
    up = jnp.dot(x, wu_ref[...], preferred_element_type=F32)
    act = (gate * jax.nn.sigmoid(gate)) * up
    acc_ref[...] += jnp.dot(act.astype(BF16), wd_ref[...], preferred_element_type=F32)

    @pl.when(f == pl.num_programs(1) - 1)
    def _():
        o_ref[...] = _layernorm(ALPHA * h_ref[...] + acc_ref[...], lnw_ref[...], lnb_ref[...])


def _ffn_ln(h, w_gate, w_up, w_down, ln_w, ln_b, tm=TM, tf=512):
    m, c = h.shape
    d_ff = w_gate.shape[1]
    vec = pl.BlockSpec((1, c), lambda i, f: (0, 0))
    return pl.pallas_call(
        _ffn_kernel,
        grid=(m // tm, d_ff // tf),
        in_specs=[pl.BlockSpec((tm, c), lambda i, f: (i, 0)),
                  pl.BlockSpec((c, tf), lambda i, f: (0, f)),
                  pl.BlockSpec((c, tf), lambda i, f: (0, f)),
                  pl.BlockSpec((tf, c), lambda i, f: (f, 0)), vec, vec],
        out_specs=pl.BlockSpec((tm, c), lambda i, f: (i, 0)),
        out_shape=jax.ShapeDtypeStruct((m, c), F32),
        scratch_shapes=[pltpu.VMEM((tm, c), BF16), pltpu.VMEM((tm, c), F32)],
        compiler_params=_params("parallel", "arbitrary"),
        name="swiglu_ffn_ln",
    )(h, w_gate, w_up, w_down, ln_w, ln_b)


def _kv_kernel(x_ref, wk_ref, wvt_ref, k_ref, mean_ref, vt_ref, xb_ref):
    @pl.when(pl.program_id(1) == 0)
    def _():
        xb_ref[...] = x_ref[...].astype(BF16)

    xb = xb_ref[...]
    y = jnp.dot(xb, wk_ref[...], preferred_element_type=F32)
    k_ref[...] = y.astype(k_ref.dtype)
    tm, tn = y.shape
    nblk = tm // MOBA_BLOCK
    mean_ref[0] = jnp.sum(y.reshape(nblk, MOBA_BLOCK, tn), axis=1) * (1.0 / MOBA_BLOCK)
    vt = lax.dot_general(wvt_ref[...], xb, (((1,), (1,)), ((), ())), preferred_element_type=F32)
    for blk in range(nblk):
        vt_ref[blk] = vt[:, blk * MOBA_BLOCK:(blk + 1) * MOBA_BLOCK].astype(vt_ref.dtype)


def _kv_proj(h, w_k, w_v_t, tm=TM, tn=1024):
    m, c = h.shape
    n = w_k.shape[1]
    nblk = tm // MOBA_BLOCK
    return pl.pallas_call(
        _kv_kernel,
        grid=(m // tm, n // tn),
        in_specs=[pl.BlockSpec((tm, c), lambda i, j: (i, 0)),
                  pl.BlockSpec((c, tn), lambda i, j: (0, j)),
                  pl.BlockSpec((tn, c), lambda i, j: (j, 0))],
        out_specs=[pl.BlockSpec((tm, tn), lambda i, j: (i, j)),
                   pl.BlockSpec((1, nblk, tn), lambda i, j: (i, 0, j)),
                   pl.BlockSpec((nblk, tn, MOBA_BLOCK), lambda i, j: (i, j, 0))],
        out_shape=[jax.ShapeDtypeStruct((m, n), BF16),
                   jax.ShapeDtypeStruct((m // tm, nblk, n), F32),
                   jax.ShapeDtypeStruct((m // MOBA_BLOCK, n, MOBA_BLOCK), BF16)],
        scratch_shapes=[pltpu.VMEM((tm, c), BF16)],
        compiler_params=_params("parallel", "arbitrary"),
        name="shared_kv_proj",
    )(h, w_k, w_v_t)


def _q_kernel(x_ref, w_ref, o_ref, xb_ref):
    @pl.when(pl.program_id(1) == 0)
    def _():
        xb_ref[...] = x_ref[...].astype(BF16)

    o_ref[...] = jnp.dot(xb_ref[...], w_ref[...], preferred_element_type=F32)


def _q_proj(h, w_q, tm=TM, tn=1024):
    m, c = h.shape
    n = w_q.shape[1]
    return pl.pallas_call(
        _q_kernel,
        grid=(m // tm, n // tn),
        in_specs=[pl.BlockSpec((tm, c), lambda i, j: (i, 0)),
                  pl.BlockSpec((c, tn), lambda i, j: (0, j))],
        out_specs=pl.BlockSpec((tm, tn), lambda i, j: (i, j)),
        out_shape=jax.ShapeDtypeStruct((m, n), F32),
        scratch_shapes=[pltpu.VMEM((tm, c), BF16)],
        compiler_params=_params("parallel", "arbitrary"),
        name="moba_q_proj",
    )(h, w_q)


def _moba_kernel(q_ref, k_ref, vt_ref, km_ref, o_ref, sel_ref, *, n_blocks, top_k, heads):
    blk = pl.program_id(2)
    BLK, DH = MOBA_BLOCK, MOBA_HEAD_DIM
    start = pl.multiple_of(blk * BLK, BLK)
    ki = lax.broadcasted_iota(jnp.int32, (BLK, BLK), 0)
    qi = lax.broadcasted_iota(jnp.int32, (BLK, BLK), 1)
    bi = lax.broadcasted_iota(jnp.int32, (n_blocks, BLK), 0)
    cols = [slice(hh * DH, (hh + 1) * DH) for hh in range(heads)]

    R = range(heads)
    q = [q_ref[:, cols[h]] for h in R]
    qb = [(q[h] * DH ** -0.5).astype(BF16) for h in R]

    gate = [lax.dot_general(km_ref[0, :, cols[h]], q[h], (((1,), (1,)), ((), ())),
                            preferred_element_type=F32, precision=HIGHEST) for h in R]
    rank = [jnp.zeros((n_blocks, BLK), jnp.int32) for h in R]
    for mm in range(n_blocks):
        past = (mm < blk).astype(jnp.int32)
        for h in R:
            gm = gate[h][mm:mm + 1, :]
            ahead = (gm > gate[h]) | ((gm == gate[h]) & (mm < bi))
            rank[h] = rank[h] + jnp.where(ahead, 1, 0) * past
    for h in R:
        sel_ref[h] = jnp.where((bi < blk) & (rank[h] < top_k), 1.0, 0.0)

    s = [_dot_nt(k_ref[pl.ds(start, BLK), cols[h]], qb[h]) for h in R]
    s = [jnp.where(ki <= qi, s[h], NEG) for h in R]
    m0 = [jnp.max(s[h], axis=0, keepdims=True) for h in R]
    p = [jnp.exp(s[h] - m0[h]) for h in R]
    l0 = [jnp.sum(p[h], axis=0, keepdims=True) for h in R]
    acc0 = [jnp.dot(vt_ref[blk, cols[h], :], p[h].astype(BF16), preferred_element_type=F32)
            for h in R]

    def body(n, carry):
        m_i, l_i, acc = carry
        off = pl.multiple_of(n * BLK, BLK)
        sn = [_dot_nt(k_ref[pl.ds(off, BLK), cols[h]], qb[h]) for h in R]
        sn = [jnp.where(sel_ref[h, pl.ds(n, 1), :] > 0.0, sn[h], NEG) for h in R]
        m_new = [jnp.maximum(m_i[h], jnp.max(sn[h], axis=0, keepdims=True)) for h in R]
        corr = [jnp.exp(m_i[h] - m_new[h]) for h in R]
        pn = [jnp.exp(sn[h] - m_new[h]) for h in R]
        l_new = [corr[h] * l_i[h] + jnp.sum(pn[h], axis=0, keepdims=True) for h in R]
        pv = [jnp.dot(vt_ref[n, cols[h], :], pn[h].astype(BF16), preferred_element_type=F32)
              for h in R]
        acc_new = [corr[h] * acc[h] + pv[h] for h in R]
        return m_new, l_new, acc_new

    _, l_f, acc_f = lax.fori_loop(0, blk, body, (m0, l0, acc0))
    for h in R:
        o_ref[:, cols[h]] = (acc_f[h] / l_f[h]).T.astype(o_ref.dtype)


def _moba_attention(q, k, v_t, k_mean, batch, seq_len, heads=MOBA_HEADS_PER_STEP):
    m, hd = q.shape
    n_blocks = seq_len // MOBA_BLOCK
    top_k = max(1, min(MOBA_TOPK, n_blocks - 1))
    BLK, W = MOBA_BLOCK, heads * MOBA_HEAD_DIM
    return pl.pallas_call(
        functools.partial(_moba_kernel, n_blocks=n_blocks, top_k=top_k, heads=heads),
        grid=(batch, hd // W, n_blocks),
        in_specs=[pl.BlockSpec((BLK, W), lambda b, h, t: (b * n_blocks + t, h)),
                  pl.BlockSpec((seq_len, W), lambda b, h, t: (b, h)),
                  pl.BlockSpec((n_blocks, W, BLK), lambda b, h, t: (b, h, 0)),
                  pl.BlockSpec((1, n_blocks, W), lambda b, h, t: (b, 0, h))],
        out_specs=pl.BlockSpec((BLK, W), lambda b, h, t: (b * n_blocks + t, h)),
        out_shape=jax.ShapeDtypeStruct((m, hd), BF16),
        scratch_shapes=[pltpu.VMEM((heads, n_blocks, BLK), F32)],
        compiler_params=_params("parallel", "parallel", "arbitrary"),
        name="moba_attention",
    )(q, k, v_t, k_mean)


def _pad_lora(w_in, w_out):
    rank = w_in.shape[1]
    pad = -rank % LORA_PAD
    return (jnp.pad(w_in, ((0, 0), (0, pad))).astype(BF16),
            jnp.pad(w_out, ((0, pad), (0, 0))).astype(BF16))


def kernel(x, a_mix, a_w_r, a_w_k, a_w_v, a_w_o, a_w0, a_w1, a_w2, a_a0, a_a1, a_a2, a_g1, a_g2,
           a_k_k, a_k_a, a_r_k, a_lnx_w, a_lnx_b, kv_w_k, kv_w_v, b_w_q, b_w_o, ffn_w_gate,
           ffn_w_up, ffn_w_down, ln1_w, ln1_b, ln2_w, ln2_b):
    batch, seq_len, c = x.shape
    n_a = a_mix.shape[0]
    n_layers = ffn_w_gate.shape[0]
    assert seq_len % MOBA_BLOCK == 0 and seq_len % TM == 0 and c == D_MODEL
    h = x.reshape(batch * seq_len, c)
    row = lambda vec: vec.reshape(1, c)
    k_all = v_t = k_mean = None

    for layer in range(n_layers):
        if layer < n_a:
            i = layer
            w1, w2 = _pad_lora(a_w1[i], a_w2[i])
            a1, a2 = _pad_lora(a_a1[i], a_a2[i])
            g1, g2 = _pad_lora(a_g1[i], a_g2[i])
            *ops, bonus, g, gamma = _rwkv_front(
                h, a_mix[i], a_w_r[i].astype(BF16), a_w_k[i].astype(BF16), a_w_v[i].astype(BF16),
                w1, w2, row(a_w0[i]), a1, a2, row(a_a0[i]), g1, g2, row(a_k_k[i]),
                row(a_k_a[i]), row(a_r_k[i]), seq_len, SCAN_CHUNK)
            y = _rwkv_scan(ops, gamma, batch, seq_len).reshape(batch * seq_len, c)
            z = _gn_gate(y, bonus, g, row(a_lnx_w[i]), row(a_lnx_b[i]))
            w_o = a_w_o[i]
        else:
            j = layer - n_a
            q = _q_proj(h, b_w_q[j].astype(BF16))
            z = _moba_attention(q, k_all, v_t, k_mean, batch, seq_len)
            w_o = b_w_o[j]
        h = _proj_ln(z, w_o.astype(BF16), h, row(ln1_w[layer]), row(ln1_b[layer]))
        h = _ffn_ln(h, ffn_w_gate[layer].astype(BF16), ffn_w_up[layer].astype(BF16),
                    ffn_w_down[layer].astype(BF16), row(ln2_w[layer]), row(ln2_b[layer]))
        if layer == n_a - 1:
            k_all, k_mean, v_t = _kv_proj(h, kv_w_k.astype(BF16), kv_w_v.T.astype(BF16))
            k_mean = k_mean.reshape(batch, seq_len // MOBA_BLOCK, -1)
    return h.reshape(batch, seq_len, c)
```

```python
import functools

import jax
import jax.numpy as jnp
from jax import lax
from jax.experimental import pallas as pl
from jax.experimental.pallas import tpu as pltpu

D_MODEL = 2048
RWKV_HEAD = 64
MOBA_HEAD_DIM = 128
MOBA_HEADS = D_MODEL // MOBA_HEAD_DIM
MOBA_BLOCK = 256
MOBA_TOPK = 3
NEG = -1e30
GN_EPS = 64e-5
LN_EPS = 1e-5
DEPTH = 2
ALPHA = (2 * DEPTH) ** 0.25
LORA_PAD = 128
NEG_EXP_MINUS_HALF = -0.6065306597126334

VMEM_LIMIT_BYTES = 56 * 1024 * 1024

TM = 512
SCAN_CHUNK = 64
SCAN_HEADS = 4
SCAN_GROUPS = 4
MOBA_HEADS_PER_STEP = 8
EW_ROWS, EW_LANES = 512, 512
SEG_LANES = 256

F32 = jnp.float32
BF16 = jnp.bfloat16
HIGHEST = lax.Precision.HIGHEST


def _params(*semantics):
    return pltpu.CompilerParams(dimension_semantics=semantics, vmem_limit_bytes=VMEM_LIMIT_BYTES)


def _dot(a, b):
    return jnp.dot(a.astype(BF16), b.astype(BF16), preferred_element_type=F32)


def _dot_nt(a, b):
    return lax.dot_general(a.astype(BF16), b.astype(BF16), (((1,), (1,)), ((), ())),
                           preferred_element_type=F32)


def _seg_sum(x, seg_ones, split=True):
    width = seg_ones.shape[0]
    parts = []
    for c0 in range(0, x.shape[1], width):
        xs = x[:, c0:c0 + width]
        hi = xs.astype(BF16)
        acc = jnp.dot(hi, seg_ones, preferred_element_type=F32)
        if split:
            lo = (xs - hi.astype(F32)).astype(BF16)
            acc = acc + jnp.dot(lo, seg_ones, preferred_element_type=F32)
        parts.append(acc)
    return parts[0] if len(parts) == 1 else jnp.concatenate(parts, axis=1)


def _seg_ones():
    head = jnp.arange(SEG_LANES) // RWKV_HEAD
    return (head[:, None] == head[None, :]).astype(BF16)


def _layernorm(t, w, b):
    mu = jnp.mean(t, axis=-1, keepdims=True)
    d = t - mu
    var = jnp.mean(d * d, axis=-1, keepdims=True)
    return d * lax.rsqrt(var + LN_EPS) * w + b


def _token_shift_delta(x_ref, prev_ref, seq_start):
    x = x_ref[...]
    rolled = pltpu.roll(x, 1, 0)
    prev_last = jnp.where(seq_start, 0.0, prev_ref[7:8, :])
    row = lax.broadcasted_iota(jnp.int32, x.shape, 0)
    x_prev = jnp.where(row == 0, prev_last, rolled)
    return x, x_prev - x


def _front_kernel(x_ref, prev_ref, mix_ref, wr_ref, wk_ref, wv_ref, w1_ref, w2_ref, w0_ref,
                  a1_ref, a2_ref, a0_ref, g1_ref, g2_ref, kk_ref, ka_ref, rk_ref, ones_ref,
                  at_ref, rt_ref, bt_ref, kt_ref, bh_ref, kh_ref, vb_ref, bonus_ref, g_ref,
                  gam_ref, xr_ref, xk_ref, xv_ref, hw_ref, ha_ref, hg_ref,
                  *, tiles_per_seq, chunk):
    @pl.when(pl.program_id(1) == 0)
    def _():
        seq_start = (pl.program_id(0) % tiles_per_seq) == 0
        x, xx = _token_shift_delta(x_ref, prev_ref, seq_start)
        mixed = lambda idx: x + xx * mix_ref[idx:idx + 1, :]
        xr_ref[...] = mixed(0).astype(BF16)
        xk_ref[...] = mixed(2).astype(BF16)
        xv_ref[...] = mixed(3).astype(BF16)
        hw_ref[...] = jnp.tanh(_dot(mixed(1), w1_ref[...])).astype(BF16)
        ha_ref[...] = _dot(mixed(4), a1_ref[...]).astype(BF16)
        hg_ref[...] = jax.nn.sigmoid(_dot(mixed(5), g1_ref[...])).astype(BF16)

    dot = functools.partial(jnp.dot, preferred_element_type=F32)
    rows, lanes = at_ref.shape
    n_chunks = rows // chunk
    blocks = [slice(c0, c0 + SEG_LANES) for c0 in range(0, lanes, SEG_LANES)]
    xr, xk, xv, hw, ha, hg = (ref[...] for ref in (xr_ref, xk_ref, xv_ref, hw_ref, ha_ref, hg_ref))
    r = [dot(xr, wr_ref[:, cb]) for cb in blocks]
    k = [dot(xk, wk_ref[:, cb]) for cb in blocks]
    v = [dot(xv, wv_ref[:, cb]) for cb in blocks]
    w_pre = [w0_ref[:, cb] + dot(hw, w2_ref[:, cb]) for cb in blocks]
    a_pre = [a0_ref[:, cb] + dot(ha, a2_ref[:, cb]) for cb in blocks]
    for cb in blocks:
        g_ref[:, cb] = dot(hg, g2_ref[:, cb])

    ones = ones_ref[...]
    pos = lax.broadcasted_iota(jnp.int32, (rows, SEG_LANES), 0) % chunk
    for idx, cb in enumerate(blocks):
        ld = jax.nn.sigmoid(w_pre[idx]) * NEG_EXP_MINUS_HALF
        a = jax.nn.sigmoid(a_pre[idx])
        kk = k[idx] * kk_ref[:, cb]
        kk = kk * lax.rsqrt(jnp.maximum(_seg_sum(kk * kk, ones), 1e-24))
        k_mod = k[idx] * (1.0 + (a - 1.0) * ka_ref[:, cb])
        b_vec = kk * a

        cs = ld
        step = 1
        while step < chunk:
            cs = cs + jnp.where(pos >= step, pltpu.roll(cs, step, 0), 0.0)
            step *= 2
        cs3 = cs.reshape(n_chunks, chunk, SEG_LANES)
        cs_last = cs3[:, chunk - 1:chunk, :]
        e_tail = jnp.exp(cs_last - cs3).reshape(rows, SEG_LANES)
        e_neg = jnp.exp(-cs)

        at_ref[:, cb] = (-kk * jnp.exp(cs - ld)).astype(BF16)
        rt_ref[:, cb] = (r[idx] * jnp.exp(cs)).astype(BF16)
        bt_ref[:, cb] = (b_vec * e_neg).astype(BF16)
        kt_ref[:, cb] = (k_mod * e_neg).astype(BF16)
        bh_ref[:, cb] = (b_vec * e_tail).astype(BF16)
        kh_ref[:, cb] = (k_mod * e_tail).astype(BF16)
        vb_ref[:, cb] = v[idx].astype(BF16)
        bonus_ref[:, cb] = _seg_sum(r[idx] * k_mod * rk_ref[:, cb], ones, split=False) * v[idx]
        gam_ref[:, cb] = jnp.exp(cs_last).reshape(n_chunks, SEG_LANES)


def _rwkv_front(h, mix, w_r, w_k, w_v, w1, w2, w0, a1, a2, a0, g1, g2, k_k, k_a, r_k,
                seq_len, chunk, rows=EW_ROWS, lanes=EW_LANES):
    m, c = h.shape
    full = lambda arr: pl.BlockSpec(arr.shape, lambda i, j: (0,) * arr.ndim)
    cols = lambda arr: pl.BlockSpec((arr.shape[0], lanes), lambda i, j: (0, j))
    tile = pl.BlockSpec((rows, lanes), lambda i, j: (i, j))
    ones = _seg_ones()
    bf = jax.ShapeDtypeStruct((m, c), BF16)
    f32 = jax.ShapeDtypeStruct((m, c), F32)
    return pl.pallas_call(
        functools.partial(_front_kernel, tiles_per_seq=seq_len // rows, chunk=chunk),
        grid=(m // rows, c // lanes),
        in_specs=[pl.BlockSpec((rows, c), lambda i, j: (i, 0)),
                  pl.BlockSpec((8, c), lambda i, j: (jnp.maximum(i * (rows // 8) - 1, 0), 0)),
                  full(mix), cols(w_r), cols(w_k), cols(w_v),
                  full(w1), cols(w2), cols(w0), full(a1), cols(a2), cols(a0),
                  full(g1), cols(g2), cols(k_k), cols(k_a), cols(r_k), full(ones)],
        out_specs=[tile] * 9 + [pl.BlockSpec((rows // chunk, lanes), lambda i, j: (i, j))],
        out_shape=[bf] * 7 + [f32, f32, jax.ShapeDtypeStruct((m // chunk, c), F32)],
        scratch_shapes=[pltpu.VMEM((rows, c), BF16)] * 3
        + [pltpu.VMEM((rows, w1.shape[1]), BF16), pltpu.VMEM((rows, a1.shape[1]), BF16),
           pltpu.VMEM((rows, g1.shape[1]), BF16)],
        compiler_params=_params("parallel", "arbitrary"),
        name="rwkv_front",
    )(h, h, mix, w_r, w_k, w_v, w1, w2, w0, a1, a2, a0, g1, g2, k_k, k_a, r_k, ones)


def _scan_kernel(at_ref, rt_ref, bt_ref, kt_ref, bh_ref, kh_ref, v_ref, gam_ref, y_ref, st_ref,
                 *, chunk, heads, groups):
    L, G, N = chunk, heads, RWKV_HEAD
    GN, GL = G * N, G * L
    assert L == N

    @pl.when(pl.program_id(2) == 0)
    def _():
        st_ref[...] = jnp.zeros_like(st_ref)

    lane_head = lax.broadcasted_iota(jnp.int32, (L, GN), 1) // N
    seg_mask = (lax.broadcasted_iota(jnp.int32, (GN, GN), 0) // N
                == lax.broadcasted_iota(jnp.int32, (GN, GN), 1) // N)
    expand_mask = (lax.broadcasted_iota(jnp.int32, (GL, GN), 0) // L
                   == lax.broadcasted_iota(jnp.int32, (GL, GN), 1) // N)
    prow = lax.broadcasted_iota(jnp.int32, (GL, GL), 0)
    pcol = lax.broadcasted_iota(jnp.int32, (GL, GL), 1)
    same_head = (prow // L) == (pcol // L)
    strict_mask = same_head & (pcol < prow)
    incl_mask = same_head & (pcol <= prow)

    def tile_rows(x):
        return jnp.concatenate([x] * G, axis=0)

    def expand(x):
        return jnp.where(expand_mask, tile_rows(x), jnp.zeros((), x.dtype))

    def fold(x_exp):
        out = jnp.zeros((L, GN), F32)
        for hh in range(G):
            out = out + jnp.where(lane_head == hh, x_exp[hh * L:(hh + 1) * L, :], 0.0)
        return out

    eye_c = (lax.broadcasted_iota(jnp.int32, (L, GN), 1) % L
             == lax.broadcasted_iota(jnp.int32, (L, GN), 0)).astype(F32)

    R = range(groups)
    lanes = [slice(gi * GN, (gi + 1) * GN) for gi in R]
    load = lambda ref: [ref[0, :, lanes[gi]] for gi in R]
    at, rt, bt, kt, bh, kh, v = (load(ref) for ref in
                                 (at_ref, rt_ref, bt_ref, kt_ref, bh_ref, kh_ref, v_ref))
    s0 = [st_ref[gi] for gi in R]
    gam = [gam_ref[0, :, lanes[gi]] for gi in R]

    at_e = [expand(x) for x in at]
    rt_e = [expand(x) for x in rt]
    bt_t = [tile_rows(x) for x in bt]
    kt_t = [tile_rows(x) for x in kt]
    v_t = [tile_rows(x) for x in v]
    a_ab = [jnp.where(strict_mask, _dot_nt(at_e[gi], bt_t[gi]), 0.0) for gi in R]
    a_ak = [jnp.where(strict_mask, _dot_nt(at_e[gi], kt_t[gi]), 0.0) for gi in R]
    a_rb = [jnp.where(incl_mask, _dot_nt(rt_e[gi], bt_t[gi]), 0.0) for gi in R]
    a_rk = [jnp.where(incl_mask, _dot_nt(rt_e[gi], kt_t[gi]), 0.0) for gi in R]

    def compact_dot(x, w_bd):
        return jnp.dot(x.astype(BF16), w_bd, preferred_element_type=F32)

    p_c = [fold(a_ab[gi]) for gi in R]
    t_c = [eye_c + p_c[gi] for gi in R]
    p_bd = [a_ab[gi].astype(BF16) for gi in R]
    p_c = [compact_dot(p_c[gi], p_bd[gi]) for gi in R]
    p_bd = [expand(p_c[gi].astype(BF16)) for gi in R]
    n_levels = L.bit_length() - 2
    for level in range(1, n_levels):
        tp = [compact_dot(jnp.concatenate([t_c[gi], p_c[gi]], axis=0), p_bd[gi]) for gi in R]
        t_c = [t_c[gi] + tp[gi][:L] for gi in R]
        p_c = [tp[gi][L:] for gi in R]
        p_bd = [expand(p_c[gi].astype(BF16)) for gi in R]
    t_c = [t_c[gi] + compact_dot(t_c[gi], p_bd[gi]) for gi in R]
    t_inv = [expand(t_c[gi].astype(BF16)) for gi in R]

    s0b = [x.astype(BF16) for x in s0]
    rhs = [tile_rows(_dot_nt(at[gi], s0b[gi])) + _dot(a_ak[gi], v_t[gi]) for gi in R]
    u = [fold(_dot(t_inv[gi], rhs[gi])) for gi in R]
    y = [fold(_dot(a_rb[gi], tile_rows(u[gi])) + _dot(a_rk[gi], v_t[gi]))
         + _dot_nt(rt[gi], s0b[gi]) for gi in R]
    uv = [jnp.concatenate([u[gi], v[gi].astype(F32)], axis=0) for gi in R]
    bk = [jnp.concatenate([bh[gi], kh[gi]], axis=0) for gi in R]
    upd = [_dot(uv[gi].T, bk[gi]) for gi in R]
    for gi in R:
        y_ref[0, :, lanes[gi]] = y[gi]
        st_ref[gi] = s0[gi] * gam[gi] + jnp.where(seg_mask, upd[gi], 0.0)


def _rwkv_scan(ops, gamma, batch, seq_len, chunk=SCAN_CHUNK, heads=SCAN_HEADS,
               groups=SCAN_GROUPS):
    c = gamma.shape[-1]
    gn = heads * RWKV_HEAD
    lanes = groups * gn
    n_chunks = seq_len // chunk
    seq = pl.BlockSpec((1, chunk, lanes), lambda b, hg, t: (b, t, hg))
    gam = pl.BlockSpec((1, 1, lanes), lambda b, hg, t: (b * n_chunks + t, 0, hg))
    return pl.pallas_call(
        functools.partial(_scan_kernel, chunk=chunk, heads=heads, groups=groups),
        grid=(batch, c // lanes, n_chunks),
        in_specs=[seq] * 7 + [gam],
        out_specs=seq,
        out_shape=jax.ShapeDtypeStruct((batch, seq_len, c), F32),
        scratch_shapes=[pltpu.VMEM((groups, gn, gn), F32)],
        compiler_params=_params("parallel", "parallel", "arbitrary"),
        name="rwkv_scan",
    )(*[o.reshape(batch, seq_len, c) for o in ops], gamma.reshape(batch * n_chunks, 1, c))


def _gn_gate_kernel(y_ref, bonus_ref, g_ref, lnw_ref, lnb_ref, ones_ref, z_ref):
    y = y_ref[...]
    ones = ones_ref[...]
    inv_n = 1.0 / RWKV_HEAD
    mu = _seg_sum(y, ones) * inv_n
    d = y - mu
    var = _seg_sum(d * d, ones) * inv_n
    yn = d * lax.rsqrt(var + GN_EPS) * lnw_ref[...] + lnb_ref[...]
    z_ref[...] = ((yn + bonus_ref[...]) * g_ref[...]).astype(z_ref.dtype)


def _gn_gate(y, bonus, g, lnx_w, lnx_b, rows=EW_ROWS, lanes=EW_LANES):
    m, c = y.shape
    tile = pl.BlockSpec((rows, lanes), lambda i, j: (i, j))
    vec = pl.BlockSpec((1, lanes), lambda i, j: (0, j))
    return pl.pallas_call(
        _gn_gate_kernel,
        grid=(m // rows, c // lanes),
        in_specs=[tile, tile, tile, vec, vec,
                  pl.BlockSpec((SEG_LANES, SEG_LANES), lambda i, j: (0, 0))],
        out_specs=tile,
        out_shape=jax.ShapeDtypeStruct((m, c), BF16),
        compiler_params=_params("parallel", "parallel"),
        name="rwkv_gn_gate",
    )(y, bonus, g, lnx_w, lnx_b, _seg_ones())


def _proj_ln_kernel(z_ref, w_ref, h_ref, lnw_ref, lnb_ref, o_ref):
    y = jnp.dot(z_ref[...], w_ref[...], preferred_element_type=F32)
    o_ref[...] = _layernorm(ALPHA * h_ref[...] + y, lnw_ref[...], lnb_ref[...])


def _proj_ln(z, w, h, ln_w, ln_b, tm=TM):
    m, kdim = z.shape
    c = w.shape[1]
    vec = pl.BlockSpec((1, c), lambda i: (0, 0))
    return pl.pallas_call(
        _proj_ln_kernel,
        grid=(m // tm,),
        in_specs=[pl.BlockSpec((tm, kdim), lambda i: (i, 0)),
                  pl.BlockSpec((kdim, c), lambda i: (0, 0)),
                  pl.BlockSpec((tm, c), lambda i: (i, 0)), vec, vec],
        out_specs=pl.BlockSpec((tm, c), lambda i: (i, 0)),
        out_shape=jax.ShapeDtypeStruct((m, c), F32),
        compiler_params=_params("parallel"),
        name="proj_residual_ln",
    )(z, w, h, ln_w, ln_b)


def _ffn_kernel(h_ref, wg_ref, wu_ref, wd_ref, lnw_ref, lnb_ref, o_ref, xb_ref, acc_ref):
    f = pl.program_id(1)

    @pl.when(f == 0)
    def _():
        xb_ref[...] = h_ref[...].astype(BF16)
        acc_ref[...] = jnp.zeros_like(acc_ref)

    x = xb_ref[...]
    gate = jnp.dot(x, wg_ref[...], preferred_element_type=F32)
    up = jnp.dot(x, wu_ref[...], preferred_element_type=F32)
    act = (gate * jax.nn.sigmoid(gate)) * up
    acc_ref[...] += jnp.dot(act.astype(BF16), wd_ref[...], preferred_element_type=F32)

    @pl.when(f == pl.num_programs(1) - 1)
    def _():
        o_ref[...] = _layernorm(ALPHA * h_ref[...] + acc_ref[...], lnw_ref[...], lnb_ref[...])


def _ffn_ln(h, w_gate, w_up, w_down, ln_w, ln_b, tm=TM, tf=512):
    m, c = h.shape
    d_ff = w_gate.shape[1]
    vec = pl.BlockSpec((1, c), lambda i, f: (0, 0))
    return pl.pallas_call(
        _ffn_kernel,
        grid=(m // tm, d_ff // tf),
        in_specs=[pl.BlockSpec((tm, c), lambda i, f: (i, 0)),
                  pl.BlockSpec((c, tf), lambda i, f: (0, f)),
                  pl.BlockSpec((c, tf), lambda i, f: (0, f)),
                  pl.BlockSpec((tf, c), lambda i, f: (f, 0)), vec, vec],
        out_specs=pl.BlockSpec((tm, c), lambda i, f: (i, 0)),
        out_shape=jax.ShapeDtypeStruct((m, c), F32),
        scratch_shapes=[pltpu.VMEM((tm, c), BF16), pltpu.VMEM((tm, c), F32)],
        compiler_params=_params("parallel", "arbitrary"),
        name="swiglu_ffn_ln",
    )(h, w_gate, w_up, w_down, ln_w, ln_b)


def _kv_kernel(x_ref, wk_ref, wvt_ref, k_ref, mean_ref, vt_ref, xb_ref):
    @pl.when(pl.program_id(1) == 0)
    def _():
        xb_ref[...] = x_ref[...].astype(BF16)

    xb = xb_ref[...]
    y = jnp.dot(xb, wk_ref[...], preferred_element_type=F32)
    k_ref[...] = y.astype(k_ref.dtype)
    tm, tn = y.shape
    nblk = tm // MOBA_BLOCK
    mean_ref[0] = jnp.sum(y.reshape(nblk, MOBA_BLOCK, tn), axis=1) * (1.0 / MOBA_BLOCK)
    vt = lax.dot_general(wvt_ref[...], xb, (((1,), (1,)), ((), ())), preferred_element_type=F32)
    for blk in range(nblk):
        vt_ref[blk] = vt[:, blk * MOBA_BLOCK:(blk + 1) * MOBA_BLOCK].astype(vt_ref.dtype)


def _kv_proj(h, w_k, w_v_t, tm=2 * TM, tn=512):
    m, c = h.shape
    n = w_k.shape[1]
    nblk = tm // MOBA_BLOCK
    return pl.pallas_call(
        _kv_kernel,
        grid=(m // tm, n // tn),
        in_specs=[pl.BlockSpec((tm, c), lambda i, j: (i, 0)),
                  pl.BlockSpec((c, tn), lambda i, j: (0, j)),
                  pl.BlockSpec((tn, c), lambda i, j: (j, 0))],
        out_specs=[pl.BlockSpec((tm, tn), lambda i, j: (i, j)),
                   pl.BlockSpec((1, nblk, tn), lambda i, j: (i, 0, j)),
                   pl.BlockSpec((nblk, tn, MOBA_BLOCK), lambda i, j: (i, j, 0))],
        out_shape=[jax.ShapeDtypeStruct((m, n), BF16),
                   jax.ShapeDtypeStruct((m // tm, nblk, n), F32),
                   jax.ShapeDtypeStruct((m // MOBA_BLOCK, n, MOBA_BLOCK), BF16)],
        scratch_shapes=[pltpu.VMEM((tm, c), BF16)],
        compiler_params=_params("parallel", "arbitrary"),
        name="shared_kv_proj",
    )(h, w_k, w_v_t)


def _q_kernel(x_ref, w_ref, o_ref, xb_ref):
    @pl.when(pl.program_id(1) == 0)
    def _():
        xb_ref[...] = x_ref[...].astype(BF16)

    o_ref[...] = jnp.dot(xb_ref[...], w_ref[...], preferred_element_type=F32)


def _q_proj(h, w_q, tm=TM, tn=1024):
    m, c = h.shape
    n = w_q.shape[1]
    return pl.pallas_call(
        _q_kernel,
        grid=(m // tm, n // tn),
        in_specs=[pl.BlockSpec((tm, c), lambda i, j: (i, 0)),
                  pl.BlockSpec((c, tn), lambda i, j: (0, j))],
        out_specs=pl.BlockSpec((tm, tn), lambda i, j: (i, j)),
        out_shape=jax.ShapeDtypeStruct((m, n), F32),
        scratch_shapes=[pltpu.VMEM((tm, c), BF16)],
        compiler_params=_params("parallel", "arbitrary"),
        name="moba_q_proj",
    )(h, w_q)


def _moba_kernel(q_ref, k_ref, vt_ref, km_ref, o_ref, sel_ref, *, n_blocks, top_k, heads):
    blk = pl.program_id(2)
    BLK, DH = MOBA_BLOCK, MOBA_HEAD_DIM
    start = pl.multiple_of(blk * BLK, BLK)
    ki = lax.broadcasted_iota(jnp.int32, (BLK, BLK), 0)
    qi = lax.broadcasted_iota(jnp.int32, (BLK, BLK), 1)
    bi = lax.broadcasted_iota(jnp.int32, (n_blocks, BLK), 0)
    cols = [slice(hh * DH, (hh + 1) * DH) for hh in range(heads)]

    R = range(heads)
    q = [q_ref[:, cols[h]] for h in R]
    qb = [(q[h] * DH ** -0.5).astype(BF16) for h in R]

    gate = [lax.dot_general(km_ref[0, :, cols[h]], q[h], (((1,), (1,)), ((), ())),
                            preferred_element_type=F32, precision=HIGHEST) for h in R]
    rank = [jnp.zeros((n_blocks, BLK), jnp.int32) for h in R]
    for mm in range(n_blocks):
        past = (mm < blk).astype(jnp.int32)
        for h in R:
            gm = gate[h][mm:mm + 1, :]
            ahead = (gm > gate[h]) | ((gm == gate[h]) & (mm < bi))
            rank[h] = rank[h] + jnp.where(ahead, 1, 0) * past
    for h in R:
        sel_ref[h] = jnp.where((bi < blk) & (rank[h] < top_k), 1.0, 0.0)

    s = [_dot_nt(k_ref[pl.ds(start, BLK), cols[h]], qb[h]) for h in R]
    s = [jnp.where(ki <= qi, s[h], NEG) for h in R]
    m0 = [jnp.max(s[h], axis=0, keepdims=True) for h in R]
    p = [jnp.exp(s[h] - m0[h]) for h in R]
    l0 = [jnp.sum(p[h], axis=0, keepdims=True) for h in R]
    acc0 = [jnp.dot(vt_ref[blk, cols[h], :], p[h].astype(BF16), preferred_element_type=F32)
            for h in R]

    def body(n, carry):
        m_i, l_i, acc = carry
        off = pl.multiple_of(n * BLK, BLK)
        sn = [_dot_nt(k_ref[pl.ds(off, BLK), cols[h]], qb[h]) for h in R]
        sn = [jnp.where(sel_ref[h, pl.ds(n, 1), :] > 0.0, sn[h], NEG) for h in R]
        m_new = [jnp.maximum(m_i[h], jnp.max(sn[h], axis=0, keepdims=True)) for h in R]
        corr = [jnp.exp(m_i[h] - m_new[h]) for h in R]
        pn = [jnp.exp(sn[h] - m_new[h]) for h in R]
        l_new = [corr[h] * l_i[h] + jnp.sum(pn[h], axis=0, keepdims=True) for h in R]
        pv = [jnp.dot(vt_ref[n, cols[h], :], pn[h].astype(BF16), preferred_element_type=F32)
              for h in R]
        acc_new = [corr[h] * acc[h] + pv[h] for h in R]
        return m_new, l_new, acc_new

    _, l_f, acc_f = lax.fori_loop(0, blk, body, (m0, l0, acc0))
    for h in R:
        o_ref[:, cols[h]] = (acc_f[h] / l_f[h]).T.astype(o_ref.dtype)


def _moba_attention(q, k, v_t, k_mean, batch, seq_len, heads=MOBA_HEADS_PER_STEP):
    m, hd = q.shape
    n_blocks = seq_len // MOBA_BLOCK
    top_k = max(1, min(MOBA_TOPK, n_blocks - 1))
    BLK, W = MOBA_BLOCK, heads * MOBA_HEAD_DIM
    return pl.pallas_call(
        functools.partial(_moba_kernel, n_blocks=n_blocks, top_k=top_k, heads=heads),
        grid=(batch, hd // W, n_blocks),
        in_specs=[pl.BlockSpec((BLK, W), lambda b, h, t: (b * n_blocks + t, h)),
                  pl.BlockSpec((seq_len, W), lambda b, h, t: (b, h)),
                  pl.BlockSpec((n_blocks, W, BLK), lambda b, h, t: (b, h, 0)),
                  pl.BlockSpec((1, n_blocks, W), lambda b, h, t: (b, 0, h))],
        out_specs=pl.BlockSpec((BLK, W), lambda b, h, t: (b * n_blocks + t, h)),
        out_shape=jax.ShapeDtypeStruct((m, hd), BF16),
        scratch_shapes=[pltpu.VMEM((heads, n_blocks, BLK), F32)],
        compiler_params=_params("parallel", "parallel", "arbitrary"),
        name="moba_attention",
    )(q, k, v_t, k_mean)


def _pad_lora(w_in, w_out):
    rank = w_in.shape[1]
    pad = -rank % LORA_PAD
    return (jnp.pad(w_in, ((0, 0), (0, pad))).astype(BF16),
            jnp.pad(w_out, ((0, pad), (0, 0))).astype(BF16))


def kernel(x, a_mix, a_w_r, a_w_k, a_w_v, a_w_o, a_w0, a_w1, a_w2, a_a0, a_a1, a_a2, a_g1, a_g2,
           a_k_k, a_k_a, a_r_k, a_lnx_w, a_lnx_b, kv_w_k, kv_w_v, b_w_q, b_w_o, ffn_w_gate,
           ffn_w_up, ffn_w_down, ln1_w, ln1_b, ln2_w, ln2_b):
    batch, seq_len, c = x.shape
    n_a = a_mix.shape[0]
    n_layers = ffn_w_gate.shape[0]
    assert seq_len % MOBA_BLOCK == 0 and seq_len % TM == 0 and c == D_MODEL
    h = x.reshape(batch * seq_len, c)
    row = lambda vec: vec.reshape(1, c)
    k_all = v_t = k_mean = None

    for layer in range(n_layers):
        if layer < n_a:
            i = layer
            w1, w2 = _pad_lora(a_w1[i], a_w2[i])
            a1, a2 = _pad_lora(a_a1[i], a_a2[i])
            g1, g2 = _pad_lora(a_g1[i], a_g2[i])
            *ops, bonus, g, gamma = _rwkv_front(
                h, a_mix[i], a_w_r[i].astype(BF16), a_w_k[i].astype(BF16), a_w_v[i].astype(BF16),
                w1, w2, row(a_w0[i]), a1, a2, row(a_a0[i]), g1, g2, row(a_k_k[i]),
                row(a_k_a[i]), row(a_r_k[i]), seq_len, SCAN_CHUNK)
            y = _rwkv_scan(ops, gamma, batch, seq_len).reshape(batch * seq_len, c)
            z = _gn_gate(y, bonus, g, row(a_lnx_w[i]), row(a_lnx_b[i]))
            w_o = a_w_o[i]
        else:
            j = layer - n_a
            q = _q_proj(h, b_w_q[j].astype(BF16))
            z = _moba_attention(q, k_all, v_t, k_mean, batch, seq_len)
            w_o = b_w_o[j]
        h = _proj_ln(z, w_o.astype(BF16), h, row(ln1_w[layer]), row(ln1_b[layer]))
        h = _ffn_ln(h, ffn_w_gate[layer].astype(BF16), ffn_w_up[layer].astype(BF16),
                    ffn_w_down[layer].astype(BF16), row(ln2_w[layer]), row(ln2_b[layer]))
        if layer == n_a - 1:
            k_all, k_mean, v_t = _kv_proj(h, kv_w_k.astype(BF16), kv_w_v.T.astype(BF16))
            k_mean = k_mean.reshape(batch, seq_len // MOBA_BLOCK, -1)
    return h.reshape(batch, seq_len, c)
```

```python
import functools

import jax
import jax.numpy as jnp
from jax import lax
from jax.experimental import pallas as pl
from jax.experimental.pallas import tpu as pltpu

D_MODEL = 2048
RWKV_HEAD = 64
MOBA_HEAD_DIM = 128
MOBA_HEADS = D_MODEL // MOBA_HEAD_DIM
MOBA_BLOCK = 256
MOBA_TOPK = 3
NEG = -1e30
GN_EPS = 64e-5
LN_EPS = 1e-5
DEPTH = 2
ALPHA = (2 * DEPTH) ** 0.25
LORA_PAD = 128
NEG_EXP_MINUS_HALF = -0.6065306597126334

VMEM_LIMIT_BYTES = 56 * 1024 * 1024

TM = 512
SCAN_CHUNK = 64
SCAN_HEADS = 4
SCAN_GROUPS = 8
MOBA_HEADS_PER_STEP = 8
EW_ROWS, EW_LANES = 512, 512
SEG_LANES = 256

F32 = jnp.float32
BF16 = jnp.bfloat16
HIGHEST = lax.Precision.HIGHEST


def _params(*semantics):
    return pltpu.CompilerParams(dimension_semantics=semantics, vmem_limit_bytes=VMEM_LIMIT_BYTES)


def _dot(a, b):
    return jnp.dot(a.astype(BF16), b.astype(BF16), preferred_element_type=F32)


def _dot_nt(a, b):
    return lax.dot_general(a.astype(BF16), b.astype(BF16), (((1,), (1,)), ((), ())),
                           preferred_element_type=F32)


def _seg_sum(x, seg_ones, split=True):
    width = seg_ones.shape[0]
    parts = []
    for c0 in range(0, x.shape[1], width):
        xs = x[:, c0:c0 + width]
        hi = xs.astype(BF16)
        acc = jnp.dot(hi, seg_ones, preferred_element_type=F32)
        if split:
            lo = (xs - hi.astype(F32)).astype(BF16)
            acc = acc + jnp.dot(lo, seg_ones, preferred_element_type=F32)
        parts.append(acc)
    return parts[0] if len(parts) == 1 else jnp.concatenate(parts, axis=1)


def _seg_ones():
    head = jnp.arange(SEG_LANES) // RWKV_HEAD
    return (head[:, None] == head[None, :]).astype(BF16)


def _layernorm(t, w, b):
    mu = jnp.mean(t, axis=-1, keepdims=True)
    d = t - mu
    var = jnp.mean(d * d, axis=-1, keepdims=True)
    return d * lax.rsqrt(var + LN_EPS) * w + b


def _token_shift_delta(x_ref, prev_ref, seq_start):
    x = x_ref[...]
    rolled = pltpu.roll(x, 1, 0)
    prev_last = jnp.where(seq_start, 0.0, prev_ref[7:8, :])
    row = lax.broadcasted_iota(jnp.int32, x.shape, 0)
    x_prev = jnp.where(row == 0, prev_last, rolled)
    return x, x_prev - x


def _front_kernel(x_ref, prev_ref, mix_ref, wr_ref, wk_ref, wv_ref, w1_ref, w2_ref, w0_ref,
                  a1_ref, a2_ref, a0_ref, g1_ref, g2_ref, kk_ref, ka_ref, rk_ref, ones_ref,
                  at_ref, rt_ref, bt_ref, kt_ref, bh_ref, kh_ref, vb_ref, bonus_ref, g_ref,
                  gam_ref, xr_ref, xk_ref, xv_ref, hw_ref, ha_ref, hg_ref,
                  *, tiles_per_seq, chunk):
    @pl.when(pl.program_id(1) == 0)
    def _():
        seq_start = (pl.program_id(0) % tiles_per_seq) == 0
        x, xx = _token_shift_delta(x_ref, prev_ref, seq_start)
        mixed = lambda idx: x + xx * mix_ref[idx:idx + 1, :]
        xr_ref[...] = mixed(0).astype(BF16)
        xk_ref[...] = mixed(2).astype(BF16)
        xv_ref[...] = mixed(3).astype(BF16)
        hw_ref[...] = jnp.tanh(_dot(mixed(1), w1_ref[...])).astype(BF16)
        ha_ref[...] = _dot(mixed(4), a1_ref[...]).astype(BF16)
        hg_ref[...] = jax.nn.sigmoid(_dot(mixed(5), g1_ref[...])).astype(BF16)

    dot = functools.partial(jnp.dot, preferred_element_type=F32)
    rows, lanes = at_ref.shape
    n_chunks = rows // chunk
    blocks = [slice(c0, c0 + SEG_LANES) for c0 in range(0, lanes, SEG_LANES)]
    xr, xk, xv, hw, ha, hg = (ref[...] for ref in (xr_ref, xk_ref, xv_ref, hw_ref, ha_ref, hg_ref))
    r = [dot(xr, wr_ref[:, cb]) for cb in blocks]
    k = [dot(xk, wk_ref[:, cb]) for cb in blocks]
    v = [dot(xv, wv_ref[:, cb]) for cb in blocks]
    w_pre = [w0_ref[:, cb] + dot(hw, w2_ref[:, cb]) for cb in blocks]
    a_pre = [a0_ref[:, cb] + dot(ha, a2_ref[:, cb]) for cb in blocks]
    for cb in blocks:
        g_ref[:, cb] = dot(hg, g2_ref[:, cb])

    ones = ones_ref[...]
    pos = lax.broadcasted_iota(jnp.int32, (rows, SEG_LANES), 0) % chunk
    for idx, cb in enumerate(blocks):
        ld = jax.nn.sigmoid(w_pre[idx]) * NEG_EXP_MINUS_HALF
        a = jax.nn.sigmoid(a_pre[idx])
        kk = k[idx] * kk_ref[:, cb]
        kk = kk * lax.rsqrt(jnp.maximum(_seg_sum(kk * kk, ones), 1e-24))
        k_mod = k[idx] * (1.0 + (a - 1.0) * ka_ref[:, cb])
        b_vec = kk * a

        cs = ld
        step = 1
        while step < chunk:
            cs = cs + jnp.where(pos >= step, pltpu.roll(cs, step, 0), 0.0)
            step *= 2
        cs3 = cs.reshape(n_chunks, chunk, SEG_LANES)
        cs_last = cs3[:, chunk - 1:chunk, :]
        e_tail = jnp.exp(cs_last - cs3).reshape(rows, SEG_LANES)
        e_neg = jnp.exp(-cs)

        at_ref[:, cb] = (-kk * jnp.exp(cs - ld)).astype(BF16)
        rt_ref[:, cb] = (r[idx] * jnp.exp(cs)).astype(BF16)
        bt_ref[:, cb] = (b_vec * e_neg).astype(BF16)
        kt_ref[:, cb] = (k_mod * e_neg).astype(BF16)
        bh_ref[:, cb] = (b_vec * e_tail).astype(BF16)
        kh_ref[:, cb] = (k_mod * e_tail).astype(BF16)
        vb_ref[:, cb] = v[idx].astype(BF16)
        bonus_ref[:, cb] = _seg_sum(r[idx] * k_mod * rk_ref[:, cb], ones, split=False) * v[idx]
        gam_ref[:, cb] = jnp.exp(cs_last).reshape(n_chunks, SEG_LANES)


def _rwkv_front(h, mix, w_r, w_k, w_v, w1, w2, w0, a1, a2, a0, g1, g2, k_k, k_a, r_k,
                seq_len, chunk, rows=EW_ROWS, lanes=EW_LANES):
    m, c = h.shape
    full = lambda arr: pl.BlockSpec(arr.shape, lambda i, j: (0,) * arr.ndim)
    cols = lambda arr: pl.BlockSpec((arr.shape[0], lanes), lambda i, j: (0, j))
    tile = pl.BlockSpec((rows, lanes), lambda i, j: (i, j))
    ones = _seg_ones()
    bf = jax.ShapeDtypeStruct((m, c), BF16)
    f32 = jax.ShapeDtypeStruct((m, c), F32)
    return pl.pallas_call(
        functools.partial(_front_kernel, tiles_per_seq=seq_len // rows, chunk=chunk),
        grid=(m // rows, c // lanes),
        in_specs=[pl.BlockSpec((rows, c), lambda i, j: (i, 0)),
                  pl.BlockSpec((8, c), lambda i, j: (jnp.maximum(i * (rows // 8) - 1, 0), 0)),
                  full(mix), cols(w_r), cols(w_k), cols(w_v),
                  full(w1), cols(w2), cols(w0), full(a1), cols(a2), cols(a0),
                  full(g1), cols(g2), cols(k_k), cols(k_a), cols(r_k), full(ones)],
        out_specs=[tile] * 9 + [pl.BlockSpec((rows // chunk, lanes), lambda i, j: (i, j))],
        out_shape=[bf] * 7 + [f32, f32, jax.ShapeDtypeStruct((m // chunk, c), F32)],
        scratch_shapes=[pltpu.VMEM((rows, c), BF16)] * 3
        + [pltpu.VMEM((rows, w1.shape[1]), BF16), pltpu.VMEM((rows, a1.shape[1]), BF16),
           pltpu.VMEM((rows, g1.shape[1]), BF16)],
        compiler_params=_params("parallel", "arbitrary"),
        name="rwkv_front",
    )(h, h, mix, w_r, w_k, w_v, w1, w2, w0, a1, a2, a0, g1, g2, k_k, k_a, r_k, ones)


def _scan_kernel(at_ref, rt_ref, bt_ref, kt_ref, bh_ref, kh_ref, v_ref, gam_ref, y_ref, st_ref,
                 *, chunk, heads, groups):
    L, G, N = chunk, heads, RWKV_HEAD
    GN, GL = G * N, G * L
    assert L == N

    @pl.when(pl.program_id(2) == 0)
    def _():
        st_ref[...] = jnp.zeros_like(st_ref)

    seg_mask = (lax.broadcasted_iota(jnp.int32, (GN, GN), 0) // N
                == lax.broadcasted_iota(jnp.int32, (GN, GN), 1) // N)
    expand_mask = (lax.broadcasted_iota(jnp.int32, (GL, GN), 0) // L
                   == lax.broadcasted_iota(jnp.int32, (GL, GN), 1) // N)
    row_t = lax.broadcasted_iota(jnp.int32, (L, GN), 0)
    lane_s = lax.broadcasted_iota(jnp.int32, (L, GN), 1) % L
    strict_c = lane_s < row_t
    incl_c = lane_s <= row_t
    eye_c = (lane_s == row_t).astype(F32)

    def expand(x):
        xb = x.astype(BF16)
        return jnp.where(expand_mask, jnp.concatenate([xb] * G, axis=0), jnp.zeros((), BF16))

    def compact_dot(x, w_bd):
        return jnp.dot(x.astype(BF16), w_bd, preferred_element_type=F32)

    R = range(groups)
    lanes = [slice(gi * GN, (gi + 1) * GN) for gi in R]
    load = lambda ref: [ref[0, :, lanes[gi]] for gi in R]
    at, rt, bt, kt, bh, kh, v = (load(ref) for ref in
                                 (at_ref, rt_ref, bt_ref, kt_ref, bh_ref, kh_ref, v_ref))
    s0 = [st_ref[gi] for gi in R]
    gam = [gam_ref[0, :, lanes[gi]] for gi in R]

    ar = [jnp.concatenate([at[gi], rt[gi]], axis=0) for gi in R]
    bt_e = [expand(x) for x in bt]
    kt_e = [expand(x) for x in kt]
    v_e = [expand(x) for x in v]
    gram_b = [_dot_nt(ar[gi], bt_e[gi]) for gi in R]
    gram_k = [_dot_nt(ar[gi], kt_e[gi]) for gi in R]
    a_ab = [jnp.where(strict_c, gram_b[gi][:L], 0.0) for gi in R]
    a_rb = [jnp.where(incl_c, gram_b[gi][L:], 0.0) for gi in R]
    a_ak = [jnp.where(strict_c, gram_k[gi][:L], 0.0) for gi in R]
    a_rk = [jnp.where(incl_c, gram_k[gi][L:], 0.0) for gi in R]

    t_c = [eye_c + a_ab[gi] for gi in R]
    p_bd = [expand(a_ab[gi]) for gi in R]
    p_c = [compact_dot(a_ab[gi], p_bd[gi]) for gi in R]
    p_bd = [expand(p_c[gi]) for gi in R]
    for _ in range(L.bit_length() - 3):
        tp = [compact_dot(jnp.concatenate([t_c[gi], p_c[gi]], axis=0), p_bd[gi]) for gi in R]
        t_c = [t_c[gi] + tp[gi][:L] for gi in R]
        p_c = [tp[gi][L:] for gi in R]
        p_bd = [expand(p_c[gi]) for gi in R]
    t_c = [t_c[gi] + compact_dot(t_c[gi], p_bd[gi]) for gi in R]

    s0b = [x.astype(BF16) for x in s0]
    from_state = [_dot_nt(ar[gi], s0b[gi]) for gi in R]
    rhs = [from_state[gi][:L] + compact_dot(a_ak[gi], v_e[gi]) for gi in R]
    u = [compact_dot(t_c[gi], expand(rhs[gi])) for gi in R]
    y = [from_state[gi][L:] + compact_dot(a_rb[gi], expand(u[gi]))
         + compact_dot(a_rk[gi], v_e[gi]) for gi in R]
    uv = [jnp.concatenate([u[gi], v[gi].astype(F32)], axis=0) for gi in R]
    bk = [jnp.concatenate([bh[gi], kh[gi]], axis=0) for gi in R]
    upd = [_dot(uv[gi].T, bk[gi]) for gi in R]
    for gi in R:
        y_ref[0, :, lanes[gi]] = y[gi]
        st_ref[gi] = s0[gi] * gam[gi] + jnp.where(seg_mask, upd[gi], 0.0)


def _rwkv_scan(ops, gamma, batch, seq_len, chunk=SCAN_CHUNK, heads=SCAN_HEADS,
               groups=SCAN_GROUPS):
    c = gamma.shape[-1]
    gn = heads * RWKV_HEAD
    lanes = groups * gn
    n_chunks = seq_len // chunk
    seq = pl.BlockSpec((1, chunk, lanes), lambda b, hg, t: (b, t, hg))
    gam = pl.BlockSpec((1, 1, lanes), lambda b, hg, t: (b * n_chunks + t, 0, hg))
    return pl.pallas_call(
        functools.partial(_scan_kernel, chunk=chunk, heads=heads, groups=groups),
        grid=(batch, c // lanes, n_chunks),
        in_specs=[seq] * 7 + [gam],
        out_specs=seq,
        out_shape=jax.ShapeDtypeStruct((batch, seq_len, c), F32),
        scratch_shapes=[pltpu.VMEM((groups, gn, gn), F32)],
        compiler_params=_params("parallel", "parallel", "arbitrary"),
        name="rwkv_scan",
    )(*[o.reshape(batch, seq_len, c) for o in ops], gamma.reshape(batch * n_chunks, 1, c))


def _gn_gate_kernel(y_ref, bonus_ref, g_ref, lnw_ref, lnb_ref, ones_ref, z_ref):
    y = y_ref[...]
    ones = ones_ref[...]
    inv_n = 1.0 / RWKV_HEAD
    mu = _seg_sum(y, ones) * inv_n
    d = y - mu
    var = _seg_sum(d * d, ones) * inv_n
    yn = d * lax.rsqrt(var + GN_EPS) * lnw_ref[...] + lnb_ref[...]
    z_ref[...] = ((yn + bonus_ref[...]) * g_ref[...]).astype(z_ref.dtype)


def _gn_gate(y, bonus, g, lnx_w, lnx_b, rows=EW_ROWS, lanes=EW_LANES):
    m, c = y.shape
    tile = pl.BlockSpec((rows, lanes), lambda i, j: (i, j))
    vec = pl.BlockSpec((1, lanes), lambda i, j: (0, j))
    return pl.pallas_call(
        _gn_gate_kernel,
        grid=(m // rows, c // lanes),
        in_specs=[tile, tile, tile, vec, vec,
                  pl.BlockSpec((SEG_LANES, SEG_LANES), lambda i, j: (0, 0))],
        out_specs=tile,
        out_shape=jax.ShapeDtypeStruct((m, c), BF16),
        compiler_params=_params("parallel", "parallel"),
        name="rwkv_gn_gate",
    )(y, bonus, g, lnx_w, lnx_b, _seg_ones())


def _proj_ln_kernel(z_ref, w_ref, h_ref, lnw_ref, lnb_ref, o_ref):
    y = jnp.dot(z_ref[...], w_ref[...], preferred_element_type=F32)
    o_ref[...] = _layernorm(ALPHA * h_ref[...] + y, lnw_ref[...], lnb_ref[...])


def _proj_ln(z, w, h, ln_w, ln_b, tm=TM):
    m, kdim = z.shape
    c = w.shape[1]
    vec = pl.BlockSpec((1, c), lambda i: (0, 0))
    return pl.pallas_call(
        _proj_ln_kernel,
        grid=(m // tm,),
        in_specs=[pl.BlockSpec((tm, kdim), lambda i: (i, 0)),
                  pl.BlockSpec((kdim, c), lambda i: (0, 0)),
                  pl.BlockSpec((tm, c), lambda i: (i, 0)), vec, vec],
        out_specs=pl.BlockSpec((tm, c), lambda i: (i, 0)),
        out_shape=jax.ShapeDtypeStruct((m, c), F32),
        compiler_params=_params("parallel"),
        name="proj_residual_ln",
    )(z, w, h, ln_w, ln_b)


def _ffn_kernel(h_ref, wg_ref, wu_ref, wd_ref, lnw_ref, lnb_ref, o_ref, xb_ref, acc_ref):
    f = pl.program_id(1)

    @pl.when(f == 0)
    def _():
        xb_ref[...] = h_ref[...].astype(BF16)
        acc_ref[...] = jnp.zeros_like(acc_ref)

    x = xb_ref[...]
    gate = jnp.dot(x, wg_ref[...], preferred_element_type=F32)
    up = jnp.dot(x, wu_ref[...], preferred_element_type=F32)
    act = (gate * jax.nn.sigmoid(gate)) * up
    acc_ref[...] += jnp.dot(act.astype(BF16), wd_ref[...], preferred_element_type=F32)

    @pl.when(f == pl.num_programs(1) - 1)
    def _():
        o_ref[...] = _layernorm(ALPHA * h_ref[...] + acc_ref[...], lnw_ref[...], lnb_ref[...])


def _ffn_ln(h, w_gate, w_up, w_down, ln_w, ln_b, tm=TM, tf=512):
    m, c = h.shape
    d_ff = w_gate.shape[1]
    vec = pl.BlockSpec((1, c), lambda i, f: (0, 0))
    return pl.pallas_call(
        _ffn_kernel,
        grid=(m // tm, d_ff // tf),
        in_specs=[pl.BlockSpec((tm, c), lambda i, f: (i, 0)),
                  pl.BlockSpec((c, tf), lambda i, f: (0, f)),
                  pl.BlockSpec((c, tf), lambda i, f: (0, f)),
                  pl.BlockSpec((tf, c), lambda i, f: (f, 0)), vec, vec],
        out_specs=pl.BlockSpec((tm, c), lambda i, f: (i, 0)),
        out_shape=jax.ShapeDtypeStruct((m, c), F32),
        scratch_shapes=[pltpu.VMEM((tm, c), BF16), pltpu.VMEM((tm, c), F32)],
        compiler_params=_params("parallel", "arbitrary"),
        name="swiglu_ffn_ln",
    )(h, w_gate, w_up, w_down, ln_w, ln_b)


def _kv_kernel(x_ref, wk_ref, wvt_ref, k_ref, mean_ref, vt_ref, xb_ref):
    @pl.when(pl.program_id(1) == 0)
    def _():
        xb_ref[...] = x_ref[...].astype(BF16)

    xb = xb_ref[...]
    y = jnp.dot(xb, wk_ref[...], preferred_element_type=F32)
    k_ref[...] = y.astype(k_ref.dtype)
    tm, tn = y.shape
    nblk = tm // MOBA_BLOCK
    mean_ref[0] = jnp.sum(y.reshape(nblk, MOBA_BLOCK, tn), axis=1) * (1.0 / MOBA_BLOCK)
    vt = lax.dot_general(wvt_ref[...], xb, (((1,), (1,)), ((), ())), preferred_element_type=F32)
    for blk in range(nblk):
        vt_ref[blk] = vt[:, blk * MOBA_BLOCK:(blk + 1) * MOBA_BLOCK].astype(vt_ref.dtype)


def _kv_proj(h, w_k, w_v_t, tm=2 * TM, tn=512):
    m, c = h.shape
    n = w_k.shape[1]
    nblk = tm // MOBA_BLOCK
    return pl.pallas_call(
        _kv_kernel,
        grid=(m // tm, n // tn),
        in_specs=[pl.BlockSpec((tm, c), lambda i, j: (i, 0)),
                  pl.BlockSpec((c, tn), lambda i, j: (0, j)),
                  pl.BlockSpec((tn, c), lambda i, j: (j, 0))],
        out_specs=[pl.BlockSpec((tm, tn), lambda i, j: (i, j)),
                   pl.BlockSpec((1, nblk, tn), lambda i, j: (i, 0, j)),
                   pl.BlockSpec((nblk, tn, MOBA_BLOCK), lambda i, j: (i, j, 0))],
        out_shape=[jax.ShapeDtypeStruct((m, n), BF16),
                   jax.ShapeDtypeStruct((m // tm, nblk, n), F32),
                   jax.ShapeDtypeStruct((m // MOBA_BLOCK, n, MOBA_BLOCK), BF16)],
        scratch_shapes=[pltpu.VMEM((tm, c), BF16)],
        compiler_params=_params("parallel", "arbitrary"),
        name="shared_kv_proj",
    )(h, w_k, w_v_t)


def _q_kernel(x_ref, w_ref, o_ref, xb_ref):
    @pl.when(pl.program_id(1) == 0)
    def _():
        xb_ref[...] = x_ref[...].astype(BF16)

    o_ref[...] = jnp.dot(xb_ref[...], w_ref[...], preferred_element_type=F32)


def _q_proj(h, w_q, tm=TM, tn=1024):
    m, c = h.shape
    n = w_q.shape[1]
    return pl.pallas_call(
        _q_kernel,
        grid=(m // tm, n // tn),
        in_specs=[pl.BlockSpec((tm, c), lambda i, j: (i, 0)),
                  pl.BlockSpec((c, tn), lambda i, j: (0, j))],
        out_specs=pl.BlockSpec((tm, tn), lambda i, j: (i, j)),
        out_shape=jax.ShapeDtypeStruct((m, n), F32),
        scratch_shapes=[pltpu.VMEM((tm, c), BF16)],
        compiler_params=_params("parallel", "arbitrary"),
        name="moba_q_proj",
    )(h, w_q)


def _moba_kernel(q_ref, k_ref, vt_ref, km_ref, o_ref, sel_ref, *, n_blocks, top_k, heads):
    blk = pl.program_id(2)
    BLK, DH = MOBA_BLOCK, MOBA_HEAD_DIM
    start = pl.multiple_of(blk * BLK, BLK)
    ki = lax.broadcasted_iota(jnp.int32, (BLK, BLK), 0)
    qi = lax.broadcasted_iota(jnp.int32, (BLK, BLK), 1)
    bi = lax.broadcasted_iota(jnp.int32, (n_blocks, BLK), 0)
    cols = [slice(hh * DH, (hh + 1) * DH) for hh in range(heads)]

    R = range(heads)
    q = [q_ref[:, cols[h]] for h in R]
    qb = [(q[h] * DH ** -0.5).astype(BF16) for h in R]

    gate = [lax.dot_general(km_ref[0, :, cols[h]], q[h], (((1,), (1,)), ((), ())),
                            preferred_element_type=F32, precision=HIGHEST) for h in R]
    rank = [jnp.zeros((n_blocks, BLK), jnp.int32) for h in R]
    for mm in range(n_blocks):
        past = (mm < blk).astype(jnp.int32)
        for h in R:
            gm = gate[h][mm:mm + 1, :]
            ahead = (gm > gate[h]) | ((gm == gate[h]) & (mm < bi))
            rank[h] = rank[h] + jnp.where(ahead, 1, 0) * past
    for h in R:
        sel_ref[h] = jnp.where((bi < blk) & (rank[h] < top_k), 1.0, 0.0)

    s = [_dot_nt(k_ref[pl.ds(start, BLK), cols[h]], qb[h]) for h in R]
    s = [jnp.where(ki <= qi, s[h], NEG) for h in R]
    m0 = [jnp.max(s[h], axis=0, keepdims=True) for h in R]
    p = [jnp.exp(s[h] - m0[h]) for h in R]
    l0 = [jnp.sum(p[h], axis=0, keepdims=True) for h in R]
    acc0 = [jnp.dot(vt_ref[blk, cols[h], :], p[h].astype(BF16), preferred_element_type=F32)
            for h in R]

    def body(n, carry):
        m_i, l_i, acc = carry
        off = pl.multiple_of(n * BLK, BLK)
        sn = [_dot_nt(k_ref[pl.ds(off, BLK), cols[h]], qb[h]) for h in R]
        sn = [jnp.where(sel_ref[h, pl.ds(n, 1), :] > 0.0, sn[h], NEG) for h in R]
        m_new = [jnp.maximum(m_i[h], jnp.max(sn[h], axis=0, keepdims=True)) for h in R]
        corr = [jnp.exp(m_i[h] - m_new[h]) for h in R]
        pn = [jnp.exp(sn[h] - m_new[h]) for h in R]
        l_new = [corr[h] * l_i[h] + jnp.sum(pn[h], axis=0, keepdims=True) for h in R]
        pv = [jnp.dot(vt_ref[n, cols[h], :], pn[h].astype(BF16), preferred_element_type=F32)
              for h in R]
        acc_new = [corr[h] * acc[h] + pv[h] for h in R]
        return m_new, l_new, acc_new

    _, l_f, acc_f = lax.fori_loop(0, blk, body, (m0, l0, acc0))
    for h in R:
        o_ref[:, cols[h]] = (acc_f[h] / l_f[h]).T.astype(o_ref.dtype)


def _moba_attention(q, k, v_t, k_mean, batch, seq_len, heads=MOBA_HEADS_PER_STEP):
    m, hd = q.shape
    n_blocks = seq_len // MOBA_BLOCK
    top_k = max(1, min(MOBA_TOPK, n_blocks - 1))
    BLK, W = MOBA_BLOCK, heads * MOBA_HEAD_DIM
    return pl.pallas_call(
        functools.partial(_moba_kernel, n_blocks=n_blocks, top_k=top_k, heads=heads),
        grid=(batch, hd // W, n_blocks),
        in_specs=[pl.BlockSpec((BLK, W), lambda b, h, t: (b * n_blocks + t, h)),
                  pl.BlockSpec((seq_len, W), lambda b, h, t: (b, h)),
                  pl.BlockSpec((n_blocks, W, BLK), lambda b, h, t: (b, h, 0)),
                  pl.BlockSpec((1, n_blocks, W), lambda b, h, t: (b, 0, h))],
        out_specs=pl.BlockSpec((BLK, W), lambda b, h, t: (b * n_blocks + t, h)),
        out_shape=jax.ShapeDtypeStruct((m, hd), BF16),
        scratch_shapes=[pltpu.VMEM((heads, n_blocks, BLK), F32)],
        compiler_params=_params("parallel", "parallel", "arbitrary"),
        name="moba_attention",
    )(q, k, v_t, k_mean)


def _pad_lora(w_in, w_out):
    rank = w_in.shape[1]
    pad = -rank % LORA_PAD
    return (jnp.pad(w_in, ((0, 0), (0, pad))).astype(BF16),
            jnp.pad(w_out, ((0, pad), (0, 0))).astype(BF16))


def kernel(x, a_mix, a_w_r, a_w_k, a_w_v, a_w_o, a_w0, a_w1, a_w2, a_a0, a_a1, a_a2, a_g1, a_g2,
           a_k_k, a_k_a, a_r_k, a_lnx_w, a_lnx_b, kv_w_k, kv_w_v, b_w_q, b_w_o, ffn_w_gate,
           ffn_w_up, ffn_w_down, ln1_w, ln1_b, ln2_w, ln2_b):
    batch, seq_len, c = x.shape
    n_a = a_mix.shape[0]
    n_layers = ffn_w_gate.shape[0]
    assert seq_len % MOBA_BLOCK == 0 and seq_len % TM == 0 and c == D_MODEL
    h = x.reshape(batch * seq_len, c)
    row = lambda vec: vec.reshape(1, c)
    k_all = v_t = k_mean = None

    for layer in range(n_layers):
        if layer < n_a:
            i = layer
            w1, w2 = _pad_lora(a_w1[i], a_w2[i])
            a1, a2 = _pad_lora(a_a1[i], a_a2[i])
            g1, g2 = _pad_lora(a_g1[i], a_g2[i])
            *ops, bonus, g, gamma = _rwkv_front(
                h, a_mix[i], a_w_r[i].astype(BF16), a_w_k[i].astype(BF16), a_w_v[i].astype(BF16),
                w1, w2, row(a_w0[i]), a1, a2, row(a_a0[i]), g1, g2, row(a_k_k[i]),
                row(a_k_a[i]), row(a_r_k[i]), seq_len, SCAN_CHUNK)
            y = _rwkv_scan(ops, gamma, batch, seq_len).reshape(batch * seq_len, c)
            z = _gn_gate(y, bonus, g, row(a_lnx_w[i]), row(a_lnx_b[i]))
            w_o = a_w_o[i]
        else:
            j = layer - n_a
            q = _q_proj(h, b_w_q[j].astype(BF16))
            z = _moba_attention(q, k_all, v_t, k_mean, batch, seq_len)
            w_o = b_w_o[j]
        h = _proj_ln(z, w_o.astype(BF16), h, row(ln1_w[layer]), row(ln1_b[layer]))
        h = _ffn_ln(h, ffn_w_gate[layer].astype(BF16), ffn_w_up[layer].astype(BF16),
                    ffn_w_down[layer].astype(BF16), row(ln2_w[layer]), row(ln2_b[layer]))
        if layer == n_a - 1:
            k_all, k_mean, v_t = _kv_proj(h, kv_w_k.astype(BF16), kv_w_v.T.astype(BF16))
            k_mean = k_mean.reshape(batch, seq_len // MOBA_BLOCK, -1)
    return h.reshape(batch, seq_len, c)
```

```python
import functools

import jax
import jax.numpy as jnp
from jax import lax
from jax.experimental import pallas as pl
from jax.experimental.pallas import tpu as pltpu

D_MODEL = 2048
RWKV_HEAD = 64
MOBA_HEAD_DIM = 128
MOBA_HEADS = D_MODEL // MOBA_HEAD_DIM
MOBA_BLOCK = 256
MOBA_TOPK = 3
NEG = -1e30
GN_EPS = 64e-5
LN_EPS = 1e-5
DEPTH = 2
ALPHA = (2 * DEPTH) ** 0.25
LORA_PAD = 128
NEG_EXP_MINUS_HALF = -0.6065306597126334

VMEM_LIMIT_BYTES = 56 * 1024 * 1024

TM = 512
SCAN_CHUNK = 64
SCAN_HEADS = 4
SCAN_GROUPS = 8
MOBA_HEADS_PER_STEP = 8
EW_ROWS, EW_LANES = 512, 512
SEG_LANES = 256

F32 = jnp.float32
BF16 = jnp.bfloat16
HIGHEST = lax.Precision.HIGHEST


def _params(*semantics):
    return pltpu.CompilerParams(dimension_semantics=semantics, vmem_limit_bytes=VMEM_LIMIT_BYTES)


def _dot(a, b):
    return jnp.dot(a.astype(BF16), b.astype(BF16), preferred_element_type=F32)


def _dot_nt(a, b):
    return lax.dot_general(a.astype(BF16), b.astype(BF16), (((1,), (1,)), ((), ())),
                           preferred_element_type=F32)


def _seg_sum(x, seg_ones, split=True):
    width = seg_ones.shape[0]
    parts = []
    for c0 in range(0, x.shape[1], width):
        xs = x[:, c0:c0 + width]
        hi = xs.astype(BF16)
        acc = jnp.dot(hi, seg_ones, preferred_element_type=F32)
        if split:
            lo = (xs - hi.astype(F32)).astype(BF16)
            acc = acc + jnp.dot(lo, seg_ones, preferred_element_type=F32)
        parts.append(acc)
    return parts[0] if len(parts) == 1 else jnp.concatenate(parts, axis=1)


def _seg_ones():
    head = jnp.arange(SEG_LANES) // RWKV_HEAD
    return (head[:, None] == head[None, :]).astype(BF16)


def _layernorm(t, w, b):
    mu = jnp.mean(t, axis=-1, keepdims=True)
    d = t - mu
    var = jnp.mean(d * d, axis=-1, keepdims=True)
    return d * lax.rsqrt(var + LN_EPS) * w + b


def _token_shift_delta(x_ref, prev_ref, seq_start):
    x = x_ref[...]
    rolled = pltpu.roll(x, 1, 0)
    prev_last = jnp.where(seq_start, 0.0, prev_ref[7:8, :])
    row = lax.broadcasted_iota(jnp.int32, x.shape, 0)
    x_prev = jnp.where(row == 0, prev_last, rolled)
    return x, x_prev - x


def _front_kernel(x_ref, prev_ref, mix_ref, wr_ref, wk_ref, wv_ref, w1_ref, w2_ref, w0_ref,
                  a1_ref, a2_ref, a0_ref, g1_ref, g2_ref, kk_ref, ka_ref, rk_ref, ones_ref,
                  at_ref, rt_ref, bt_ref, kt_ref, bh_ref, kh_ref, vb_ref, bonus_ref, g_ref,
                  gam_ref, xr_ref, xk_ref, xv_ref, hw_ref, ha_ref, hg_ref,
                  *, tiles_per_seq, chunk):
    @pl.when(pl.program_id(1) == 0)
    def _():
        seq_start = (pl.program_id(0) % tiles_per_seq) == 0
        x, xx = _token_shift_delta(x_ref, prev_ref, seq_start)
        mixed = lambda idx: x + xx * mix_ref[idx:idx + 1, :]
        xr_ref[...] = mixed(0).astype(BF16)
        xk_ref[...] = mixed(2).astype(BF16)
        xv_ref[...] = mixed(3).astype(BF16)
        hw_ref[...] = jnp.tanh(_dot(mixed(1), w1_ref[...])).astype(BF16)
        ha_ref[...] = _dot(mixed(4), a1_ref[...]).astype(BF16)
        hg_ref[...] = jax.nn.sigmoid(_dot(mixed(5), g1_ref[...])).astype(BF16)

    dot = functools.partial(jnp.dot, preferred_element_type=F32)
    rows, lanes = at_ref.shape
    n_chunks = rows // chunk
    blocks = [slice(c0, c0 + SEG_LANES) for c0 in range(0, lanes, SEG_LANES)]
    xr, xk, xv, hw, ha, hg = (ref[...] for ref in (xr_ref, xk_ref, xv_ref, hw_ref, ha_ref, hg_ref))
    r = [dot(xr, wr_ref[:, cb]) for cb in blocks]
    k = [dot(xk, wk_ref[:, cb]) for cb in blocks]
    v = [dot(xv, wv_ref[:, cb]) for cb in blocks]
    w_pre = [w0_ref[:, cb] + dot(hw, w2_ref[:, cb]) for cb in blocks]
    a_pre = [a0_ref[:, cb] + dot(ha, a2_ref[:, cb]) for cb in blocks]
    for cb in blocks:
        g_ref[:, cb] = dot(hg, g2_ref[:, cb])

    ones = ones_ref[...]
    pos = lax.broadcasted_iota(jnp.int32, (rows, SEG_LANES), 0) % chunk
    for idx, cb in enumerate(blocks):
        ld = jax.nn.sigmoid(w_pre[idx]) * NEG_EXP_MINUS_HALF
        a = jax.nn.sigmoid(a_pre[idx])
        kk = k[idx] * kk_ref[:, cb]
        kk = kk * lax.rsqrt(jnp.maximum(_seg_sum(kk * kk, ones), 1e-24))
        k_mod = k[idx] * (1.0 + (a - 1.0) * ka_ref[:, cb])
        b_vec = kk * a

        cs = ld
        step = 1
        while step < chunk:
            cs = cs + jnp.where(pos >= step, pltpu.roll(cs, step, 0), 0.0)
            step *= 2
        cs3 = cs.reshape(n_chunks, chunk, SEG_LANES)
        cs_last = cs3[:, chunk - 1:chunk, :]
        e_tail = jnp.exp(cs_last - cs3).reshape(rows, SEG_LANES)
        e_neg = jnp.exp(-cs)

        at_ref[:, cb] = (-kk * jnp.exp(cs - ld)).astype(BF16)
        rt_ref[:, cb] = (r[idx] * jnp.exp(cs)).astype(BF16)
        bt_ref[:, cb] = (b_vec * e_neg).astype(BF16)
        kt_ref[:, cb] = (k_mod * e_neg).astype(BF16)
        bh_ref[:, cb] = (b_vec * e_tail).astype(BF16)
        kh_ref[:, cb] = (k_mod * e_tail).astype(BF16)
        vb_ref[:, cb] = v[idx].astype(BF16)
        bonus_ref[:, cb] = _seg_sum(r[idx] * k_mod * rk_ref[:, cb], ones, split=False) * v[idx]
        gam_ref[:, cb] = jnp.exp(cs_last).reshape(n_chunks, SEG_LANES)


def _rwkv_front(h, mix, w_r, w_k, w_v, w1, w2, w0, a1, a2, a0, g1, g2, k_k, k_a, r_k,
                seq_len, chunk, rows=EW_ROWS, lanes=EW_LANES):
    m, c = h.shape
    full = lambda arr: pl.BlockSpec(arr.shape, lambda i, j: (0,) * arr.ndim)
    cols = lambda arr: pl.BlockSpec((arr.shape[0], lanes), lambda i, j: (0, j))
    tile = pl.BlockSpec((rows, lanes), lambda i, j: (i, j))
    ones = _seg_ones()
    bf = jax.ShapeDtypeStruct((m, c), BF16)
    f32 = jax.ShapeDtypeStruct((m, c), F32)
    return pl.pallas_call(
        functools.partial(_front_kernel, tiles_per_seq=seq_len // rows, chunk=chunk),
        grid=(m // rows, c // lanes),
        in_specs=[pl.BlockSpec((rows, c), lambda i, j: (i, 0)),
                  pl.BlockSpec((8, c), lambda i, j: (jnp.maximum(i * (rows // 8) - 1, 0), 0)),
                  full(mix), cols(w_r), cols(w_k), cols(w_v),
                  full(w1), cols(w2), cols(w0), full(a1), cols(a2), cols(a0),
                  full(g1), cols(g2), cols(k_k), cols(k_a), cols(r_k), full(ones)],
        out_specs=[tile] * 9 + [pl.BlockSpec((rows // chunk, lanes), lambda i, j: (i, j))],
        out_shape=[bf] * 7 + [f32, f32, jax.ShapeDtypeStruct((m // chunk, c), F32)],
        scratch_shapes=[pltpu.VMEM((rows, c), BF16)] * 3
        + [pltpu.VMEM((rows, w1.shape[1]), BF16), pltpu.VMEM((rows, a1.shape[1]), BF16),
           pltpu.VMEM((rows, g1.shape[1]), BF16)],
        compiler_params=_params("parallel", "arbitrary"),
        name="rwkv_front",
    )(h, h, mix, w_r, w_k, w_v, w1, w2, w0, a1, a2, a0, g1, g2, k_k, k_a, r_k, ones)


def _scan_kernel(at_ref, rt_ref, bt_ref, kt_ref, bh_ref, kh_ref, v_ref, gam_ref, y_ref, st_ref,
                 *, chunk, heads, groups):
    L, G, N = chunk, heads, RWKV_HEAD
    GN, GL = G * N, G * L
    assert L == N

    @pl.when(pl.program_id(2) == 0)
    def _():
        st_ref[...] = jnp.zeros_like(st_ref)

    seg_mask = (lax.broadcasted_iota(jnp.int32, (GN, GN), 0) // N
                == lax.broadcasted_iota(jnp.int32, (GN, GN), 1) // N)
    expand_mask = (lax.broadcasted_iota(jnp.int32, (GL, GN), 0) // L
                   == lax.broadcasted_iota(jnp.int32, (GL, GN), 1) // N)
    row_t = lax.broadcasted_iota(jnp.int32, (L, GN), 0)
    lane_s = lax.broadcasted_iota(jnp.int32, (L, GN), 1) % L
    strict_c = lane_s < row_t
    incl_c = lane_s <= row_t
    eye_c = (lane_s == row_t).astype(F32)

    def expand(x):
        xb = x.astype(BF16)
        return jnp.where(expand_mask, jnp.concatenate([xb] * G, axis=0), jnp.zeros((), BF16))

    def compact_dot(x, w_bd):
        return jnp.dot(x.astype(BF16), w_bd, preferred_element_type=F32)

    R = range(groups)
    lanes = [slice(gi * GN, (gi + 1) * GN) for gi in R]
    load = lambda ref: [ref[0, :, lanes[gi]] for gi in R]
    at, rt, bt, kt, bh, kh, v = (load(ref) for ref in
                                 (at_ref, rt_ref, bt_ref, kt_ref, bh_ref, kh_ref, v_ref))
    s0 = [st_ref[gi] for gi in R]
    gam = [gam_ref[0, :, lanes[gi]] for gi in R]

    ar = [jnp.concatenate([at[gi], rt[gi]], axis=0) for gi in R]
    bt_e = [expand(x) for x in bt]
    kt_e = [expand(x) for x in kt]
    v_e = [expand(x) for x in v]
    gram_b = [_dot_nt(ar[gi], bt_e[gi]) for gi in R]
    gram_k = [_dot_nt(ar[gi], kt_e[gi]) for gi in R]
    a_ab = [jnp.where(strict_c, gram_b[gi][:L], 0.0) for gi in R]
    a_rb = [jnp.where(incl_c, gram_b[gi][L:], 0.0) for gi in R]
    a_ak = [jnp.where(strict_c, gram_k[gi][:L], 0.0) for gi in R]
    a_rk = [jnp.where(incl_c, gram_k[gi][L:], 0.0) for gi in R]

    t_c = [eye_c + a_ab[gi] for gi in R]
    p_bd = [expand(a_ab[gi]) for gi in R]
    p_c = [compact_dot(a_ab[gi], p_bd[gi]) for gi in R]
    p_bd = [expand(p_c[gi]) for gi in R]
    for _ in range(L.bit_length() - 3):
        tp = [compact_dot(jnp.concatenate([t_c[gi], p_c[gi]], axis=0), p_bd[gi]) for gi in R]
        t_c = [t_c[gi] + tp[gi][:L] for gi in R]
        p_c = [tp[gi][L:] for gi in R]
        p_bd = [expand(p_c[gi]) for gi in R]
    t_c = [t_c[gi] + compact_dot(t_c[gi], p_bd[gi]) for gi in R]

    s0b = [x.astype(BF16) for x in s0]
    from_state = [_dot_nt(ar[gi], s0b[gi]) for gi in R]
    rhs = [from_state[gi][:L] + compact_dot(a_ak[gi], v_e[gi]) for gi in R]
    u = [compact_dot(t_c[gi], expand(rhs[gi])) for gi in R]
    y = [from_state[gi][L:] + compact_dot(a_rb[gi], expand(u[gi]))
         + compact_dot(a_rk[gi], v_e[gi]) for gi in R]
    uv = [jnp.concatenate([u[gi], v[gi].astype(F32)], axis=0) for gi in R]
    bk = [jnp.concatenate([bh[gi], kh[gi]], axis=0) for gi in R]
    upd = [_dot(uv[gi].T, bk[gi]) for gi in R]
    for gi in R:
        y_ref[0, :, lanes[gi]] = y[gi]
        st_ref[gi] = s0[gi] * gam[gi] + jnp.where(seg_mask, upd[gi], 0.0)


def _rwkv_scan(ops, gamma, batch, seq_len, chunk=SCAN_CHUNK, heads=SCAN_HEADS,
               groups=SCAN_GROUPS):
    c = gamma.shape[-1]
    gn = heads * RWKV_HEAD
    lanes = groups * gn
    n_chunks = seq_len // chunk
    seq = pl.BlockSpec((1, chunk, lanes), lambda b, hg, t: (b, t, hg))
    gam = pl.BlockSpec((1, 1, lanes), lambda b, hg, t: (b * n_chunks + t, 0, hg))
    return pl.pallas_call(
        functools.partial(_scan_kernel, chunk=chunk, heads=heads, groups=groups),
        grid=(batch, c // lanes, n_chunks),
        in_specs=[seq] * 7 + [gam],
        out_specs=seq,
        out_shape=jax.ShapeDtypeStruct((batch, seq_len, c), F32),
        scratch_shapes=[pltpu.VMEM((groups, gn, gn), F32)],
        compiler_params=_params("parallel", "parallel", "arbitrary"),
        name="rwkv_scan",
    )(*[o.reshape(batch, seq_len, c) for o in ops], gamma.reshape(batch * n_chunks, 1, c))


def _gn_proj_ln_kernel(y_ref, bonus_ref, g_ref, gnw_ref, gnb_ref, ones_ref, w_ref, h_ref,
                       lnw_ref, lnb_ref, o_ref):
    ones = ones_ref[...]
    inv_n = 1.0 / RWKV_HEAD
    z = []
    for c0 in range(0, y_ref.shape[1], SEG_LANES):
        cb = slice(c0, c0 + SEG_LANES)
        y = y_ref[:, cb]
        mu = _seg_sum(y, ones) * inv_n
        d = y - mu
        var = _seg_sum(d * d, ones) * inv_n
        yn = d * lax.rsqrt(var + GN_EPS) * gnw_ref[:, cb] + gnb_ref[:, cb]
        z.append(((yn + bonus_ref[:, cb]) * g_ref[:, cb]).astype(BF16))
    proj = jnp.dot(jnp.concatenate(z, axis=1), w_ref[...], preferred_element_type=F32)
    o_ref[...] = _layernorm(ALPHA * h_ref[...] + proj, lnw_ref[...], lnb_ref[...])


def _gn_proj_ln(y, bonus, g, gn_w, gn_b, w, h, ln_w, ln_b, tm=TM // 2):
    m, c = y.shape
    tile = pl.BlockSpec((tm, c), lambda i: (i, 0))
    vec = pl.BlockSpec((1, c), lambda i: (0, 0))
    return pl.pallas_call(
        _gn_proj_ln_kernel,
        grid=(m // tm,),
        in_specs=[tile, tile, tile, vec, vec,
                  pl.BlockSpec((SEG_LANES, SEG_LANES), lambda i: (0, 0)),
                  pl.BlockSpec((c, c), lambda i: (0, 0)), tile, vec, vec],
        out_specs=tile,
        out_shape=jax.ShapeDtypeStruct((m, c), F32),
        compiler_params=_params("parallel"),
        name="rwkv_gn_proj_ln",
    )(y, bonus, g, gn_w, gn_b, _seg_ones(), w, h, ln_w, ln_b)


def _proj_ln_kernel(z_ref, w_ref, h_ref, lnw_ref, lnb_ref, o_ref):
    y = jnp.dot(z_ref[...], w_ref[...], preferred_element_type=F32)
    o_ref[...] = _layernorm(ALPHA * h_ref[...] + y, lnw_ref[...], lnb_ref[...])


def _proj_ln(z, w, h, ln_w, ln_b, tm=TM):
    m, kdim = z.shape
    c = w.shape[1]
    vec = pl.BlockSpec((1, c), lambda i: (0, 0))
    return pl.pallas_call(
        _proj_ln_kernel,
        grid=(m // tm,),
        in_specs=[pl.BlockSpec((tm, kdim), lambda i: (i, 0)),
                  pl.BlockSpec((kdim, c), lambda i: (0, 0)),
                  pl.BlockSpec((tm, c), lambda i: (i, 0)), vec, vec],
        out_specs=pl.BlockSpec((tm, c), lambda i: (i, 0)),
        out_shape=jax.ShapeDtypeStruct((m, c), F32),
        compiler_params=_params("parallel"),
        name="proj_residual_ln",
    )(z, w, h, ln_w, ln_b)


def _ffn_kernel(h_ref, wg_ref, wu_ref, wd_ref, lnw_ref, lnb_ref, o_ref, xb_ref, acc_ref):
    f = pl.program_id(1)

    @pl.when(f == 0)
    def _():
        xb_ref[...] = h_ref[...].astype(BF16)
        acc_ref[...] = jnp.zeros_like(acc_ref)

    x = xb_ref[...]
    gate = jnp.dot(x, wg_ref[...], preferred_element_type=F32)
    up = jnp.dot(x, wu_ref[...], preferred_element_type=F32)
    act = (gate * jax.nn.sigmoid(gate)) * up
    acc_ref[...] += jnp.dot(act.astype(BF16), wd_ref[...], preferred_element_type=F32)

    @pl.when(f == pl.num_programs(1) - 1)
    def _():
        o_ref[...] = _layernorm(ALPHA * h_ref[...] + acc_ref[...], lnw_ref[...], lnb_ref[...])


def _ffn_ln(h, w_gate, w_up, w_down, ln_w, ln_b, tm=TM, tf=512):
    m, c = h.shape
    d_ff = w_gate.shape[1]
    vec = pl.BlockSpec((1, c), lambda i, f: (0, 0))
    return pl.pallas_call(
        _ffn_kernel,
        grid=(m // tm, d_ff // tf),
        in_specs=[pl.BlockSpec((tm, c), lambda i, f: (i, 0)),
                  pl.BlockSpec((c, tf), lambda i, f: (0, f)),
                  pl.BlockSpec((c, tf), lambda i, f: (0, f)),
                  pl.BlockSpec((tf, c), lambda i, f: (f, 0)), vec, vec],
        out_specs=pl.BlockSpec((tm, c), lambda i, f: (i, 0)),
        out_shape=jax.ShapeDtypeStruct((m, c), F32),
        scratch_shapes=[pltpu.VMEM((tm, c), BF16), pltpu.VMEM((tm, c), F32)],
        compiler_params=_params("parallel", "arbitrary"),
        name="swiglu_ffn_ln",
    )(h, w_gate, w_up, w_down, ln_w, ln_b)


def _kv_kernel(x_ref, wk_ref, wvt_ref, k_ref, mean_ref, vt_ref, xb_ref):
    @pl.when(pl.program_id(1) == 0)
    def _():
        xb_ref[...] = x_ref[...].astype(BF16)

    xb = xb_ref[...]
    y = jnp.dot(xb, wk_ref[...], preferred_element_type=F32)
    k_ref[...] = y.astype(k_ref.dtype)
    tm, tn = y.shape
    nblk = tm // MOBA_BLOCK
    mean_ref[0] = jnp.sum(y.reshape(nblk, MOBA_BLOCK, tn), axis=1) * (1.0 / MOBA_BLOCK)
    vt = lax.dot_general(wvt_ref[...], xb, (((1,), (1,)), ((), ())), preferred_element_type=F32)
    for blk in range(nblk):
        vt_ref[blk] = vt[:, blk * MOBA_BLOCK:(blk + 1) * MOBA_BLOCK].astype(vt_ref.dtype)


def _kv_proj(h, w_k, w_v_t, tm=2 * TM, tn=512):
    m, c = h.shape
    n = w_k.shape[1]
    nblk = tm // MOBA_BLOCK
    return pl.pallas_call(
        _kv_kernel,
        grid=(m // tm, n // tn),
        in_specs=[pl.BlockSpec((tm, c), lambda i, j: (i, 0)),
                  pl.BlockSpec((c, tn), lambda i, j: (0, j)),
                  pl.BlockSpec((tn, c), lambda i, j: (j, 0))],
        out_specs=[pl.BlockSpec((tm, tn), lambda i, j: (i, j)),
                   pl.BlockSpec((1, nblk, tn), lambda i, j: (i, 0, j)),
                   pl.BlockSpec((nblk, tn, MOBA_BLOCK), lambda i, j: (i, j, 0))],
        out_shape=[jax.ShapeDtypeStruct((m, n), BF16),
                   jax.ShapeDtypeStruct((m // tm, nblk, n), F32),
                   jax.ShapeDtypeStruct((m // MOBA_BLOCK, n, MOBA_BLOCK), BF16)],
        scratch_shapes=[pltpu.VMEM((tm, c), BF16)],
        compiler_params=_params("parallel", "arbitrary"),
        name="shared_kv_proj",
    )(h, w_k, w_v_t)


def _q_kernel(x_ref, w_ref, o_ref, xb_ref):
    @pl.when(pl.program_id(1) == 0)
    def _():
        xb_ref[...] = x_ref[...].astype(BF16)

    o_ref[...] = jnp.dot(xb_ref[...], w_ref[...], preferred_element_type=F32)


def _q_proj(h, w_q, tm=2 * TM, tn=1024):
    m, c = h.shape
    n = w_q.shape[1]
    return pl.pallas_call(
        _q_kernel,
        grid=(m // tm, n // tn),
        in_specs=[pl.BlockSpec((tm, c), lambda i, j: (i, 0)),
                  pl.BlockSpec((c, tn), lambda i, j: (0, j))],
        out_specs=pl.BlockSpec((tm, tn), lambda i, j: (i, j)),
        out_shape=jax.ShapeDtypeStruct((m, n), F32),
        scratch_shapes=[pltpu.VMEM((tm, c), BF16)],
        compiler_params=_params("parallel", "arbitrary"),
        name="moba_q_proj",
    )(h, w_q)


def _moba_kernel(q_ref, k_ref, vt_ref, km_ref, o_ref, sel_ref, *, n_blocks, top_k, heads):
    blk = pl.program_id(2)
    BLK, DH = MOBA_BLOCK, MOBA_HEAD_DIM
    start = pl.multiple_of(blk * BLK, BLK)
    ki = lax.broadcasted_iota(jnp.int32, (BLK, BLK), 0)
    qi = lax.broadcasted_iota(jnp.int32, (BLK, BLK), 1)
    bi = lax.broadcasted_iota(jnp.int32, (n_blocks, BLK), 0)
    cols = [slice(hh * DH, (hh + 1) * DH) for hh in range(heads)]

    R = range(heads)
    q = [q_ref[:, cols[h]] for h in R]
    qb = [(q[h] * DH ** -0.5).astype(BF16) for h in R]

    gate = [lax.dot_general(km_ref[0, :, cols[h]], q[h], (((1,), (1,)), ((), ())),
                            preferred_element_type=F32, precision=HIGHEST) for h in R]
    rank = [jnp.zeros((n_blocks, BLK), jnp.int32) for h in R]
    for mm in range(n_blocks):
        past = (mm < blk).astype(jnp.int32)
        for h in R:
            gm = gate[h][mm:mm + 1, :]
            ahead = (gm > gate[h]) | ((gm == gate[h]) & (mm < bi))
            rank[h] = rank[h] + jnp.where(ahead, 1, 0) * past
    for h in R:
        sel_ref[h] = jnp.where((bi < blk) & (rank[h] < top_k), 1.0, 0.0)

    s = [_dot_nt(k_ref[pl.ds(start, BLK), cols[h]], qb[h]) for h in R]
    s = [jnp.where(ki <= qi, s[h], NEG) for h in R]
    m0 = [jnp.max(s[h], axis=0, keepdims=True) for h in R]
    p = [jnp.exp(s[h] - m0[h]) for h in R]
    l0 = [jnp.sum(p[h], axis=0, keepdims=True) for h in R]
    acc0 = [jnp.dot(vt_ref[blk, cols[h], :], p[h].astype(BF16), preferred_element_type=F32)
            for h in R]

    def body(n, carry):
        m_i, l_i, acc = carry
        off = pl.multiple_of(n * BLK, BLK)
        sn = [_dot_nt(k_ref[pl.ds(off, BLK), cols[h]], qb[h]) for h in R]
        sn = [jnp.where(sel_ref[h, pl.ds(n, 1), :] > 0.0, sn[h], NEG) for h in R]
        m_new = [jnp.maximum(m_i[h], jnp.max(sn[h], axis=0, keepdims=True)) for h in R]
        corr = [jnp.exp(m_i[h] - m_new[h]) for h in R]
        pn = [jnp.exp(sn[h] - m_new[h]) for h in R]
        l_new = [corr[h] * l_i[h] + jnp.sum(pn[h], axis=0, keepdims=True) for h in R]
        pv = [jnp.dot(vt_ref[n, cols[h], :], pn[h].astype(BF16), preferred_element_type=F32)
              for h in R]
        acc_new = [corr[h] * acc[h] + pv[h] for h in R]
        return m_new, l_new, acc_new

    _, l_f, acc_f = lax.fori_loop(0, blk, body, (m0, l0, acc0))
    for h in R:
        o_ref[:, cols[h]] = (acc_f[h] / l_f[h]).T.astype(o_ref.dtype)


def _moba_attention(q, k, v_t, k_mean, batch, seq_len, heads=MOBA_HEADS_PER_STEP):
    m, hd = q.shape
    n_blocks = seq_len // MOBA_BLOCK
    top_k = max(1, min(MOBA_TOPK, n_blocks - 1))
    BLK, W = MOBA_BLOCK, heads * MOBA_HEAD_DIM
    return pl.pallas_call(
        functools.partial(_moba_kernel, n_blocks=n_blocks, top_k=top_k, heads=heads),
        grid=(batch, hd // W, n_blocks),
        in_specs=[pl.BlockSpec((BLK, W), lambda b, h, t: (b * n_blocks + t, h)),
                  pl.BlockSpec((seq_len, W), lambda b, h, t: (b, h)),
                  pl.BlockSpec((n_blocks, W, BLK), lambda b, h, t: (b, h, 0)),
                  pl.BlockSpec((1, n_blocks, W), lambda b, h, t: (b, 0, h))],
        out_specs=pl.BlockSpec((BLK, W), lambda b, h, t: (b * n_blocks + t, h)),
        out_shape=jax.ShapeDtypeStruct((m, hd), BF16),
        scratch_shapes=[pltpu.VMEM((heads, n_blocks, BLK), F32)],
        compiler_params=_params("parallel", "parallel", "arbitrary"),
        name="moba_attention",
    )(q, k, v_t, k_mean)


def _pad_lora(w_in, w_out):
    rank = w_in.shape[1]
    pad = -rank % LORA_PAD
    return (jnp.pad(w_in, ((0, 0), (0, pad))).astype(BF16),
            jnp.pad(w_out, ((0, pad), (0, 0))).astype(BF16))


def kernel(x, a_mix, a_w_r, a_w_k, a_w_v, a_w_o, a_w0, a_w1, a_w2, a_a0, a_a1, a_a2, a_g1, a_g2,
           a_k_k, a_k_a, a_r_k, a_lnx_w, a_lnx_b, kv_w_k, kv_w_v, b_w_q, b_w_o, ffn_w_gate,
           ffn_w_up, ffn_w_down, ln1_w, ln1_b, ln2_w, ln2_b):
    batch, seq_len, c = x.shape
    n_a = a_mix.shape[0]
    n_layers = ffn_w_gate.shape[0]
    assert seq_len % MOBA_BLOCK == 0 and seq_len % TM == 0 and c == D_MODEL
    h = x.reshape(batch * seq_len, c)
    row = lambda vec: vec.reshape(1, c)
    k_all = v_t = k_mean = None

    for layer in range(n_layers):
        if layer < n_a:
            i = layer
            w1, w2 = _pad_lora(a_w1[i], a_w2[i])
            a1, a2 = _pad_lora(a_a1[i], a_a2[i])
            g1, g2 = _pad_lora(a_g1[i], a_g2[i])
            *ops, bonus, g, gamma = _rwkv_front(
                h, a_mix[i], a_w_r[i].astype(BF16), a_w_k[i].astype(BF16), a_w_v[i].astype(BF16),
                w1, w2, row(a_w0[i]), a1, a2, row(a_a0[i]), g1, g2, row(a_k_k[i]),
                row(a_k_a[i]), row(a_r_k[i]), seq_len, SCAN_CHUNK)
            y = _rwkv_scan(ops, gamma, batch, seq_len).reshape(batch * seq_len, c)
            h = _gn_proj_ln(y, bonus, g, row(a_lnx_w[i]), row(a_lnx_b[i]),
                            a_w_o[i].astype(BF16), h, row(ln1_w[layer]), row(ln1_b[layer]))
        else:
            j = layer - n_a
            q = _q_proj(h, b_w_q[j].astype(BF16))
            z = _moba_attention(q, k_all, v_t, k_mean, batch, seq_len)
            h = _proj_ln(z, b_w_o[j].astype(BF16), h, row(ln1_w[layer]), row(ln1_b[layer]))
        h = _ffn_ln(h, ffn_w_gate[layer].astype(BF16), ffn_w_up[layer].astype(BF16),
                    ffn_w_down[layer].astype(BF16), row(ln2_w[layer]), row(ln2_b[layer]))
        if layer == n_a - 1:
            k_all, k_mean, v_t = _kv_proj(h, kv_w_k.astype(BF16), kv_w_v.T.astype(BF16))
            k_mean = k_mean.reshape(batch, seq_len // MOBA_BLOCK, -1)
    return h.reshape(batch, seq_len, c)
```

```python
import functools

import jax
import jax.numpy as jnp
from jax import lax
from jax.experimental import pallas as pl
from jax.experimental.pallas import tpu as pltpu

D_MODEL = 2048
RWKV_HEAD = 64
MOBA_HEAD_DIM = 128
MOBA_HEADS = D_MODEL // MOBA_HEAD_DIM
MOBA_BLOCK = 256
MOBA_TOPK = 3
NEG = -1e30
GN_EPS = 64e-5
LN_EPS = 1e-5
DEPTH = 2
ALPHA = (2 * DEPTH) ** 0.25
LORA_PAD = 128
NEG_EXP_MINUS_HALF = -0.6065306597126334

VMEM_LIMIT_BYTES = 56 * 1024 * 1024

TM = 512
SCAN_CHUNK = 64
SCAN_HEADS = 4
SCAN_GROUPS = 8
MOBA_HEADS_PER_STEP = 8
EW_ROWS, EW_LANES = 512, 512
SEG_LANES = 256

F32 = jnp.float32
BF16 = jnp.bfloat16
HIGHEST = lax.Precision.HIGHEST


def _params(*semantics):
    return pltpu.CompilerParams(dimension_semantics=semantics, vmem_limit_bytes=VMEM_LIMIT_BYTES)


def _dot(a, b):
    return jnp.dot(a.astype(BF16), b.astype(BF16), preferred_element_type=F32)


def _dot_nt(a, b):
    return lax.dot_general(a.astype(BF16), b.astype(BF16), (((1,), (1,)), ((), ())),
                           preferred_element_type=F32)


def _seg_sum(x, seg_ones, split=True):
    width = seg_ones.shape[0]
    parts = []
    for c0 in range(0, x.shape[1], width):
        xs = x[:, c0:c0 + width]
        hi = xs.astype(BF16)
        acc = jnp.dot(hi, seg_ones, preferred_element_type=F32)
        if split:
            lo = (xs - hi.astype(F32)).astype(BF16)
            acc = acc + jnp.dot(lo, seg_ones, preferred_element_type=F32)
        parts.append(acc)
    return parts[0] if len(parts) == 1 else jnp.concatenate(parts, axis=1)


def _seg_ones():
    head = jnp.arange(SEG_LANES) // RWKV_HEAD
    return (head[:, None] == head[None, :]).astype(BF16)


def _layernorm(t, w, b):
    mu = jnp.mean(t, axis=-1, keepdims=True)
    d = t - mu
    var = jnp.mean(d * d, axis=-1, keepdims=True)
    return d * lax.rsqrt(var + LN_EPS) * w + b


def _token_shift_delta(x_ref, prev_ref, seq_start):
    x = x_ref[...]
    rolled = pltpu.roll(x, 1, 0)
    prev_last = jnp.where(seq_start, 0.0, prev_ref[7:8, :])
    row = lax.broadcasted_iota(jnp.int32, x.shape, 0)
    x_prev = jnp.where(row == 0, prev_last, rolled)
    return x, x_prev - x


def _front_kernel(x_ref, prev_ref, mix_ref, wr_ref, wk_ref, wv_ref, w1_ref, w2_ref, w0_ref,
                  a1_ref, a2_ref, a0_ref, g1_ref, g2_ref, kk_ref, ka_ref, rk_ref, ones_ref,
                  at_ref, rt_ref, bt_ref, kt_ref, bh_ref, kh_ref, vb_ref, bonus_ref, g_ref,
                  gam_ref, xr_ref, xk_ref, xv_ref, hw_ref, ha_ref, hg_ref,
                  *, tiles_per_seq, chunk):
    @pl.when(pl.program_id(1) == 0)
    def _():
        seq_start = (pl.program_id(0) % tiles_per_seq) == 0
        x, xx = _token_shift_delta(x_ref, prev_ref, seq_start)
        mixed = lambda idx: x + xx * mix_ref[idx:idx + 1, :]
        xr_ref[...] = mixed(0).astype(BF16)
        xk_ref[...] = mixed(2).astype(BF16)
        xv_ref[...] = mixed(3).astype(BF16)
        hw_ref[...] = jnp.tanh(_dot(mixed(1), w1_ref[...])).astype(BF16)
        ha_ref[...] = _dot(mixed(4), a1_ref[...]).astype(BF16)
        hg_ref[...] = jax.nn.sigmoid(_dot(mixed(5), g1_ref[...])).astype(BF16)

    dot = functools.partial(jnp.dot, preferred_element_type=F32)
    rows, lanes = at_ref.shape
    n_chunks = rows // chunk
    blocks = [slice(c0, c0 + SEG_LANES) for c0 in range(0, lanes, SEG_LANES)]
    xr, xk, xv, hw, ha, hg = (ref[...] for ref in (xr_ref, xk_ref, xv_ref, hw_ref, ha_ref, hg_ref))
    r = [dot(xr, wr_ref[:, cb]) for cb in blocks]
    k = [dot(xk, wk_ref[:, cb]) for cb in blocks]
    v = [dot(xv, wv_ref[:, cb]) for cb in blocks]
    w_pre = [w0_ref[:, cb] + dot(hw, w2_ref[:, cb]) for cb in blocks]
    a_pre = [a0_ref[:, cb] + dot(ha, a2_ref[:, cb]) for cb in blocks]
    for cb in blocks:
        g_ref[:, cb] = dot(hg, g2_ref[:, cb])

    ones = ones_ref[...]
    pos = lax.broadcasted_iota(jnp.int32, (rows, SEG_LANES), 0) % chunk
    for idx, cb in enumerate(blocks):
        ld = jax.nn.sigmoid(w_pre[idx]) * NEG_EXP_MINUS_HALF
        a = jax.nn.sigmoid(a_pre[idx])
        kk = k[idx] * kk_ref[:, cb]
        kk = kk * lax.rsqrt(jnp.maximum(_seg_sum(kk * kk, ones), 1e-24))
        k_mod = k[idx] * (1.0 + (a - 1.0) * ka_ref[:, cb])
        b_vec = kk * a

        cs = ld
        step = 1
        while step < chunk:
            cs = cs + jnp.where(pos >= step, pltpu.roll(cs, step, 0), 0.0)
            step *= 2
        cs3 = cs.reshape(n_chunks, chunk, SEG_LANES)
        cs_last = cs3[:, chunk - 1:chunk, :]
        e_tail = jnp.exp(cs_last - cs3).reshape(rows, SEG_LANES)
        e_neg = jnp.exp(-cs)

        at_ref[:, cb] = (-kk * jnp.exp(cs - ld)).astype(BF16)
        rt_ref[:, cb] = (r[idx] * jnp.exp(cs)).astype(BF16)
        bt_ref[:, cb] = (b_vec * e_neg).astype(BF16)
        kt_ref[:, cb] = (k_mod * e_neg).astype(BF16)
        bh_ref[:, cb] = (b_vec * e_tail).astype(BF16)
        kh_ref[:, cb] = (k_mod * e_tail).astype(BF16)
        vb_ref[:, cb] = v[idx].astype(BF16)
        bonus_ref[:, cb] = _seg_sum(r[idx] * k_mod * rk_ref[:, cb], ones, split=False) * v[idx]
        gam_ref[:, cb] = jnp.exp(cs_last).reshape(n_chunks, SEG_LANES)


def _rwkv_front(h, mix, w_r, w_k, w_v, w1, w2, w0, a1, a2, a0, g1, g2, k_k, k_a, r_k,
                seq_len, chunk, rows=EW_ROWS, lanes=EW_LANES):
    m, c = h.shape
    full = lambda arr: pl.BlockSpec(arr.shape, lambda i, j: (0,) * arr.ndim)
    cols = lambda arr: pl.BlockSpec((arr.shape[0], lanes), lambda i, j: (0, j))
    tile = pl.BlockSpec((rows, lanes), lambda i, j: (i, j))
    ones = _seg_ones()
    bf = jax.ShapeDtypeStruct((m, c), BF16)
    f32 = jax.ShapeDtypeStruct((m, c), F32)
    return pl.pallas_call(
        functools.partial(_front_kernel, tiles_per_seq=seq_len // rows, chunk=chunk),
        grid=(m // rows, c // lanes),
        in_specs=[pl.BlockSpec((rows, c), lambda i, j: (i, 0)),
                  pl.BlockSpec((8, c), lambda i, j: (jnp.maximum(i * (rows // 8) - 1, 0), 0)),
                  full(mix), cols(w_r), cols(w_k), cols(w_v),
                  full(w1), cols(w2), cols(w0), full(a1), cols(a2), cols(a0),
                  full(g1), cols(g2), cols(k_k), cols(k_a), cols(r_k), full(ones)],
        out_specs=[tile] * 9 + [pl.BlockSpec((rows // chunk, lanes), lambda i, j: (i, j))],
        out_shape=[bf] * 7 + [f32, f32, jax.ShapeDtypeStruct((m // chunk, c), F32)],
        scratch_shapes=[pltpu.VMEM((rows, c), BF16)] * 3
        + [pltpu.VMEM((rows, w1.shape[1]), BF16), pltpu.VMEM((rows, a1.shape[1]), BF16),
           pltpu.VMEM((rows, g1.shape[1]), BF16)],
        compiler_params=_params("parallel", "arbitrary"),
        name="rwkv_front",
    )(h, h, mix, w_r, w_k, w_v, w1, w2, w0, a1, a2, a0, g1, g2, k_k, k_a, r_k, ones)


def _scan_kernel(at_ref, rt_ref, bt_ref, kt_ref, bh_ref, kh_ref, v_ref, gam_ref, y_ref, st_ref,
                 *, chunk, heads, groups):
    L, G, N = chunk, heads, RWKV_HEAD
    GN, GL = G * N, G * L
    assert L == N

    @pl.when(pl.program_id(2) == 0)
    def _():
        st_ref[...] = jnp.zeros_like(st_ref)

    seg_mask = (lax.broadcasted_iota(jnp.int32, (GN, GN), 0) // N
                == lax.broadcasted_iota(jnp.int32, (GN, GN), 1) // N)
    expand_mask = (lax.broadcasted_iota(jnp.int32, (GL, GN), 0) // L
                   == lax.broadcasted_iota(jnp.int32, (GL, GN), 1) // N)
    row_t = lax.broadcasted_iota(jnp.int32, (L, GN), 0)
    lane_s = lax.broadcasted_iota(jnp.int32, (L, GN), 1) % L
    strict_c = lane_s < row_t
    incl_c = lane_s <= row_t
    eye_c = (lane_s == row_t).astype(F32)

    def expand(x):
        xb = x.astype(BF16)
        return jnp.where(expand_mask, jnp.concatenate([xb] * G, axis=0), jnp.zeros((), BF16))

    def compact_dot(x, w_bd):
        return jnp.dot(x.astype(BF16), w_bd, preferred_element_type=F32)

    R = range(groups)
    lanes = [slice(gi * GN, (gi + 1) * GN) for gi in R]
    load = lambda ref: [ref[0, :, lanes[gi]] for gi in R]
    at, rt, bt, kt, bh, kh, v = (load(ref) for ref in
                                 (at_ref, rt_ref, bt_ref, kt_ref, bh_ref, kh_ref, v_ref))
    s0 = [st_ref[gi] for gi in R]
    gam = [gam_ref[0, :, lanes[gi]] for gi in R]

    ar = [jnp.concatenate([at[gi], rt[gi]], axis=0) for gi in R]
    bt_e = [expand(x) for x in bt]
    kt_e = [expand(x) for x in kt]
    v_e = [expand(x) for x in v]
    gram_b = [_dot_nt(ar[gi], bt_e[gi]) for gi in R]
    gram_k = [_dot_nt(ar[gi], kt_e[gi]) for gi in R]
    a_ab = [jnp.where(strict_c, gram_b[gi][:L], 0.0) for gi in R]
    a_rb = [jnp.where(incl_c, gram_b[gi][L:], 0.0) for gi in R]
    a_ak = [jnp.where(strict_c, gram_k[gi][:L], 0.0) for gi in R]
    a_rk = [jnp.where(incl_c, gram_k[gi][L:], 0.0) for gi in R]

    t_c = [eye_c + a_ab[gi] for gi in R]
    p_bd = [expand(a_ab[gi]) for gi in R]
    p_c = [compact_dot(a_ab[gi], p_bd[gi]) for gi in R]
    p_bd = [expand(p_c[gi]) for gi in R]
    for _ in range(L.bit_length() - 3):
        tp = [compact_dot(jnp.concatenate([t_c[gi], p_c[gi]], axis=0), p_bd[gi]) for gi in R]
        t_c = [t_c[gi] + tp[gi][:L] for gi in R]
        p_c = [tp[gi][L:] for gi in R]
        p_bd = [expand(p_c[gi]) for gi in R]
    t_c = [t_c[gi] + compact_dot(t_c[gi], p_bd[gi]) for gi in R]

    s0b = [x.astype(BF16) for x in s0]
    from_state = [_dot_nt(ar[gi], s0b[gi]) for gi in R]
    rhs = [from_state[gi][:L] + compact_dot(a_ak[gi], v_e[gi]) for gi in R]
    u = [compact_dot(t_c[gi], expand(rhs[gi])) for gi in R]
    y = [from_state[gi][L:] + compact_dot(a_rb[gi], expand(u[gi]))
         + compact_dot(a_rk[gi], v_e[gi]) for gi in R]
    uv = [jnp.concatenate([u[gi], v[gi].astype(F32)], axis=0) for gi in R]
    bk = [jnp.concatenate([bh[gi], kh[gi]], axis=0) for gi in R]
    upd = [_dot(uv[gi].T, bk[gi]) for gi in R]
    for gi in R:
        y_ref[0, :, lanes[gi]] = y[gi]
        st_ref[gi] = s0[gi] * gam[gi] + jnp.where(seg_mask, upd[gi], 0.0)


def _rwkv_scan(ops, gamma, batch, seq_len, chunk=SCAN_CHUNK, heads=SCAN_HEADS,
               groups=SCAN_GROUPS):
    c = gamma.shape[-1]
    gn = heads * RWKV_HEAD
    lanes = groups * gn
    n_chunks = seq_len // chunk
    seq = pl.BlockSpec((1, chunk, lanes), lambda b, hg, t: (b, t, hg))
    gam = pl.BlockSpec((1, 1, lanes), lambda b, hg, t: (b * n_chunks + t, 0, hg))
    return pl.pallas_call(
        functools.partial(_scan_kernel, chunk=chunk, heads=heads, groups=groups),
        grid=(batch, c // lanes, n_chunks),
        in_specs=[seq] * 7 + [gam],
        out_specs=seq,
        out_shape=jax.ShapeDtypeStruct((batch, seq_len, c), F32),
        scratch_shapes=[pltpu.VMEM((groups, gn, gn), F32)],
        compiler_params=_params("parallel", "parallel", "arbitrary"),
        name="rwkv_scan",
    )(*[o.reshape(batch, seq_len, c) for o in ops], gamma.reshape(batch * n_chunks, 1, c))


def _gn_proj_ln_kernel(y_ref, bonus_ref, g_ref, gnw_ref, gnb_ref, ones_ref, w_ref, h_ref,
                       lnw_ref, lnb_ref, o_ref):
    ones = ones_ref[...]
    inv_n = 1.0 / RWKV_HEAD
    z = []
    for c0 in range(0, y_ref.shape[1], SEG_LANES):
        cb = slice(c0, c0 + SEG_LANES)
        y = y_ref[:, cb]
        mu = _seg_sum(y, ones) * inv_n
        d = y - mu
        var = _seg_sum(d * d, ones) * inv_n
        yn = d * lax.rsqrt(var + GN_EPS) * gnw_ref[:, cb] + gnb_ref[:, cb]
        z.append(((yn + bonus_ref[:, cb]) * g_ref[:, cb]).astype(BF16))
    proj = jnp.dot(jnp.concatenate(z, axis=1), w_ref[...], preferred_element_type=F32)
    o_ref[...] = _layernorm(ALPHA * h_ref[...] + proj, lnw_ref[...], lnb_ref[...])


def _gn_proj_ln(y, bonus, g, gn_w, gn_b, w, h, ln_w, ln_b, tm=TM // 2):
    m, c = y.shape
    tile = pl.BlockSpec((tm, c), lambda i: (i, 0))
    vec = pl.BlockSpec((1, c), lambda i: (0, 0))
    return pl.pallas_call(
        _gn_proj_ln_kernel,
        grid=(m // tm,),
        in_specs=[tile, tile, tile, vec, vec,
                  pl.BlockSpec((SEG_LANES, SEG_LANES), lambda i: (0, 0)),
                  pl.BlockSpec((c, c), lambda i: (0, 0)), tile, vec, vec],
        out_specs=tile,
        out_shape=jax.ShapeDtypeStruct((m, c), F32),
        compiler_params=_params("parallel"),
        name="rwkv_gn_proj_ln",
    )(y, bonus, g, gn_w, gn_b, _seg_ones(), w, h, ln_w, ln_b)


def _proj_ln_kernel(z_ref, w_ref, h_ref, lnw_ref, lnb_ref, o_ref):
    y = jnp.dot(z_ref[...], w_ref[...], preferred_element_type=F32)
    o_ref[...] = _layernorm(ALPHA * h_ref[...] + y, lnw_ref[...], lnb_ref[...])


def _proj_ln(z, w, h, ln_w, ln_b, tm=TM):
    m, kdim = z.shape
    c = w.shape[1]
    vec = pl.BlockSpec((1, c), lambda i: (0, 0))
    return pl.pallas_call(
        _proj_ln_kernel,
        grid=(m // tm,),
        in_specs=[pl.BlockSpec((tm, kdim), lambda i: (i, 0)),
                  pl.BlockSpec((kdim, c), lambda i: (0, 0)),
                  pl.BlockSpec((tm, c), lambda i: (i, 0)), vec, vec],
        out_specs=pl.BlockSpec((tm, c), lambda i: (i, 0)),
        out_shape=jax.ShapeDtypeStruct((m, c), F32),
        compiler_params=_params("parallel"),
        name="proj_residual_ln",
    )(z, w, h, ln_w, ln_b)


def _ffn_kernel(h_ref, wg_ref, wu_ref, wd_ref, lnw_ref, lnb_ref, o_ref, xb_ref, acc_ref):
    f = pl.program_id(1)

    @pl.when(f == 0)
    def _():
        xb_ref[...] = h_ref[...].astype(BF16)
        acc_ref[...] = jnp.zeros_like(acc_ref)

    x = xb_ref[...]
    gate = jnp.dot(x, wg_ref[...], preferred_element_type=F32)
    up = jnp.dot(x, wu_ref[...], preferred_element_type=F32)
    act = (gate * jax.nn.sigmoid(gate)) * up
    acc_ref[...] += jnp.dot(act.astype(BF16), wd_ref[...], preferred_element_type=F32)

    @pl.when(f == pl.num_programs(1) - 1)
    def _():
        o_ref[...] = _layernorm(ALPHA * h_ref[...] + acc_ref[...], lnw_ref[...], lnb_ref[...])


def _ffn_ln(h, w_gate, w_up, w_down, layer, ln_w, ln_b, tm=TM, tf=512):
    m, c = h.shape
    d_ff = w_gate.shape[2]
    vec = pl.BlockSpec((1, c), lambda i, f: (0, 0))
    return pl.pallas_call(
        _ffn_kernel,
        grid=(m // tm, d_ff // tf),
        in_specs=[pl.BlockSpec((tm, c), lambda i, f: (i, 0)),
                  pl.BlockSpec((None, c, tf), lambda i, f: (layer, 0, f)),
                  pl.BlockSpec((None, c, tf), lambda i, f: (layer, 0, f)),
                  pl.BlockSpec((None, tf, c), lambda i, f: (layer, f, 0)), vec, vec],
        out_specs=pl.BlockSpec((tm, c), lambda i, f: (i, 0)),
        out_shape=jax.ShapeDtypeStruct((m, c), F32),
        scratch_shapes=[pltpu.VMEM((tm, c), BF16), pltpu.VMEM((tm, c), F32)],
        compiler_params=_params("parallel", "arbitrary"),
        name="swiglu_ffn_ln",
    )(h, w_gate, w_up, w_down, ln_w, ln_b)


def _kv_kernel(x_ref, wk_ref, wvt_ref, k_ref, mean_ref, vt_ref, xb_ref):
    @pl.when(pl.program_id(1) == 0)
    def _():
        xb_ref[...] = x_ref[...].astype(BF16)

    xb = xb_ref[...]
    y = jnp.dot(xb, wk_ref[...], preferred_element_type=F32)
    k_ref[...] = y.astype(k_ref.dtype)
    tm, tn = y.shape
    nblk = tm // MOBA_BLOCK
    mean_ref[0] = jnp.sum(y.reshape(nblk, MOBA_BLOCK, tn), axis=1) * (1.0 / MOBA_BLOCK)
    vt = lax.dot_general(wvt_ref[...], xb, (((1,), (1,)), ((), ())), preferred_element_type=F32)
    for blk in range(nblk):
        vt_ref[blk] = vt[:, blk * MOBA_BLOCK:(blk + 1) * MOBA_BLOCK].astype(vt_ref.dtype)


def _kv_proj(h, w_k, w_v_t, tm=2 * TM, tn=512):
    m, c = h.shape
    n = w_k.shape[1]
    nblk = tm // MOBA_BLOCK
    return pl.pallas_call(
        _kv_kernel,
        grid=(m // tm, n // tn),
        in_specs=[pl.BlockSpec((tm, c), lambda i, j: (i, 0)),
                  pl.BlockSpec((c, tn), lambda i, j: (0, j)),
                  pl.BlockSpec((tn, c), lambda i, j: (j, 0))],
        out_specs=[pl.BlockSpec((tm, tn), lambda i, j: (i, j)),
                   pl.BlockSpec((1, nblk, tn), lambda i, j: (i, 0, j)),
                   pl.BlockSpec((nblk, tn, MOBA_BLOCK), lambda i, j: (i, j, 0))],
        out_shape=[jax.ShapeDtypeStruct((m, n), BF16),
                   jax.ShapeDtypeStruct((m // tm, nblk, n), F32),
                   jax.ShapeDtypeStruct((m // MOBA_BLOCK, n, MOBA_BLOCK), BF16)],
        scratch_shapes=[pltpu.VMEM((tm, c), BF16)],
        compiler_params=_params("parallel", "arbitrary"),
        name="shared_kv_proj",
    )(h, w_k, w_v_t)


def _q_kernel(x_ref, w_ref, o_ref, xb_ref):
    @pl.when(pl.program_id(1) == 0)
    def _():
        xb_ref[...] = x_ref[...].astype(BF16)

    o_ref[...] = jnp.dot(xb_ref[...], w_ref[...], preferred_element_type=F32)


def _q_proj(h, w_q, tm=2 * TM, tn=1024):
    m, c = h.shape
    n = w_q.shape[1]
    return pl.pallas_call(
        _q_kernel,
        grid=(m // tm, n // tn),
        in_specs=[pl.BlockSpec((tm, c), lambda i, j: (i, 0)),
                  pl.BlockSpec((c, tn), lambda i, j: (0, j))],
        out_specs=pl.BlockSpec((tm, tn), lambda i, j: (i, j)),
        out_shape=jax.ShapeDtypeStruct((m, n), F32),
        scratch_shapes=[pltpu.VMEM((tm, c), BF16)],
        compiler_params=_params("parallel", "arbitrary"),
        name="moba_q_proj",
    )(h, w_q)


def _moba_kernel(q_ref, k_ref, vt_ref, km_ref, o_ref, sel_ref, *, n_blocks, top_k, heads):
    blk = pl.program_id(2)
    BLK, DH = MOBA_BLOCK, MOBA_HEAD_DIM
    start = pl.multiple_of(blk * BLK, BLK)
    ki = lax.broadcasted_iota(jnp.int32, (BLK, BLK), 0)
    qi = lax.broadcasted_iota(jnp.int32, (BLK, BLK), 1)
    bi = lax.broadcasted_iota(jnp.int32, (n_blocks, BLK), 0)
    cols = [slice(hh * DH, (hh + 1) * DH) for hh in range(heads)]

    R = range(heads)
    q = [q_ref[:, cols[h]] for h in R]
    qb = [(q[h] * DH ** -0.5).astype(BF16) for h in R]

    gate = [lax.dot_general(km_ref[0, :, cols[h]], q[h], (((1,), (1,)), ((), ())),
                            preferred_element_type=F32, precision=HIGHEST) for h in R]
    rank = [jnp.zeros((n_blocks, BLK), jnp.int32) for h in R]
    for mm in range(n_blocks):
        past = (mm < blk).astype(jnp.int32)
        for h in R:
            gm = gate[h][mm:mm + 1, :]
            ahead = (gm > gate[h]) | ((gm == gate[h]) & (mm < bi))
            rank[h] = rank[h] + jnp.where(ahead, 1, 0) * past
    for h in R:
        sel_ref[h] = jnp.where((bi < blk) & (rank[h] < top_k), 1.0, 0.0)

    s = [_dot_nt(k_ref[pl.ds(start, BLK), cols[h]], qb[h]) for h in R]
    s = [jnp.where(ki <= qi, s[h], NEG) for h in R]
    m0 = [jnp.max(s[h], axis=0, keepdims=True) for h in R]
    p = [jnp.exp(s[h] - m0[h]) for h in R]
    l0 = [jnp.sum(p[h], axis=0, keepdims=True) for h in R]
    acc0 = [jnp.dot(vt_ref[blk, cols[h], :], p[h].astype(BF16), preferred_element_type=F32)
            for h in R]

    def body(n, carry):
        m_i, l_i, acc = carry
        off = pl.multiple_of(n * BLK, BLK)
        sn = [_dot_nt(k_ref[pl.ds(off, BLK), cols[h]], qb[h]) for h in R]
        sn = [jnp.where(sel_ref[h, pl.ds(n, 1), :] > 0.0, sn[h], NEG) for h in R]
        m_new = [jnp.maximum(m_i[h], jnp.max(sn[h], axis=0, keepdims=True)) for h in R]
        corr = [jnp.exp(m_i[h] - m_new[h]) for h in R]
        pn = [jnp.exp(sn[h] - m_new[h]) for h in R]
        l_new = [corr[h] * l_i[h] + jnp.sum(pn[h], axis=0, keepdims=True) for h in R]
        pv = [jnp.dot(vt_ref[n, cols[h], :], pn[h].astype(BF16), preferred_element_type=F32)
              for h in R]
        acc_new = [corr[h] * acc[h] + pv[h] for h in R]
        return m_new, l_new, acc_new

    _, l_f, acc_f = lax.fori_loop(0, blk, body, (m0, l0, acc0))
    for h in R:
        o_ref[:, cols[h]] = (acc_f[h] / l_f[h]).T.astype(o_ref.dtype)


def _moba_attention(q, k, v_t, k_mean, batch, seq_len, heads=MOBA_HEADS_PER_STEP):
    m, hd = q.shape
    n_blocks = seq_len // MOBA_BLOCK
    top_k = max(1, min(MOBA_TOPK, n_blocks - 1))
    BLK, W = MOBA_BLOCK, heads * MOBA_HEAD_DIM
    return pl.pallas_call(
        functools.partial(_moba_kernel, n_blocks=n_blocks, top_k=top_k, heads=heads),
        grid=(batch, hd // W, n_blocks),
        in_specs=[pl.BlockSpec((BLK, W), lambda b, h, t: (b * n_blocks + t, h)),
                  pl.BlockSpec((seq_len, W), lambda b, h, t: (b, h)),
                  pl.BlockSpec((n_blocks, W, BLK), lambda b, h, t: (b, h, 0)),
                  pl.BlockSpec((1, n_blocks, W), lambda b, h, t: (b, 0, h))],
        out_specs=pl.BlockSpec((BLK, W), lambda b, h, t: (b * n_blocks + t, h)),
        out_shape=jax.ShapeDtypeStruct((m, hd), BF16),
        scratch_shapes=[pltpu.VMEM((heads, n_blocks, BLK), F32)],
        compiler_params=_params("parallel", "parallel", "arbitrary"),
        name="moba_attention",
    )(q, k, v_t, k_mean)


def _pad_lora(w_in, w_out):
    rank = w_in.shape[1]
    pad = -rank % LORA_PAD
    return (jnp.pad(w_in, ((0, 0), (0, pad))).astype(BF16),
            jnp.pad(w_out, ((0, pad), (0, 0))).astype(BF16))


def kernel(x, a_mix, a_w_r, a_w_k, a_w_v, a_w_o, a_w0, a_w1, a_w2, a_a0, a_a1, a_a2, a_g1, a_g2,
           a_k_k, a_k_a, a_r_k, a_lnx_w, a_lnx_b, kv_w_k, kv_w_v, b_w_q, b_w_o, ffn_w_gate,
           ffn_w_up, ffn_w_down, ln1_w, ln1_b, ln2_w, ln2_b):
    batch, seq_len, c = x.shape
    n_a = a_mix.shape[0]
    n_layers = ffn_w_gate.shape[0]
    assert seq_len % MOBA_BLOCK == 0 and seq_len % TM == 0 and c == D_MODEL
    h = x.reshape(batch * seq_len, c)
    row = lambda vec: vec.reshape(1, c)
    k_all = v_t = k_mean = None
    ffn_gate_b, ffn_up_b, ffn_down_b = (w.astype(BF16) for w in (ffn_w_gate, ffn_w_up, ffn_w_down))

    for layer in range(n_layers):
        if layer < n_a:
            i = layer
            w1, w2 = _pad_lora(a_w1[i], a_w2[i])
            a1, a2 = _pad_lora(a_a1[i], a_a2[i])
            g1, g2 = _pad_lora(a_g1[i], a_g2[i])
            *ops, bonus, g, gamma = _rwkv_front(
                h, a_mix[i], a_w_r[i].astype(BF16), a_w_k[i].astype(BF16), a_w_v[i].astype(BF16),
                w1, w2, row(a_w0[i]), a1, a2, row(a_a0[i]), g1, g2, row(a_k_k[i]),
                row(a_k_a[i]), row(a_r_k[i]), seq_len, SCAN_CHUNK)
            y = _rwkv_scan(ops, gamma, batch, seq_len).reshape(batch * seq_len, c)
            h = _gn_proj_ln(y, bonus, g, row(a_lnx_w[i]), row(a_lnx_b[i]),
                            a_w_o[i].astype(BF16), h, row(ln1_w[layer]), row(ln1_b[layer]))
        else:
            j = layer - n_a
            q = _q_proj(h, b_w_q[j].astype(BF16))
            z = _moba_attention(q, k_all, v_t, k_mean, batch, seq_len)
            h = _proj_ln(z, b_w_o[j].astype(BF16), h, row(ln1_w[layer]), row(ln1_b[layer]))
        h = _ffn_ln(h, ffn_gate_b, ffn_up_b, ffn_down_b, layer, row(ln2_w[layer]),
                    row(ln2_b[layer]))
        if layer == n_a - 1:
            k_all, k_mean, v_t = _kv_proj(h, kv_w_k.astype(BF16), kv_w_v.T.astype(BF16))
            k_mean = k_mean.reshape(batch, seq_len // MOBA_BLOCK, -1)
    return h.reshape(batch, seq_len, c)
```

```python
import functools

import jax
import jax.numpy as jnp
from jax import lax
from jax.experimental import pallas as pl
from jax.experimental.pallas import tpu as pltpu

D_MODEL = 2048
RWKV_HEAD = 64
MOBA_HEAD_DIM = 128
MOBA_HEADS = D_MODEL // MOBA_HEAD_DIM
MOBA_BLOCK = 256
MOBA_TOPK = 3
NEG = -1e30
GN_EPS = 64e-5
LN_EPS = 1e-5
DEPTH = 2
ALPHA = (2 * DEPTH) ** 0.25
LORA_PAD = 128
NEG_EXP_MINUS_HALF = -0.6065306597126334

VMEM_LIMIT_BYTES = 56 * 1024 * 1024

TM = 512
SCAN_CHUNK = 64
SCAN_HEADS = 4
SCAN_GROUPS = 8
MOBA_HEADS_PER_STEP = 8
ONES_ROWS = 16
EW_ROWS, EW_LANES = 512, 512
SEG_LANES = 256

F32 = jnp.float32
BF16 = jnp.bfloat16
HIGHEST = lax.Precision.HIGHEST


def _params(*semantics):
    return pltpu.CompilerParams(dimension_semantics=semantics, vmem_limit_bytes=VMEM_LIMIT_BYTES)


def _dot(a, b):
    return jnp.dot(a.astype(BF16), b.astype(BF16), preferred_element_type=F32)


def _dot_nt(a, b):
    return lax.dot_general(a.astype(BF16), b.astype(BF16), (((1,), (1,)), ((), ())),
                           preferred_element_type=F32)


def _dot_nt_split(a, b):
    a_hi = a.astype(BF16).astype(F32)
    b_hi = b.astype(BF16)
    b_lo = (b - b_hi.astype(F32)).astype(BF16)
    rows = a.shape[0]
    both = _dot_nt(jnp.concatenate([a_hi, a - a_hi], axis=0), b_hi)
    return both[:rows] + both[rows:] + _dot_nt(a_hi, b_lo)


def _seg_sum(x, seg_ones, split=True):
    width = seg_ones.shape[0]
    parts = []
    for c0 in range(0, x.shape[1], width):
        xs = x[:, c0:c0 + width]
        hi = xs.astype(BF16)
        acc = jnp.dot(hi, seg_ones, preferred_element_type=F32)
        if split:
            lo = (xs - hi.astype(F32)).astype(BF16)
            acc = acc + jnp.dot(lo, seg_ones, preferred_element_type=F32)
        parts.append(acc)
    return parts[0] if len(parts) == 1 else jnp.concatenate(parts, axis=1)


def _seg_ones():
    head = jnp.arange(SEG_LANES) // RWKV_HEAD
    return (head[:, None] == head[None, :]).astype(BF16)


def _layernorm(t, w, b):
    mu = jnp.mean(t, axis=-1, keepdims=True)
    d = t - mu
    var = jnp.mean(d * d, axis=-1, keepdims=True)
    return d * lax.rsqrt(var + LN_EPS) * w + b


def _token_shift_delta(x_ref, prev_ref, seq_start):
    x = x_ref[...]
    rolled = pltpu.roll(x, 1, 0)
    prev_last = jnp.where(seq_start, 0.0, prev_ref[7:8, :])
    row = lax.broadcasted_iota(jnp.int32, x.shape, 0)
    x_prev = jnp.where(row == 0, prev_last, rolled)
    return x, x_prev - x


def _front_kernel(x_ref, prev_ref, mix_ref, wr_ref, wk_ref, wv_ref, w1_ref, w2_ref, w0_ref,
                  a1_ref, a2_ref, a0_ref, g1_ref, g2_ref, kk_ref, ka_ref, rk_ref, ones_ref,
                  at_ref, rt_ref, bt_ref, kt_ref, bh_ref, kh_ref, vb_ref, bonus_ref, g_ref,
                  gam_ref, xr_ref, xk_ref, xv_ref, hw_ref, ha_ref, hg_ref,
                  *, tiles_per_seq, chunk):
    @pl.when(pl.program_id(1) == 0)
    def _():
        seq_start = (pl.program_id(0) % tiles_per_seq) == 0
        x, xx = _token_shift_delta(x_ref, prev_ref, seq_start)
        mixed = lambda idx: x + xx * mix_ref[idx:idx + 1, :]
        xr_ref[...] = mixed(0).astype(BF16)
        xk_ref[...] = mixed(2).astype(BF16)
        xv_ref[...] = mixed(3).astype(BF16)
        hw_ref[...] = jnp.tanh(_dot(mixed(1), w1_ref[...])).astype(BF16)
        ha_ref[...] = _dot(mixed(4), a1_ref[...]).astype(BF16)
        hg_ref[...] = jax.nn.sigmoid(_dot(mixed(5), g1_ref[...])).astype(BF16)

    dot = functools.partial(jnp.dot, preferred_element_type=F32)
    rows, lanes = at_ref.shape
    n_chunks = rows // chunk
    blocks = [slice(c0, c0 + SEG_LANES) for c0 in range(0, lanes, SEG_LANES)]
    xr, xk, xv, hw, ha, hg = (ref[...] for ref in (xr_ref, xk_ref, xv_ref, hw_ref, ha_ref, hg_ref))
    r = [dot(xr, wr_ref[:, cb]) for cb in blocks]
    k = [dot(xk, wk_ref[:, cb]) for cb in blocks]
    v = [dot(xv, wv_ref[:, cb]) for cb in blocks]
    w_pre = [w0_ref[:, cb] + dot(hw, w2_ref[:, cb]) for cb in blocks]
    a_pre = [a0_ref[:, cb] + dot(ha, a2_ref[:, cb]) for cb in blocks]
    for cb in blocks:
        g_ref[:, cb] = dot(hg, g2_ref[:, cb])

    ones = ones_ref[...]
    pos = lax.broadcasted_iota(jnp.int32, (rows, SEG_LANES), 0) % chunk
    for idx, cb in enumerate(blocks):
        ld = jax.nn.sigmoid(w_pre[idx]) * NEG_EXP_MINUS_HALF
        a = jax.nn.sigmoid(a_pre[idx])
        kk = k[idx] * kk_ref[:, cb]
        kk = kk * lax.rsqrt(jnp.maximum(_seg_sum(kk * kk, ones), 1e-24))
        k_mod = k[idx] * (1.0 + (a - 1.0) * ka_ref[:, cb])
        b_vec = kk * a

        cs = ld
        step = 1
        while step < chunk:
            cs = cs + jnp.where(pos >= step, pltpu.roll(cs, step, 0), 0.0)
            step *= 2
        cs3 = cs.reshape(n_chunks, chunk, SEG_LANES)
        cs_last = cs3[:, chunk - 1:chunk, :]
        e_tail = jnp.exp(cs_last - cs3).reshape(rows, SEG_LANES)
        e_neg = jnp.exp(-cs)

        at_ref[:, cb] = (-kk * jnp.exp(cs - ld)).astype(BF16)
        rt_ref[:, cb] = (r[idx] * jnp.exp(cs)).astype(BF16)
        bt_ref[:, cb] = (b_vec * e_neg).astype(BF16)
        kt_ref[:, cb] = (k_mod * e_neg).astype(BF16)
        bh_ref[:, cb] = (b_vec * e_tail).astype(BF16)
        kh_ref[:, cb] = (k_mod * e_tail).astype(BF16)
        vb_ref[:, cb] = v[idx].astype(BF16)
        bonus_ref[:, cb] = _seg_sum(r[idx] * k_mod * rk_ref[:, cb], ones, split=False) * v[idx]
        gam_ref[:, cb] = jnp.exp(cs_last).reshape(n_chunks, SEG_LANES)


def _rwkv_front(h, mix, w_r, w_k, w_v, w1, w2, w0, a1, a2, a0, g1, g2, k_k, k_a, r_k,
                seq_len, chunk, rows=EW_ROWS, lanes=EW_LANES):
    m, c = h.shape
    full = lambda arr: pl.BlockSpec(arr.shape, lambda i, j: (0,) * arr.ndim)
    cols = lambda arr: pl.BlockSpec((arr.shape[0], lanes), lambda i, j: (0, j))
    tile = pl.BlockSpec((rows, lanes), lambda i, j: (i, j))
    ones = _seg_ones()
    bf = jax.ShapeDtypeStruct((m, c), BF16)
    f32 = jax.ShapeDtypeStruct((m, c), F32)
    return pl.pallas_call(
        functools.partial(_front_kernel, tiles_per_seq=seq_len // rows, chunk=chunk),
        grid=(m // rows, c // lanes),
        in_specs=[pl.BlockSpec((rows, c), lambda i, j: (i, 0)),
                  pl.BlockSpec((8, c), lambda i, j: (jnp.maximum(i * (rows // 8) - 1, 0), 0)),
                  full(mix), cols(w_r), cols(w_k), cols(w_v),
                  full(w1), cols(w2), cols(w0), full(a1), cols(a2), cols(a0),
                  full(g1), cols(g2), cols(k_k), cols(k_a), cols(r_k), full(ones)],
        out_specs=[tile] * 9 + [pl.BlockSpec((rows // chunk, lanes), lambda i, j: (i, j))],
        out_shape=[bf] * 7 + [f32, f32, jax.ShapeDtypeStruct((m // chunk, c), F32)],
        scratch_shapes=[pltpu.VMEM((rows, c), BF16)] * 3
        + [pltpu.VMEM((rows, w1.shape[1]), BF16), pltpu.VMEM((rows, a1.shape[1]), BF16),
           pltpu.VMEM((rows, g1.shape[1]), BF16)],
        compiler_params=_params("parallel", "arbitrary"),
        name="rwkv_front",
    )(h, h, mix, w_r, w_k, w_v, w1, w2, w0, a1, a2, a0, g1, g2, k_k, k_a, r_k, ones)


def _scan_kernel(at_ref, rt_ref, bt_ref, kt_ref, bh_ref, kh_ref, v_ref, gam_ref, y_ref, st_ref,
                 *, chunk, heads, groups):
    L, G, N = chunk, heads, RWKV_HEAD
    GN, GL = G * N, G * L
    assert L == N

    @pl.when(pl.program_id(2) == 0)
    def _():
        st_ref[...] = jnp.zeros_like(st_ref)

    seg_mask = (lax.broadcasted_iota(jnp.int32, (GN, GN), 0) // N
                == lax.broadcasted_iota(jnp.int32, (GN, GN), 1) // N)
    expand_mask = (lax.broadcasted_iota(jnp.int32, (GL, GN), 0) // L
                   == lax.broadcasted_iota(jnp.int32, (GL, GN), 1) // N)
    row_t = lax.broadcasted_iota(jnp.int32, (L, GN), 0)
    lane_s = lax.broadcasted_iota(jnp.int32, (L, GN), 1) % L
    strict_c = lane_s < row_t
    incl_c = lane_s <= row_t
    eye_c = (lane_s == row_t).astype(F32)

    def expand(x):
        xb = x.astype(BF16)
        return jnp.where(expand_mask, jnp.concatenate([xb] * G, axis=0), jnp.zeros((), BF16))

    def compact_dot(x, w_bd):
        return jnp.dot(x.astype(BF16), w_bd, preferred_element_type=F32)

    R = range(groups)
    lanes = [slice(gi * GN, (gi + 1) * GN) for gi in R]
    load = lambda ref: [ref[0, :, lanes[gi]] for gi in R]
    at, rt, bt, kt, bh, kh, v = (load(ref) for ref in
                                 (at_ref, rt_ref, bt_ref, kt_ref, bh_ref, kh_ref, v_ref))
    s0 = [st_ref[gi] for gi in R]
    gam = [gam_ref[0, :, lanes[gi]] for gi in R]

    ar = [jnp.concatenate([at[gi], rt[gi]], axis=0) for gi in R]
    bt_e = [expand(x) for x in bt]
    kt_e = [expand(x) for x in kt]
    v_e = [expand(x) for x in v]
    gram_b = [_dot_nt(ar[gi], bt_e[gi]) for gi in R]
    gram_k = [_dot_nt(ar[gi], kt_e[gi]) for gi in R]
    a_ab = [jnp.where(strict_c, gram_b[gi][:L], 0.0) for gi in R]
    a_rb = [jnp.where(incl_c, gram_b[gi][L:], 0.0) for gi in R]
    a_ak = [jnp.where(strict_c, gram_k[gi][:L], 0.0) for gi in R]
    a_rk = [jnp.where(incl_c, gram_k[gi][L:], 0.0) for gi in R]

    t_c = [eye_c + a_ab[gi] for gi in R]
    p_bd = [expand(a_ab[gi]) for gi in R]
    p_c = [compact_dot(a_ab[gi], p_bd[gi]) for gi in R]
    p_bd = [expand(p_c[gi]) for gi in R]
    for _ in range(L.bit_length() - 3):
        tp = [compact_dot(jnp.concatenate([t_c[gi], p_c[gi]], axis=0), p_bd[gi]) for gi in R]
        t_c = [t_c[gi] + tp[gi][:L] for gi in R]
        p_c = [tp[gi][L:] for gi in R]
        p_bd = [expand(p_c[gi]) for gi in R]
    t_c = [t_c[gi] + compact_dot(t_c[gi], p_bd[gi]) for gi in R]

    s0b = [x.astype(BF16) for x in s0]
    from_state = [_dot_nt(ar[gi], s0b[gi]) for gi in R]
    rhs = [from_state[gi][:L] + compact_dot(a_ak[gi], v_e[gi]) for gi in R]
    u = [compact_dot(t_c[gi], expand(rhs[gi])) for gi in R]
    y = [from_state[gi][L:] + compact_dot(a_rb[gi], expand(u[gi]))
         + compact_dot(a_rk[gi], v_e[gi]) for gi in R]
    uv = [jnp.concatenate([u[gi], v[gi].astype(F32)], axis=0) for gi in R]
    bk = [jnp.concatenate([bh[gi], kh[gi]], axis=0) for gi in R]
    upd = [_dot(uv[gi].T, bk[gi]) for gi in R]
    for gi in R:
        y_ref[0, :, lanes[gi]] = y[gi]
        st_ref[gi] = s0[gi] * gam[gi] + jnp.where(seg_mask, upd[gi], 0.0)


def _rwkv_scan(ops, gamma, batch, seq_len, chunk=SCAN_CHUNK, heads=SCAN_HEADS,
               groups=SCAN_GROUPS):
    c = gamma.shape[-1]
    gn = heads * RWKV_HEAD
    lanes = groups * gn
    n_chunks = seq_len // chunk
    seq = pl.BlockSpec((1, chunk, lanes), lambda b, hg, t: (b, t, hg))
    gam = pl.BlockSpec((1, 1, lanes), lambda b, hg, t: (b * n_chunks + t, 0, hg))
    return pl.pallas_call(
        functools.partial(_scan_kernel, chunk=chunk, heads=heads, groups=groups),
        grid=(batch, c // lanes, n_chunks),
        in_specs=[seq] * 7 + [gam],
        out_specs=seq,
        out_shape=jax.ShapeDtypeStruct((batch, seq_len, c), F32),
        scratch_shapes=[pltpu.VMEM((groups, gn, gn), F32)],
        compiler_params=_params("parallel", "parallel", "arbitrary"),
        name="rwkv_scan",
    )(*[o.reshape(batch, seq_len, c) for o in ops], gamma.reshape(batch * n_chunks, 1, c))


def _gn_proj_ln_kernel(y_ref, bonus_ref, g_ref, gnw_ref, gnb_ref, ones_ref, w_ref, h_ref,
                       lnw_ref, lnb_ref, o_ref):
    ones = ones_ref[...]
    inv_n = 1.0 / RWKV_HEAD
    z = []
    for c0 in range(0, y_ref.shape[1], SEG_LANES):
        cb = slice(c0, c0 + SEG_LANES)
        y = y_ref[:, cb]
        mu = _seg_sum(y, ones) * inv_n
        d = y - mu
        var = _seg_sum(d * d, ones) * inv_n
        yn = d * lax.rsqrt(var + GN_EPS) * gnw_ref[:, cb] + gnb_ref[:, cb]
        z.append(((yn + bonus_ref[:, cb]) * g_ref[:, cb]).astype(BF16))
    proj = jnp.dot(jnp.concatenate(z, axis=1), w_ref[...], preferred_element_type=F32)
    o_ref[...] = _layernorm(ALPHA * h_ref[...] + proj, lnw_ref[...], lnb_ref[...])


def _gn_proj_ln(y, bonus, g, gn_w, gn_b, w, h, ln_w, ln_b, tm=TM // 2):
    m, c = y.shape
    tile = pl.BlockSpec((tm, c), lambda i: (i, 0))
    vec = pl.BlockSpec((1, c), lambda i: (0, 0))
    return pl.pallas_call(
        _gn_proj_ln_kernel,
        grid=(m // tm,),
        in_specs=[tile, tile, tile, vec, vec,
                  pl.BlockSpec((SEG_LANES, SEG_LANES), lambda i: (0, 0)),
                  pl.BlockSpec((c, c), lambda i: (0, 0)), tile, vec, vec],
        out_specs=tile,
        out_shape=jax.ShapeDtypeStruct((m, c), F32),
        compiler_params=_params("parallel"),
        name="rwkv_gn_proj_ln",
    )(y, bonus, g, gn_w, gn_b, _seg_ones(), w, h, ln_w, ln_b)


def _proj_ln_kernel(z_ref, w_ref, h_ref, lnw_ref, lnb_ref, o_ref):
    y = jnp.dot(z_ref[...], w_ref[...], preferred_element_type=F32)
    o_ref[...] = _layernorm(ALPHA * h_ref[...] + y, lnw_ref[...], lnb_ref[...])


def _proj_ln(z, w, h, ln_w, ln_b, tm=TM):
    m, kdim = z.shape
    c = w.shape[1]
    vec = pl.BlockSpec((1, c), lambda i: (0, 0))
    return pl.pallas_call(
        _proj_ln_kernel,
        grid=(m // tm,),
        in_specs=[pl.BlockSpec((tm, kdim), lambda i: (i, 0)),
                  pl.BlockSpec((kdim, c), lambda i: (0, 0)),
                  pl.BlockSpec((tm, c), lambda i: (i, 0)), vec, vec],
        out_specs=pl.BlockSpec((tm, c), lambda i: (i, 0)),
        out_shape=jax.ShapeDtypeStruct((m, c), F32),
        compiler_params=_params("parallel"),
        name="proj_residual_ln",
    )(z, w, h, ln_w, ln_b)


def _ffn_kernel(h_ref, wg_ref, wu_ref, wd_ref, lnw_ref, lnb_ref, o_ref, xb_ref, acc_ref):
    f = pl.program_id(1)

    @pl.when(f == 0)
    def _():
        xb_ref[...] = h_ref[...].astype(BF16)
        acc_ref[...] = jnp.zeros_like(acc_ref)

    x = xb_ref[...]
    gate = jnp.dot(x, wg_ref[...], preferred_element_type=F32)
    up = jnp.dot(x, wu_ref[...], preferred_element_type=F32)
    act = (gate * jax.nn.sigmoid(gate)) * up
    acc_ref[...] += jnp.dot(act.astype(BF16), wd_ref[...], preferred_element_type=F32)

    @pl.when(f == pl.num_programs(1) - 1)
    def _():
        o_ref[...] = _layernorm(ALPHA * h_ref[...] + acc_ref[...], lnw_ref[...], lnb_ref[...])


def _ffn_ln(h, w_gate, w_up, w_down, layer, ln_w, ln_b, tm=TM, tf=512):
    m, c = h.shape
    d_ff = w_gate.shape[2]
    vec = pl.BlockSpec((1, c), lambda i, f: (0, 0))
    return pl.pallas_call(
        _ffn_kernel,
        grid=(m // tm, d_ff // tf),
        in_specs=[pl.BlockSpec((tm, c), lambda i, f: (i, 0)),
                  pl.BlockSpec((None, c, tf), lambda i, f: (layer, 0, f)),
                  pl.BlockSpec((None, c, tf), lambda i, f: (layer, 0, f)),
                  pl.BlockSpec((None, tf, c), lambda i, f: (layer, f, 0)), vec, vec],
        out_specs=pl.BlockSpec((tm, c), lambda i, f: (i, 0)),
        out_shape=jax.ShapeDtypeStruct((m, c), F32),
        scratch_shapes=[pltpu.VMEM((tm, c), BF16), pltpu.VMEM((tm, c), F32)],
        compiler_params=_params("parallel", "arbitrary"),
        name="swiglu_ffn_ln",
    )(h, w_gate, w_up, w_down, ln_w, ln_b)


def _kv_kernel(x_ref, wk_ref, wvt_ref, k_ref, mean_ref, vt_ref, xb_ref):
    @pl.when(pl.program_id(1) == 0)
    def _():
        xb_ref[...] = x_ref[...].astype(BF16)

    xb = xb_ref[...]
    y = jnp.dot(xb, wk_ref[...], preferred_element_type=F32)
    k_ref[...] = y.astype(k_ref.dtype)
    tm, tn = y.shape
    nblk = tm // MOBA_BLOCK
    mean_ref[0] = jnp.sum(y.reshape(nblk, MOBA_BLOCK, tn), axis=1) * (1.0 / MOBA_BLOCK)
    vt = lax.dot_general(wvt_ref[...], xb, (((1,), (1,)), ((), ())), preferred_element_type=F32)
    for blk in range(nblk):
        vt_ref[blk] = vt[:, blk * MOBA_BLOCK:(blk + 1) * MOBA_BLOCK].astype(vt_ref.dtype)


def _kv_proj(h, w_k, w_v_t, tm=2 * TM, tn=512):
    m, c = h.shape
    n = w_k.shape[1]
    nblk = tm // MOBA_BLOCK
    return pl.pallas_call(
        _kv_kernel,
        grid=(m // tm, n // tn),
        in_specs=[pl.BlockSpec((tm, c), lambda i, j: (i, 0)),
                  pl.BlockSpec((c, tn), lambda i, j: (0, j)),
                  pl.BlockSpec((tn, c), lambda i, j: (j, 0))],
        out_specs=[pl.BlockSpec((tm, tn), lambda i, j: (i, j)),
                   pl.BlockSpec((1, nblk, tn), lambda i, j: (i, 0, j)),
                   pl.BlockSpec((nblk, tn, MOBA_BLOCK), lambda i, j: (i, j, 0))],
        out_shape=[jax.ShapeDtypeStruct((m, n), BF16),
                   jax.ShapeDtypeStruct((m // tm, nblk, n), F32),
                   jax.ShapeDtypeStruct((m // MOBA_BLOCK, n, MOBA_BLOCK), BF16)],
        scratch_shapes=[pltpu.VMEM((tm, c), BF16)],
        compiler_params=_params("parallel", "arbitrary"),
        name="shared_kv_proj",
    )(h, w_k, w_v_t)


def _q_kernel(x_ref, w_ref, o_ref, xb_ref):
    @pl.when(pl.program_id(1) == 0)
    def _():
        xb_ref[...] = x_ref[...].astype(BF16)

    o_ref[...] = jnp.dot(xb_ref[...], w_ref[...], preferred_element_type=F32)


def _q_proj(h, w_q, tm=2 * TM, tn=1024):
    m, c = h.shape
    n = w_q.shape[1]
    return pl.pallas_call(
        _q_kernel,
        grid=(m // tm, n // tn),
        in_specs=[pl.BlockSpec((tm, c), lambda i, j: (i, 0)),
                  pl.BlockSpec((c, tn), lambda i, j: (0, j))],
        out_specs=pl.BlockSpec((tm, tn), lambda i, j: (i, j)),
        out_shape=jax.ShapeDtypeStruct((m, n), F32),
        scratch_shapes=[pltpu.VMEM((tm, c), BF16)],
        compiler_params=_params("parallel", "arbitrary"),
        name="moba_q_proj",
    )(h, w_q)


def _moba_kernel(q_ref, k_ref, vt_ref, km_ref, o_ref, sel_ref, *, n_blocks, top_k, heads):
    blk = pl.program_id(2)
    BLK, DH = MOBA_BLOCK, MOBA_HEAD_DIM
    start = pl.multiple_of(blk * BLK, BLK)
    ki = lax.broadcasted_iota(jnp.int32, (BLK, BLK), 0)
    qi = lax.broadcasted_iota(jnp.int32, (BLK, BLK), 1)
    bi = lax.broadcasted_iota(jnp.int32, (n_blocks, BLK), 0)
    cols = [slice(hh * DH, (hh + 1) * DH) for hh in range(heads)]

    R = range(heads)
    q = [q_ref[:, cols[h]] for h in R]
    qb = [(q[h] * DH ** -0.5).astype(BF16) for h in R]

    gate = [_dot_nt_split(km_ref[0, :, cols[h]], q[h]) for h in R]
    rank = [jnp.zeros((n_blocks, BLK), jnp.int32) for h in R]
    for mm in range(n_blocks):
        past = (mm < blk).astype(jnp.int32)
        for h in R:
            gm = gate[h][mm:mm + 1, :]
            ahead = (gm > gate[h]) | ((gm == gate[h]) & (mm < bi))
            rank[h] = rank[h] + jnp.where(ahead, 1, 0) * past
    for h in R:
        sel_ref[h] = jnp.where((bi < blk) & (rank[h] < top_k), 1.0, 0.0)

    s = [_dot_nt(k_ref[pl.ds(start, BLK), cols[h]], qb[h]) for h in R]
    s = [jnp.where(ki <= qi, s[h], NEG) for h in R]
    ones_rows = jnp.ones((ONES_ROWS, BLK), BF16)
    v_ext = lambda n, h: jnp.concatenate([vt_ref[n, cols[h], :], ones_rows], axis=0)
    m0 = [jnp.max(s[h], axis=0, keepdims=True) for h in R]
    p = [jnp.exp((s[h] - m0[h]).astype(BF16)) for h in R]
    acc0 = [jnp.dot(v_ext(blk, h), p[h], preferred_element_type=F32) for h in R]

    def body(n, carry):
        m_i, acc = carry
        off = pl.multiple_of(n * BLK, BLK)
        sn = [_dot_nt(k_ref[pl.ds(off, BLK), cols[h]], qb[h]) for h in R]
        sn = [jnp.where(sel_ref[h, pl.ds(n, 1), :] > 0.0, sn[h], NEG) for h in R]
        m_new = [jnp.maximum(m_i[h], jnp.max(sn[h], axis=0, keepdims=True)) for h in R]
        corr = [jnp.exp(m_i[h] - m_new[h]) for h in R]
        pn = [jnp.exp((sn[h] - m_new[h]).astype(BF16)) for h in R]
        pv = [jnp.dot(v_ext(n, h), pn[h], preferred_element_type=F32) for h in R]
        acc_new = [corr[h] * acc[h] + pv[h] for h in R]
        return m_new, acc_new

    _, acc_f = lax.fori_loop(0, blk, body, (m0, acc0))
    for h in R:
        out = acc_f[h][:DH] / acc_f[h][DH:DH + 1]
        o_ref[:, cols[h]] = out.T.astype(o_ref.dtype)


def _moba_attention(q, k, v_t, k_mean, batch, seq_len, heads=MOBA_HEADS_PER_STEP):
    m, hd = q.shape
    n_blocks = seq_len // MOBA_BLOCK
    top_k = max(1, min(MOBA_TOPK, n_blocks - 1))
    BLK, W = MOBA_BLOCK, heads * MOBA_HEAD_DIM
    return pl.pallas_call(
        functools.partial(_moba_kernel, n_blocks=n_blocks, top_k=top_k, heads=heads),
        grid=(batch, hd // W, n_blocks),
        in_specs=[pl.BlockSpec((BLK, W), lambda b, h, t: (b * n_blocks + t, h)),
                  pl.BlockSpec((seq_len, W), lambda b, h, t: (b, h)),
                  pl.BlockSpec((n_blocks, W, BLK), lambda b, h, t: (b, h, 0)),
                  pl.BlockSpec((1, n_blocks, W), lambda b, h, t: (b, 0, h))],
        out_specs=pl.BlockSpec((BLK, W), lambda b, h, t: (b * n_blocks + t, h)),
        out_shape=jax.ShapeDtypeStruct((m, hd), BF16),
        scratch_shapes=[pltpu.VMEM((heads, n_blocks, BLK), F32)],
        compiler_params=_params("parallel", "parallel", "arbitrary"),
        name="moba_attention",
    )(q, k, v_t, k_mean)


def _pad_lora(w_in, w_out):
    rank = w_in.shape[1]
    pad = -rank % LORA_PAD
    return (jnp.pad(w_in, ((0, 0), (0, pad))).astype(BF16),
            jnp.pad(w_out, ((0, pad), (0, 0))).astype(BF16))


def kernel(x, a_mix, a_w_r, a_w_k, a_w_v, a_w_o, a_w0, a_w1, a_w2, a_a0, a_a1, a_a2, a_g1, a_g2,
           a_k_k, a_k_a, a_r_k, a_lnx_w, a_lnx_b, kv_w_k, kv_w_v, b_w_q, b_w_o, ffn_w_gate,
           ffn_w_up, ffn_w_down, ln1_w, ln1_b, ln2_w, ln2_b):
    batch, seq_len, c = x.shape
    n_a = a_mix.shape[0]
    n_layers = ffn_w_gate.shape[0]
    assert seq_len % MOBA_BLOCK == 0 and seq_len % TM == 0 and c == D_MODEL
    h = x.reshape(batch * seq_len, c)
    row = lambda vec: vec.reshape(1, c)
    k_all = v_t = k_mean = None
    ffn_gate_b, ffn_up_b, ffn_down_b = (w.astype(BF16) for w in (ffn_w_gate, ffn_w_up, ffn_w_down))

    for layer in range(n_layers):
        if layer < n_a:
            i = layer
            w1, w2 = _pad_lora(a_w1[i], a_w2[i])
            a1, a2 = _pad_lora(a_a1[i], a_a2[i])
            g1, g2 = _pad_lora(a_g1[i], a_g2[i])
            *ops, bonus, g, gamma = _rwkv_front(
                h, a_mix[i], a_w_r[i].astype(BF16), a_w_k[i].astype(BF16), a_w_v[i].astype(BF16),
                w1, w2, row(a_w0[i]), a1, a2, row(a_a0[i]), g1, g2, row(a_k_k[i]),
                row(a_k_a[i]), row(a_r_k[i]), seq_len, SCAN_CHUNK)
            y = _rwkv_scan(ops, gamma, batch, seq_len).reshape(batch * seq_len, c)
            h = _gn_proj_ln(y, bonus, g, row(a_lnx_w[i]), row(a_lnx_b[i]),
                            a_w_o[i].astype(BF16), h, row(ln1_w[layer]), row(ln1_b[layer]))
        else:
            j = layer - n_a
            q = _q_proj(h, b_w_q[j].astype(BF16))
            z = _moba_attention(q, k_all, v_t, k_mean, batch, seq_len)
            h = _proj_ln(z, b_w_o[j].astype(BF16), h, row(ln1_w[layer]), row(ln1_b[layer]))
        h = _ffn_ln(h, ffn_gate_b, ffn_up_b, ffn_down_b, layer, row(ln2_w[layer]),
                    row(ln2_b[layer]))
        if layer == n_a - 1:
            k_all, k_mean, v_t = _kv_proj(h, kv_w_k.astype(BF16), kv_w_v.T.astype(BF16))
            k_mean = k_mean.reshape(batch, seq_len // MOBA_BLOCK, -1)
    return h.reshape(batch, seq_len, c)
```

```python
import functools

import jax
import jax.numpy as jnp
from jax import lax
from jax.experimental import pallas as pl
from jax.experimental.pallas import tpu as pltpu

D_MODEL = 2048
RWKV_HEAD = 64
MOBA_HEAD_DIM = 128
MOBA_HEADS = D_MODEL // MOBA_HEAD_DIM
MOBA_BLOCK = 256
MOBA_TOPK = 3
NEG = -1e30
GN_EPS = 64e-5
LN_EPS = 1e-5
DEPTH = 2
ALPHA = (2 * DEPTH) ** 0.25
LORA_PAD = 128
NEG_EXP_MINUS_HALF = -0.6065306597126334

VMEM_LIMIT_BYTES = 56 * 1024 * 1024

TM = 512
SCAN_CHUNK = 64
SCAN_HEADS = 4
SCAN_GROUPS = 8
MOBA_HEADS_PER_STEP = 8
ONES_ROWS = 16
EW_ROWS, EW_LANES = 512, 512
SEG_LANES = 256

F32 = jnp.float32
BF16 = jnp.bfloat16
HIGHEST = lax.Precision.HIGHEST


def _params(*semantics):
    return pltpu.CompilerParams(dimension_semantics=semantics, vmem_limit_bytes=VMEM_LIMIT_BYTES)


def _dot(a, b):
    return jnp.dot(a.astype(BF16), b.astype(BF16), preferred_element_type=F32)


def _dot_nt(a, b):
    return lax.dot_general(a.astype(BF16), b.astype(BF16), (((1,), (1,)), ((), ())),
                           preferred_element_type=F32)


def _dot_nt_split(a, b):
    a_hi = a.astype(BF16).astype(F32)
    b_hi = b.astype(BF16)
    b_lo = (b - b_hi.astype(F32)).astype(BF16)
    rows = a.shape[0]
    both = _dot_nt(jnp.concatenate([a_hi, a - a_hi], axis=0), b_hi)
    return both[:rows] + both[rows:] + _dot_nt(a_hi, b_lo)


def _seg_sum(x, seg_ones, split=True):
    width = seg_ones.shape[0]
    parts = []
    for c0 in range(0, x.shape[1], width):
        xs = x[:, c0:c0 + width]
        hi = xs.astype(BF16)
        acc = jnp.dot(hi, seg_ones, preferred_element_type=F32)
        if split:
            lo = (xs - hi.astype(F32)).astype(BF16)
            acc = acc + jnp.dot(lo, seg_ones, preferred_element_type=F32)
        parts.append(acc)
    return parts[0] if len(parts) == 1 else jnp.concatenate(parts, axis=1)


def _seg_ones():
    head = jnp.arange(SEG_LANES) // RWKV_HEAD
    return (head[:, None] == head[None, :]).astype(BF16)


def _layernorm(t, w, b):
    mu = jnp.mean(t, axis=-1, keepdims=True)
    d = t - mu
    var = jnp.mean(d * d, axis=-1, keepdims=True)
    return d * lax.rsqrt(var + LN_EPS) * w + b


def _token_shift_delta(x_ref, prev_ref, seq_start):
    x = x_ref[...]
    rolled = pltpu.roll(x, 1, 0)
    prev_last = jnp.where(seq_start, 0.0, prev_ref[7:8, :])
    row = lax.broadcasted_iota(jnp.int32, x.shape, 0)
    x_prev = jnp.where(row == 0, prev_last, rolled)
    return x, x_prev - x


def _front_kernel(x_ref, prev_ref, mix_ref, wr_ref, wk_ref, wv_ref, w1_ref, w2_ref, w0_ref,
                  a1_ref, a2_ref, a0_ref, g1_ref, g2_ref, kk_ref, ka_ref, rk_ref, ones_ref,
                  at_ref, rt_ref, bt_ref, kt_ref, vb_ref, bonus_ref, g_ref,
                  gam_ref, xr_ref, xk_ref, xv_ref, hw_ref, ha_ref, hg_ref,
                  *, tiles_per_seq, chunk):
    @pl.when(pl.program_id(1) == 0)
    def _():
        seq_start = (pl.program_id(0) % tiles_per_seq) == 0
        x, xx = _token_shift_delta(x_ref, prev_ref, seq_start)
        mixed = lambda idx: x + xx * mix_ref[idx:idx + 1, :]
        xr_ref[...] = mixed(0).astype(BF16)
        xk_ref[...] = mixed(2).astype(BF16)
        xv_ref[...] = mixed(3).astype(BF16)
        hw_ref[...] = jnp.tanh(_dot(mixed(1), w1_ref[...])).astype(BF16)
        ha_ref[...] = _dot(mixed(4), a1_ref[...]).astype(BF16)
        hg_ref[...] = jax.nn.sigmoid(_dot(mixed(5), g1_ref[...])).astype(BF16)

    dot = functools.partial(jnp.dot, preferred_element_type=F32)
    rows, lanes = at_ref.shape
    n_chunks = rows // chunk
    blocks = [slice(c0, c0 + SEG_LANES) for c0 in range(0, lanes, SEG_LANES)]
    xr, xk, xv, hw, ha, hg = (ref[...] for ref in (xr_ref, xk_ref, xv_ref, hw_ref, ha_ref, hg_ref))
    r = [dot(xr, wr_ref[:, cb]) for cb in blocks]
    k = [dot(xk, wk_ref[:, cb]) for cb in blocks]
    v = [dot(xv, wv_ref[:, cb]) for cb in blocks]
    w_pre = [w0_ref[:, cb] + dot(hw, w2_ref[:, cb]) for cb in blocks]
    a_pre = [a0_ref[:, cb] + dot(ha, a2_ref[:, cb]) for cb in blocks]
    for cb in blocks:
        g_ref[:, cb] = dot(hg, g2_ref[:, cb])

    ones = ones_ref[...]
    pos = lax.broadcasted_iota(jnp.int32, (rows, SEG_LANES), 0) % chunk
    for idx, cb in enumerate(blocks):
        ld = jax.nn.sigmoid(w_pre[idx]) * NEG_EXP_MINUS_HALF
        a = jax.nn.sigmoid(a_pre[idx])
        kk = k[idx] * kk_ref[:, cb]
        kk = kk * lax.rsqrt(jnp.maximum(_seg_sum(kk * kk, ones), 1e-24))
        k_mod = k[idx] * (1.0 + (a - 1.0) * ka_ref[:, cb])
        b_vec = kk * a

        cs = ld
        step = 1
        while step < chunk:
            cs = cs + jnp.where(pos >= step, pltpu.roll(cs, step, 0), 0.0)
            step *= 2
        cs3 = cs.reshape(n_chunks, chunk, SEG_LANES)
        cs_last = cs3[:, chunk - 1:chunk, :]
        e_neg = jnp.exp(-cs)

        at_ref[:, cb] = (-kk * jnp.exp(cs - ld)).astype(BF16)
        rt_ref[:, cb] = (r[idx] * jnp.exp(cs)).astype(BF16)
        bt_ref[:, cb] = (b_vec * e_neg).astype(BF16)
        kt_ref[:, cb] = (k_mod * e_neg).astype(BF16)
        vb_ref[:, cb] = v[idx].astype(BF16)
        bonus_ref[:, cb] = _seg_sum(r[idx] * k_mod * rk_ref[:, cb], ones, split=False) * v[idx]
        gam_ref[:, cb] = jnp.exp(cs_last).reshape(n_chunks, SEG_LANES)


def _rwkv_front(h, mix, w_r, w_k, w_v, w1, w2, w0, a1, a2, a0, g1, g2, k_k, k_a, r_k,
                seq_len, chunk, rows=EW_ROWS, lanes=EW_LANES):
    m, c = h.shape
    full = lambda arr: pl.BlockSpec(arr.shape, lambda i, j: (0,) * arr.ndim)
    cols = lambda arr: pl.BlockSpec((arr.shape[0], lanes), lambda i, j: (0, j))
    tile = pl.BlockSpec((rows, lanes), lambda i, j: (i, j))
    ones = _seg_ones()
    bf = jax.ShapeDtypeStruct((m, c), BF16)
    f32 = jax.ShapeDtypeStruct((m, c), F32)
    return pl.pallas_call(
        functools.partial(_front_kernel, tiles_per_seq=seq_len // rows, chunk=chunk),
        grid=(m // rows, c // lanes),
        in_specs=[pl.BlockSpec((rows, c), lambda i, j: (i, 0)),
                  pl.BlockSpec((8, c), lambda i, j: (jnp.maximum(i * (rows // 8) - 1, 0), 0)),
                  full(mix), cols(w_r), cols(w_k), cols(w_v),
                  full(w1), cols(w2), cols(w0), full(a1), cols(a2), cols(a0),
                  full(g1), cols(g2), cols(k_k), cols(k_a), cols(r_k), full(ones)],
        out_specs=[tile] * 7 + [pl.BlockSpec((rows // chunk, lanes), lambda i, j: (i, j))],
        out_shape=[bf] * 5 + [f32, f32, jax.ShapeDtypeStruct((m // chunk, c), F32)],
        scratch_shapes=[pltpu.VMEM((rows, c), BF16)] * 3
        + [pltpu.VMEM((rows, w1.shape[1]), BF16), pltpu.VMEM((rows, a1.shape[1]), BF16),
           pltpu.VMEM((rows, g1.shape[1]), BF16)],
        compiler_params=_params("parallel", "arbitrary"),
        name="rwkv_front",
    )(h, h, mix, w_r, w_k, w_v, w1, w2, w0, a1, a2, a0, g1, g2, k_k, k_a, r_k, ones)


def _scan_kernel(at_ref, rt_ref, bt_ref, kt_ref, v_ref, gam_ref, y_ref, st_ref,
                 *, chunk, heads, groups):
    L, G, N = chunk, heads, RWKV_HEAD
    GN, GL = G * N, G * L
    assert L == N

    @pl.when(pl.program_id(2) == 0)
    def _():
        st_ref[...] = jnp.zeros_like(st_ref)

    seg_mask = (lax.broadcasted_iota(jnp.int32, (GN, GN), 0) // N
                == lax.broadcasted_iota(jnp.int32, (GN, GN), 1) // N)
    expand_mask = (lax.broadcasted_iota(jnp.int32, (GL, GN), 0) // L
                   == lax.broadcasted_iota(jnp.int32, (GL, GN), 1) // N)
    row_t = lax.broadcasted_iota(jnp.int32, (L, GN), 0)
    lane_s = lax.broadcasted_iota(jnp.int32, (L, GN), 1) % L
    strict_c = lane_s < row_t
    incl_c = lane_s <= row_t
    eye_c = (lane_s == row_t).astype(F32)

    def expand(x):
        xb = x.astype(BF16)
        return jnp.where(expand_mask, jnp.concatenate([xb] * G, axis=0), jnp.zeros((), BF16))

    def compact_dot(x, w_bd):
        return jnp.dot(x.astype(BF16), w_bd, preferred_element_type=F32)

    R = range(groups)
    lanes = [slice(gi * GN, (gi + 1) * GN) for gi in R]
    load = lambda ref: [ref[0, :, lanes[gi]] for gi in R]
    at, rt, bt, kt, v = (load(ref) for ref in (at_ref, rt_ref, bt_ref, kt_ref, v_ref))
    s0 = [st_ref[gi] for gi in R]
    gam = [gam_ref[0, :, lanes[gi]] for gi in R]

    ar = [jnp.concatenate([at[gi], rt[gi]], axis=0) for gi in R]
    bt_e = [expand(x) for x in bt]
    kt_e = [expand(x) for x in kt]
    v_e = [expand(x) for x in v]
    gram_b = [_dot_nt(ar[gi], bt_e[gi]) for gi in R]
    gram_k = [_dot_nt(ar[gi], kt_e[gi]) for gi in R]
    a_ab = [jnp.where(strict_c, gram_b[gi][:L], 0.0) for gi in R]
    a_rb = [jnp.where(incl_c, gram_b[gi][L:], 0.0) for gi in R]
    a_ak = [jnp.where(strict_c, gram_k[gi][:L], 0.0) for gi in R]
    a_rk = [jnp.where(incl_c, gram_k[gi][L:], 0.0) for gi in R]

    t_c = [eye_c + a_ab[gi] for gi in R]
    p_bd = [expand(a_ab[gi]) for gi in R]
    p_c = [compact_dot(a_ab[gi], p_bd[gi]) for gi in R]
    p_bd = [expand(p_c[gi]) for gi in R]
    for _ in range(L.bit_length() - 3):
        tp = [compact_dot(jnp.concatenate([t_c[gi], p_c[gi]], axis=0), p_bd[gi]) for gi in R]
        t_c = [t_c[gi] + tp[gi][:L] for gi in R]
        p_c = [tp[gi][L:] for gi in R]
        p_bd = [expand(p_c[gi]) for gi in R]
    t_c = [t_c[gi] + compact_dot(t_c[gi], p_bd[gi]) for gi in R]

    s0b = [x.astype(BF16) for x in s0]
    from_state = [_dot_nt(ar[gi], s0b[gi]) for gi in R]
    rhs = [from_state[gi][:L] + compact_dot(a_ak[gi], v_e[gi]) for gi in R]
    u = [compact_dot(t_c[gi], expand(rhs[gi])) for gi in R]
    y = [from_state[gi][L:] + compact_dot(a_rb[gi], expand(u[gi]))
         + compact_dot(a_rk[gi], v_e[gi]) for gi in R]
    uv = [jnp.concatenate([u[gi], v[gi].astype(F32)], axis=0) for gi in R]
    bk = [jnp.concatenate([bt[gi], kt[gi]], axis=0) for gi in R]
    upd = [_dot(uv[gi].T, bk[gi]) for gi in R]
    for gi in R:
        y_ref[0, :, lanes[gi]] = y[gi]
        st_ref[gi] = (s0[gi] + jnp.where(seg_mask, upd[gi], 0.0)) * gam[gi]


def _rwkv_scan(ops, gamma, batch, seq_len, chunk=SCAN_CHUNK, heads=SCAN_HEADS,
               groups=SCAN_GROUPS):
    c = gamma.shape[-1]
    gn = heads * RWKV_HEAD
    lanes = groups * gn
    n_chunks = seq_len // chunk
    seq = pl.BlockSpec((1, chunk, lanes), lambda b, hg, t: (b, t, hg))
    gam = pl.BlockSpec((1, 1, lanes), lambda b, hg, t: (b * n_chunks + t, 0, hg))
    return pl.pallas_call(
        functools.partial(_scan_kernel, chunk=chunk, heads=heads, groups=groups),
        grid=(batch, c // lanes, n_chunks),
        in_specs=[seq] * 5 + [gam],
        out_specs=seq,
        out_shape=jax.ShapeDtypeStruct((batch, seq_len, c), F32),
        scratch_shapes=[pltpu.VMEM((groups, gn, gn), F32)],
        compiler_params=_params("parallel", "parallel", "arbitrary"),
        name="rwkv_scan",
    )(*[o.reshape(batch, seq_len, c) for o in ops], gamma.reshape(batch * n_chunks, 1, c))


def _gn_proj_ln_kernel(y_ref, bonus_ref, g_ref, gnw_ref, gnb_ref, ones_ref, w_ref, h_ref,
                       lnw_ref, lnb_ref, o_ref):
    ones = ones_ref[...]
    inv_n = 1.0 / RWKV_HEAD
    z = []
    for c0 in range(0, y_ref.shape[1], SEG_LANES):
        cb = slice(c0, c0 + SEG_LANES)
        y = y_ref[:, cb]
        mu = _seg_sum(y, ones) * inv_n
        d = y - mu
        var = _seg_sum(d * d, ones) * inv_n
        yn = d * lax.rsqrt(var + GN_EPS) * gnw_ref[:, cb] + gnb_ref[:, cb]
        z.append(((yn + bonus_ref[:, cb]) * g_ref[:, cb]).astype(BF16))
    proj = jnp.dot(jnp.concatenate(z, axis=1), w_ref[...], preferred_element_type=F32)
    o_ref[...] = _layernorm(ALPHA * h_ref[...] + proj, lnw_ref[...], lnb_ref[...])


def _gn_proj_ln(y, bonus, g, gn_w, gn_b, w, h, ln_w, ln_b, tm=TM // 2):
    m, c = y.shape
    tile = pl.BlockSpec((tm, c), lambda i: (i, 0))
    vec = pl.BlockSpec((1, c), lambda i: (0, 0))
    return pl.pallas_call(
        _gn_proj_ln_kernel,
        grid=(m // tm,),
        in_specs=[tile, tile, tile, vec, vec,
                  pl.BlockSpec((SEG_LANES, SEG_LANES), lambda i: (0, 0)),
                  pl.BlockSpec((c, c), lambda i: (0, 0)), tile, vec, vec],
        out_specs=tile,
        out_shape=jax.ShapeDtypeStruct((m, c), F32),
        compiler_params=_params("parallel"),
        name="rwkv_gn_proj_ln",
    )(y, bonus, g, gn_w, gn_b, _seg_ones(), w, h, ln_w, ln_b)


def _proj_ln_kernel(z_ref, w_ref, h_ref, lnw_ref, lnb_ref, o_ref):
    y = jnp.dot(z_ref[...], w_ref[...], preferred_element_type=F32)
    o_ref[...] = _layernorm(ALPHA * h_ref[...] + y, lnw_ref[...], lnb_ref[...])


def _proj_ln(z, w, h, ln_w, ln_b, tm=TM):
    m, kdim = z.shape
    c = w.shape[1]
    vec = pl.BlockSpec((1, c), lambda i: (0, 0))
    return pl.pallas_call(
        _proj_ln_kernel,
        grid=(m // tm,),
        in_specs=[pl.BlockSpec((tm, kdim), lambda i: (i, 0)),
                  pl.BlockSpec((kdim, c), lambda i: (0, 0)),
                  pl.BlockSpec((tm, c), lambda i: (i, 0)), vec, vec],
        out_specs=pl.BlockSpec((tm, c), lambda i: (i, 0)),
        out_shape=jax.ShapeDtypeStruct((m, c), F32),
        compiler_params=_params("parallel"),
        name="proj_residual_ln",
    )(z, w, h, ln_w, ln_b)


def _ffn_kernel(h_ref, wg_ref, wu_ref, wd_ref, lnw_ref, lnb_ref, o_ref, xb_ref, acc_ref):
    f = pl.program_id(1)

    @pl.when(f == 0)
    def _():
        xb_ref[...] = h_ref[...].astype(BF16)
        acc_ref[...] = jnp.zeros_like(acc_ref)

    x = xb_ref[...]
    gate = jnp.dot(x, wg_ref[...], preferred_element_type=F32)
    up = jnp.dot(x, wu_ref[...], preferred_element_type=F32)
    act = (gate * jax.nn.sigmoid(gate)) * up
    acc_ref[...] += jnp.dot(act.astype(BF16), wd_ref[...], preferred_element_type=F32)

    @pl.when(f == pl.num_programs(1) - 1)
    def _():
        o_ref[...] = _layernorm(ALPHA * h_ref[...] + acc_ref[...], lnw_ref[...], lnb_ref[...])


def _ffn_ln(h, w_gate, w_up, w_down, layer, ln_w, ln_b, tm=TM, tf=512):
    m, c = h.shape
    d_ff = w_gate.shape[2]
    vec = pl.BlockSpec((1, c), lambda i, f: (0, 0))
    return pl.pallas_call(
        _ffn_kernel,
        grid=(m // tm, d_ff // tf),
        in_specs=[pl.BlockSpec((tm, c), lambda i, f: (i, 0)),
                  pl.BlockSpec((None, c, tf), lambda i, f: (layer, 0, f)),
                  pl.BlockSpec((None, c, tf), lambda i, f: (layer, 0, f)),
                  pl.BlockSpec((None, tf, c), lambda i, f: (layer, f, 0)), vec, vec],
        out_specs=pl.BlockSpec((tm, c), lambda i, f: (i, 0)),
        out_shape=jax.ShapeDtypeStruct((m, c), F32),
        scratch_shapes=[pltpu.VMEM((tm, c), BF16), pltpu.VMEM((tm, c), F32)],
        compiler_params=_params("parallel", "arbitrary"),
        name="swiglu_ffn_ln",
    )(h, w_gate, w_up, w_down, ln_w, ln_b)


def _kv_kernel(x_ref, wk_ref, wvt_ref, k_ref, mean_ref, vt_ref, xb_ref):
    @pl.when(pl.program_id(1) == 0)
    def _():
        xb_ref[...] = x_ref[...].astype(BF16)

    xb = xb_ref[...]
    y = jnp.dot(xb, wk_ref[...], preferred_element_type=F32)
    k_ref[...] = y.astype(k_ref.dtype)
    tm, tn = y.shape
    nblk = tm // MOBA_BLOCK
    mean_ref[0] = jnp.sum(y.reshape(nblk, MOBA_BLOCK, tn), axis=1) * (1.0 / MOBA_BLOCK)
    vt = lax.dot_general(wvt_ref[...], xb, (((1,), (1,)), ((), ())), preferred_element_type=F32)
    for blk in range(nblk):
        vt_ref[blk] = vt[:, blk * MOBA_BLOCK:(blk + 1) * MOBA_BLOCK].astype(vt_ref.dtype)


def _kv_proj(h, w_k, w_v_t, tm=2 * TM, tn=512):
    m, c = h.shape
    n = w_k.shape[1]
    nblk = tm // MOBA_BLOCK
    return pl.pallas_call(
        _kv_kernel,
        grid=(m // tm, n // tn),
        in_specs=[pl.BlockSpec((tm, c), lambda i, j: (i, 0)),
                  pl.BlockSpec((c, tn), lambda i, j: (0, j)),
                  pl.BlockSpec((tn, c), lambda i, j: (j, 0))],
        out_specs=[pl.BlockSpec((tm, tn), lambda i, j: (i, j)),
                   pl.BlockSpec((1, nblk, tn), lambda i, j: (i, 0, j)),
                   pl.BlockSpec((nblk, tn, MOBA_BLOCK), lambda i, j: (i, j, 0))],
        out_shape=[jax.ShapeDtypeStruct((m, n), BF16),
                   jax.ShapeDtypeStruct((m // tm, nblk, n), F32),
                   jax.ShapeDtypeStruct((m // MOBA_BLOCK, n, MOBA_BLOCK), BF16)],
        scratch_shapes=[pltpu.VMEM((tm, c), BF16)],
        compiler_params=_params("parallel", "arbitrary"),
        name="shared_kv_proj",
    )(h, w_k, w_v_t)


def _q_kernel(x_ref, w_ref, o_ref, xb_ref):
    @pl.when(pl.program_id(1) == 0)
    def _():
        xb_ref[...] = x_ref[...].astype(BF16)

    o_ref[...] = jnp.dot(xb_ref[...], w_ref[...], preferred_element_type=F32)


def _q_proj(h, w_q, tm=2 * TM, tn=1024):
    m, c = h.shape
    n = w_q.shape[1]
    return pl.pallas_call(
        _q_kernel,
        grid=(m // tm, n // tn),
        in_specs=[pl.BlockSpec((tm, c), lambda i, j: (i, 0)),
                  pl.BlockSpec((c, tn), lambda i, j: (0, j))],
        out_specs=pl.BlockSpec((tm, tn), lambda i, j: (i, j)),
        out_shape=jax.ShapeDtypeStruct((m, n), F32),
        scratch_shapes=[pltpu.VMEM((tm, c), BF16)],
        compiler_params=_params("parallel", "arbitrary"),
        name="moba_q_proj",
    )(h, w_q)


def _moba_kernel(q_ref, k_ref, vt_ref, km_ref, o_ref, sel_ref, *, n_blocks, top_k, heads):
    blk = pl.program_id(2)
    BLK, DH = MOBA_BLOCK, MOBA_HEAD_DIM
    start = pl.multiple_of(blk * BLK, BLK)
    ki = lax.broadcasted_iota(jnp.int32, (BLK, BLK), 0)
    qi = lax.broadcasted_iota(jnp.int32, (BLK, BLK), 1)
    bi = lax.broadcasted_iota(jnp.int32, (n_blocks, BLK), 0)
    cols = [slice(hh * DH, (hh + 1) * DH) for hh in range(heads)]

    R = range(heads)
    q = [q_ref[:, cols[h]] for h in R]
    qb = [(q[h] * DH ** -0.5).astype(BF16) for h in R]

    gate = [_dot_nt_split(km_ref[0, :, cols[h]], q[h]) for h in R]
    rank = [jnp.zeros((n_blocks, BLK), jnp.int32) for h in R]
    for mm in range(n_blocks):
        past = (mm < blk).astype(jnp.int32)
        for h in R:
            gm = gate[h][mm:mm + 1, :]
            ahead = (gm > gate[h]) | ((gm == gate[h]) & (mm < bi))
            rank[h] = rank[h] + jnp.where(ahead, 1, 0) * past
    for h in R:
        sel_ref[h] = jnp.where((bi < blk) & (rank[h] < top_k), 1.0, 0.0)

    s = [_dot_nt(k_ref[pl.ds(start, BLK), cols[h]], qb[h]) for h in R]
    s = [jnp.where(ki <= qi, s[h], NEG) for h in R]
    ones_rows = jnp.ones((ONES_ROWS, BLK), BF16)
    v_ext = lambda n, h: jnp.concatenate([vt_ref[n, cols[h], :], ones_rows], axis=0)
    m0 = [jnp.max(s[h], axis=0, keepdims=True) for h in R]
    p = [jnp.exp((s[h] - m0[h]).astype(BF16)) for h in R]
    acc0 = [jnp.dot(v_ext(blk, h), p[h], preferred_element_type=F32) for h in R]

    def body(n, carry):
        m_i, acc = carry
        off = pl.multiple_of(n * BLK, BLK)
        sn = [_dot_nt(k_ref[pl.ds(off, BLK), cols[h]], qb[h]) for h in R]
        sn = [jnp.where(sel_ref[h, pl.ds(n, 1), :] > 0.0, sn[h], NEG) for h in R]
        m_new = [jnp.maximum(m_i[h], jnp.max(sn[h], axis=0, keepdims=True)) for h in R]
        corr = [jnp.exp(m_i[h] - m_new[h]) for h in R]
        pn = [jnp.exp((sn[h] - m_new[h]).astype(BF16)) for h in R]
        pv = [jnp.dot(v_ext(n, h), pn[h], preferred_element_type=F32) for h in R]
        acc_new = [corr[h] * acc[h] + pv[h] for h in R]
        return m_new, acc_new

    _, acc_f = lax.fori_loop(0, blk, body, (m0, acc0))
    for h in R:
        out = acc_f[h][:DH] / acc_f[h][DH:DH + 1]
        o_ref[:, cols[h]] = out.T.astype(o_ref.dtype)


def _moba_attention(q, k, v_t, k_mean, batch, seq_len, heads=MOBA_HEADS_PER_STEP):
    m, hd = q.shape
    n_blocks = seq_len // MOBA_BLOCK
    top_k = max(1, min(MOBA_TOPK, n_blocks - 1))
    BLK, W = MOBA_BLOCK, heads * MOBA_HEAD_DIM
    return pl.pallas_call(
        functools.partial(_moba_kernel, n_blocks=n_blocks, top_k=top_k, heads=heads),
        grid=(batch, hd // W, n_blocks),
        in_specs=[pl.BlockSpec((BLK, W), lambda b, h, t: (b * n_blocks + t, h)),
                  pl.BlockSpec((seq_len, W), lambda b, h, t: (b, h)),
                  pl.BlockSpec((n_blocks, W, BLK), lambda b, h, t: (b, h, 0)),
                  pl.BlockSpec((1, n_blocks, W), lambda b, h, t: (b, 0, h))],
        out_specs=pl.BlockSpec((BLK, W), lambda b, h, t: (b * n_blocks + t, h)),
        out_shape=jax.ShapeDtypeStruct((m, hd), BF16),
        scratch_shapes=[pltpu.VMEM((heads, n_blocks, BLK), F32)],
        compiler_params=_params("parallel", "parallel", "arbitrary"),
        name="moba_attention",
    )(q, k, v_t, k_mean)


def _pad_lora(w_in, w_out):
    rank = w_in.shape[1]
    pad = -rank % LORA_PAD
    return (jnp.pad(w_in, ((0, 0), (0, pad))).astype(BF16),
            jnp.pad(w_out, ((0, pad), (0, 0))).astype(BF16))


def kernel(x, a_mix, a_w_r, a_w_k, a_w_v, a_w_o, a_w0, a_w1, a_w2, a_a0, a_a1, a_a2, a_g1, a_g2,
           a_k_k, a_k_a, a_r_k, a_lnx_w, a_lnx_b, kv_w_k, kv_w_v, b_w_q, b_w_o, ffn_w_gate,
           ffn_w_up, ffn_w_down, ln1_w, ln1_b, ln2_w, ln2_b):
    batch, seq_len, c = x.shape
    n_a = a_mix.shape[0]
    n_layers = ffn_w_gate.shape[0]
    assert seq_len % MOBA_BLOCK == 0 and seq_len % TM == 0 and c == D_MODEL
    h = x.reshape(batch * seq_len, c)
    row = lambda vec: vec.reshape(1, c)
    k_all = v_t = k_mean = None
    ffn_gate_b, ffn_up_b, ffn_down_b = (w.astype(BF16) for w in (ffn_w_gate, ffn_w_up, ffn_w_down))

    for layer in range(n_layers):
        if layer < n_a:
            i = layer
            w1, w2 = _pad_lora(a_w1[i], a_w2[i])
            a1, a2 = _pad_lora(a_a1[i], a_a2[i])
            g1, g2 = _pad_lora(a_g1[i], a_g2[i])
            *ops, bonus, g, gamma = _rwkv_front(
                h, a_mix[i], a_w_r[i].astype(BF16), a_w_k[i].astype(BF16), a_w_v[i].astype(BF16),
                w1, w2, row(a_w0[i]), a1, a2, row(a_a0[i]), g1, g2, row(a_k_k[i]),
                row(a_k_a[i]), row(a_r_k[i]), seq_len, SCAN_CHUNK)
            y = _rwkv_scan(ops, gamma, batch, seq_len).reshape(batch * seq_len, c)
            h = _gn_proj_ln(y, bonus, g, row(a_lnx_w[i]), row(a_lnx_b[i]),
                            a_w_o[i].astype(BF16), h, row(ln1_w[layer]), row(ln1_b[layer]))
        else:
            j = layer - n_a
            q = _q_proj(h, b_w_q[j].astype(BF16))
            z = _moba_attention(q, k_all, v_t, k_mean, batch, seq_len)
            h = _proj_ln(z, b_w_o[j].astype(BF16), h, row(ln1_w[layer]), row(ln1_b[layer]))
        h = _ffn_ln(h, ffn_gate_b, ffn_up_b, ffn_down_b, layer, row(ln2_w[layer]),
                    row(ln2_b[layer]))
        if layer == n_a - 1:
            k_all, k_mean, v_t = _kv_proj(h, kv_w_k.astype(BF16), kv_w_v.T.astype(BF16))
            k_mean = k_mean.reshape(batch, seq_len // MOBA_BLOCK, -1)
    return h.reshape(batch, seq_len, c)
```

```python
import functools

import jax
import jax.numpy as jnp
from jax import lax
from jax.experimental import pallas as pl
from jax.experimental.pallas import tpu as pltpu

D_MODEL = 2048
RWKV_HEAD = 64
MOBA_HEAD_DIM = 128
MOBA_HEADS = D_MODEL // MOBA_HEAD_DIM
MOBA_BLOCK = 256
MOBA_TOPK = 3
NEG = -1e30
GN_EPS = 64e-5
LN_EPS = 1e-5
DEPTH = 2
ALPHA = (2 * DEPTH) ** 0.25
LORA_PAD = 128
NEG_EXP_MINUS_HALF = -0.6065306597126334

VMEM_LIMIT_BYTES = 56 * 1024 * 1024
FFN_VMEM_LIMIT_BYTES = 60000 * 1024

TM = 512
SCAN_CHUNK = 64
SCAN_HEADS = 4
SCAN_GROUPS = 8
MOBA_HEADS_PER_STEP = 8
ONES_ROWS = 16
EW_ROWS, EW_LANES = 512, 512
SEG_LANES = 256

F32 = jnp.float32
BF16 = jnp.bfloat16
HIGHEST = lax.Precision.HIGHEST


def _params(*semantics, vmem_limit_bytes=VMEM_LIMIT_BYTES):
    return pltpu.CompilerParams(dimension_semantics=semantics, vmem_limit_bytes=vmem_limit_bytes)


def _dot(a, b):
    return jnp.dot(a.astype(BF16), b.astype(BF16), preferred_element_type=F32)


def _dot_nt(a, b):
    return lax.dot_general(a.astype(BF16), b.astype(BF16), (((1,), (1,)), ((), ())),
                           preferred_element_type=F32)


def _dot_nt_split(a, b):
    a_hi = a.astype(BF16).astype(F32)
    b_hi = b.astype(BF16)
    b_lo = (b - b_hi.astype(F32)).astype(BF16)
    rows = a.shape[0]
    both = _dot_nt(jnp.concatenate([a_hi, a - a_hi], axis=0), b_hi)
    return both[:rows] + both[rows:] + _dot_nt(a_hi, b_lo)


def _seg_sum(x, seg_ones, split=True):
    width = seg_ones.shape[0]
    parts = []
    for c0 in range(0, x.shape[1], width):
        xs = x[:, c0:c0 + width]
        hi = xs.astype(BF16)
        acc = jnp.dot(hi, seg_ones, preferred_element_type=F32)
        if split:
            lo = (xs - hi.astype(F32)).astype(BF16)
            acc = acc + jnp.dot(lo, seg_ones, preferred_element_type=F32)
        parts.append(acc)
    return parts[0] if len(parts) == 1 else jnp.concatenate(parts, axis=1)


def _seg_ones():
    head = jnp.arange(SEG_LANES) // RWKV_HEAD
    return (head[:, None] == head[None, :]).astype(BF16)


def _layernorm(t, w, b):
    mu = jnp.mean(t, axis=-1, keepdims=True)
    d = t - mu
    var = jnp.mean(d * d, axis=-1, keepdims=True)
    return d * lax.rsqrt(var + LN_EPS) * w + b


def _token_shift_delta(x_ref, prev_ref, seq_start):
    x = x_ref[...]
    rolled = pltpu.roll(x, 1, 0)
    prev_last = jnp.where(seq_start, 0.0, prev_ref[7:8, :])
    row = lax.broadcasted_iota(jnp.int32, x.shape, 0)
    x_prev = jnp.where(row == 0, prev_last, rolled)
    return x, x_prev - x


def _front_kernel(x_ref, prev_ref, mix_ref, wr_ref, wk_ref, wv_ref, w1_ref, w2_ref, w0_ref,
                  a1_ref, a2_ref, a0_ref, g1_ref, g2_ref, kk_ref, ka_ref, rk_ref, ones_ref,
                  at_ref, rt_ref, bt_ref, kt_ref, vb_ref, bonus_ref, g_ref,
                  gam_ref, xr_ref, xk_ref, xv_ref, hw_ref, ha_ref, hg_ref,
                  *, tiles_per_seq, chunk):
    @pl.when(pl.program_id(1) == 0)
    def _():
        seq_start = (pl.program_id(0) % tiles_per_seq) == 0
        x, xx = _token_shift_delta(x_ref, prev_ref, seq_start)
        mixed = lambda idx: x + xx * mix_ref[idx:idx + 1, :]
        xr_ref[...] = mixed(0).astype(BF16)
        xk_ref[...] = mixed(2).astype(BF16)
        xv_ref[...] = mixed(3).astype(BF16)
        hw_ref[...] = jnp.tanh(_dot(mixed(1), w1_ref[...])).astype(BF16)
        ha_ref[...] = _dot(mixed(4), a1_ref[...]).astype(BF16)
        hg_ref[...] = jax.nn.sigmoid(_dot(mixed(5), g1_ref[...])).astype(BF16)

    dot = functools.partial(jnp.dot, preferred_element_type=F32)
    rows, lanes = at_ref.shape
    n_chunks = rows // chunk
    blocks = [slice(c0, c0 + SEG_LANES) for c0 in range(0, lanes, SEG_LANES)]
    xr, xk, xv, hw, ha, hg = (ref[...] for ref in (xr_ref, xk_ref, xv_ref, hw_ref, ha_ref, hg_ref))
    r = [dot(xr, wr_ref[:, cb]) for cb in blocks]
    k = [dot(xk, wk_ref[:, cb]) for cb in blocks]
    v = [dot(xv, wv_ref[:, cb]) for cb in blocks]
    w_pre = [w0_ref[:, cb] + dot(hw, w2_ref[:, cb]) for cb in blocks]
    a_pre = [a0_ref[:, cb] + dot(ha, a2_ref[:, cb]) for cb in blocks]
    for cb in blocks:
        g_ref[:, cb] = dot(hg, g2_ref[:, cb])

    ones = ones_ref[...]
    pos = lax.broadcasted_iota(jnp.int32, (rows, SEG_LANES), 0) % chunk
    for idx, cb in enumerate(blocks):
        ld = jax.nn.sigmoid(w_pre[idx]) * NEG_EXP_MINUS_HALF
        a = jax.nn.sigmoid(a_pre[idx])
        kk = k[idx] * kk_ref[:, cb]
        kk = kk * lax.rsqrt(jnp.maximum(_seg_sum(kk * kk, ones), 1e-24))
        k_mod = k[idx] * (1.0 + (a - 1.0) * ka_ref[:, cb])
        b_vec = kk * a

        cs = ld
        step = 1
        while step < chunk:
            cs = cs + jnp.where(pos >= step, pltpu.roll(cs, step, 0), 0.0)
            step *= 2
        cs3 = cs.reshape(n_chunks, chunk, SEG_LANES)
        cs_last = cs3[:, chunk - 1:chunk, :]
        e_neg = jnp.exp(-cs)

        at_ref[:, cb] = (-kk * jnp.exp(cs - ld)).astype(BF16)
        rt_ref[:, cb] = (r[idx] * jnp.exp(cs)).astype(BF16)
        bt_ref[:, cb] = (b_vec * e_neg).astype(BF16)
        kt_ref[:, cb] = (k_mod * e_neg).astype(BF16)
        vb_ref[:, cb] = v[idx].astype(BF16)
        bonus_ref[:, cb] = _seg_sum(r[idx] * k_mod * rk_ref[:, cb], ones, split=False) * v[idx]
        gam_ref[:, cb] = jnp.exp(cs_last).reshape(n_chunks, SEG_LANES)


def _rwkv_front(h, mix, w_r, w_k, w_v, w1, w2, w0, a1, a2, a0, g1, g2, k_k, k_a, r_k,
                seq_len, chunk, rows=EW_ROWS, lanes=EW_LANES):
    m, c = h.shape
    full = lambda arr: pl.BlockSpec(arr.shape, lambda i, j: (0,) * arr.ndim)
    cols = lambda arr: pl.BlockSpec((arr.shape[0], lanes), lambda i, j: (0, j))
    tile = pl.BlockSpec((rows, lanes), lambda i, j: (i, j))
    ones = _seg_ones()
    bf = jax.ShapeDtypeStruct((m, c), BF16)
    f32 = jax.ShapeDtypeStruct((m, c), F32)
    return pl.pallas_call(
        functools.partial(_front_kernel, tiles_per_seq=seq_len // rows, chunk=chunk),
        grid=(m // rows, c // lanes),
        in_specs=[pl.BlockSpec((rows, c), lambda i, j: (i, 0)),
                  pl.BlockSpec((8, c), lambda i, j: (jnp.maximum(i * (rows // 8) - 1, 0), 0)),
                  full(mix), cols(w_r), cols(w_k), cols(w_v),
                  full(w1), cols(w2), cols(w0), full(a1), cols(a2), cols(a0),
                  full(g1), cols(g2), cols(k_k), cols(k_a), cols(r_k), full(ones)],
        out_specs=[tile] * 7 + [pl.BlockSpec((rows // chunk, lanes), lambda i, j: (i, j))],
        out_shape=[bf] * 5 + [f32, f32, jax.ShapeDtypeStruct((m // chunk, c), F32)],
        scratch_shapes=[pltpu.VMEM((rows, c), BF16)] * 3
        + [pltpu.VMEM((rows, w1.shape[1]), BF16), pltpu.VMEM((rows, a1.shape[1]), BF16),
           pltpu.VMEM((rows, g1.shape[1]), BF16)],
        compiler_params=_params("parallel", "arbitrary"),
        name="rwkv_front",
    )(h, h, mix, w_r, w_k, w_v, w1, w2, w0, a1, a2, a0, g1, g2, k_k, k_a, r_k, ones)


def _scan_kernel(at_ref, rt_ref, bt_ref, kt_ref, v_ref, gam_ref, y_ref, st_ref,
                 *, chunk, heads, groups):
    L, G, N = chunk, heads, RWKV_HEAD
    GN, GL = G * N, G * L
    assert L == N

    @pl.when(pl.program_id(2) == 0)
    def _():
        st_ref[...] = jnp.zeros_like(st_ref)

    seg_mask = (lax.broadcasted_iota(jnp.int32, (GN, GN), 0) // N
                == lax.broadcasted_iota(jnp.int32, (GN, GN), 1) // N)
    expand_mask = (lax.broadcasted_iota(jnp.int32, (GL, GN), 0) // L
                   == lax.broadcasted_iota(jnp.int32, (GL, GN), 1) // N)
    row_t = lax.broadcasted_iota(jnp.int32, (L, GN), 0)
    lane_s = lax.broadcasted_iota(jnp.int32, (L, GN), 1) % L
    strict_c = lane_s < row_t
    incl_c = lane_s <= row_t
    eye_c = (lane_s == row_t).astype(F32)

    def expand(x):
        xb = x.astype(BF16)
        return jnp.where(expand_mask, jnp.concatenate([xb] * G, axis=0), jnp.zeros((), BF16))

    def compact_dot(x, w_bd):
        return jnp.dot(x.astype(BF16), w_bd, preferred_element_type=F32)

    R = range(groups)
    lanes = [slice(gi * GN, (gi + 1) * GN) for gi in R]
    load = lambda ref: [ref[0, :, lanes[gi]] for gi in R]
    at, rt, bt, kt, v = (load(ref) for ref in (at_ref, rt_ref, bt_ref, kt_ref, v_ref))
    s0 = [st_ref[gi] for gi in R]
    gam = [gam_ref[0, :, lanes[gi]] for gi in R]

    ar = [jnp.concatenate([at[gi], rt[gi]], axis=0) for gi in R]
    bt_e = [expand(x) for x in bt]
    kt_e = [expand(x) for x in kt]
    v_e = [expand(x) for x in v]
    gram_b = [_dot_nt(ar[gi], bt_e[gi]) for gi in R]
    gram_k = [_dot_nt(ar[gi], kt_e[gi]) for gi in R]
    a_ab = [jnp.where(strict_c, gram_b[gi][:L], 0.0) for gi in R]
    a_rb = [jnp.where(incl_c, gram_b[gi][L:], 0.0) for gi in R]
    a_ak = [jnp.where(strict_c, gram_k[gi][:L], 0.0) for gi in R]
    a_rk = [jnp.where(incl_c, gram_k[gi][L:], 0.0) for gi in R]

    t_c = [eye_c + a_ab[gi] for gi in R]
    p_bd = [expand(a_ab[gi]) for gi in R]
    p_c = [compact_dot(a_ab[gi], p_bd[gi]) for gi in R]
    p_bd = [expand(p_c[gi]) for gi in R]
    for _ in range(L.bit_length() - 3):
        tp = [compact_dot(jnp.concatenate([t_c[gi], p_c[gi]], axis=0), p_bd[gi]) for gi in R]
        t_c = [t_c[gi] + tp[gi][:L] for gi in R]
        p_c = [tp[gi][L:] for gi in R]
        p_bd = [expand(p_c[gi]) for gi in R]
    t_c = [t_c[gi] + compact_dot(t_c[gi], p_bd[gi]) for gi in R]

    s0b = [x.astype(BF16) for x in s0]
    from_state = [_dot_nt(ar[gi], s0b[gi]) for gi in R]
    rhs = [from_state[gi][:L] + compact_dot(a_ak[gi], v_e[gi]) for gi in R]
    u = [compact_dot(t_c[gi], expand(rhs[gi])) for gi in R]
    y = [from_state[gi][L:] + compact_dot(a_rb[gi], expand(u[gi]))
         + compact_dot(a_rk[gi], v_e[gi]) for gi in R]
    uv = [jnp.concatenate([u[gi], v[gi].astype(F32)], axis=0) for gi in R]
    bk = [jnp.concatenate([bt[gi], kt[gi]], axis=0) for gi in R]
    upd = [_dot(uv[gi].T, bk[gi]) for gi in R]
    for gi in R:
        y_ref[0, :, lanes[gi]] = y[gi]
        st_ref[gi] = (s0[gi] + jnp.where(seg_mask, upd[gi], 0.0)) * gam[gi]


def _rwkv_scan(ops, gamma, batch, seq_len, chunk=SCAN_CHUNK, heads=SCAN_HEADS,
               groups=SCAN_GROUPS):
    c = gamma.shape[-1]
    gn = heads * RWKV_HEAD
    lanes = groups * gn
    n_chunks = seq_len // chunk
    seq = pl.BlockSpec((1, chunk, lanes), lambda b, hg, t: (b, t, hg))
    gam = pl.BlockSpec((1, 1, lanes), lambda b, hg, t: (b * n_chunks + t, 0, hg))
    return pl.pallas_call(
        functools.partial(_scan_kernel, chunk=chunk, heads=heads, groups=groups),
        grid=(batch, c // lanes, n_chunks),
        in_specs=[seq] * 5 + [gam],
        out_specs=seq,
        out_shape=jax.ShapeDtypeStruct((batch, seq_len, c), F32),
        scratch_shapes=[pltpu.VMEM((groups, gn, gn), F32)],
        compiler_params=_params("parallel", "parallel", "arbitrary"),
        name="rwkv_scan",
    )(*[o.reshape(batch, seq_len, c) for o in ops], gamma.reshape(batch * n_chunks, 1, c))


def _gn_proj_ln_kernel(y_ref, bonus_ref, g_ref, gnw_ref, gnb_ref, ones_ref, w_ref, h_ref,
                       lnw_ref, lnb_ref, o_ref):
    ones = ones_ref[...]
    inv_n = 1.0 / RWKV_HEAD
    z = []
    for c0 in range(0, y_ref.shape[1], SEG_LANES):
        cb = slice(c0, c0 + SEG_LANES)
        y = y_ref[:, cb]
        mu = _seg_sum(y, ones) * inv_n
        d = y - mu
        var = _seg_sum(d * d, ones) * inv_n
        yn = d * lax.rsqrt(var + GN_EPS) * gnw_ref[:, cb] + gnb_ref[:, cb]
        z.append(((yn + bonus_ref[:, cb]) * g_ref[:, cb]).astype(BF16))
    proj = jnp.dot(jnp.concatenate(z, axis=1), w_ref[...], preferred_element_type=F32)
    o_ref[...] = _layernorm(ALPHA * h_ref[...] + proj, lnw_ref[...], lnb_ref[...])


def _gn_proj_ln(y, bonus, g, gn_w, gn_b, w, h, ln_w, ln_b, tm=TM // 2):
    m, c = y.shape
    tile = pl.BlockSpec((tm, c), lambda i: (i, 0))
    vec = pl.BlockSpec((1, c), lambda i: (0, 0))
    return pl.pallas_call(
        _gn_proj_ln_kernel,
        grid=(m // tm,),
        in_specs=[tile, tile, tile, vec, vec,
                  pl.BlockSpec((SEG_LANES, SEG_LANES), lambda i: (0, 0)),
                  pl.BlockSpec((c, c), lambda i: (0, 0)), tile, vec, vec],
        out_specs=tile,
        out_shape=jax.ShapeDtypeStruct((m, c), F32),
        compiler_params=_params("parallel"),
        name="rwkv_gn_proj_ln",
    )(y, bonus, g, gn_w, gn_b, _seg_ones(), w, h, ln_w, ln_b)


def _proj_ln_kernel(z_ref, w_ref, h_ref, lnw_ref, lnb_ref, o_ref):
    y = jnp.dot(z_ref[...], w_ref[...], preferred_element_type=F32)
    o_ref[...] = _layernorm(ALPHA * h_ref[...] + y, lnw_ref[...], lnb_ref[...])


def _proj_ln(z, w, h, ln_w, ln_b, tm=TM):
    m, kdim = z.shape
    c = w.shape[1]
    vec = pl.BlockSpec((1, c), lambda i: (0, 0))
    return pl.pallas_call(
        _proj_ln_kernel,
        grid=(m // tm,),
        in_specs=[pl.BlockSpec((tm, kdim), lambda i: (i, 0)),
                  pl.BlockSpec((kdim, c), lambda i: (0, 0)),
                  pl.BlockSpec((tm, c), lambda i: (i, 0)), vec, vec],
        out_specs=pl.BlockSpec((tm, c), lambda i: (i, 0)),
        out_shape=jax.ShapeDtypeStruct((m, c), F32),
        compiler_params=_params("parallel"),
        name="proj_residual_ln",
    )(z, w, h, ln_w, ln_b)


def _ffn_kernel(h_ref, wg_ref, wu_ref, wd_ref, lnw_ref, lnb_ref, o_ref):
    f = pl.program_id(1)

    @pl.when(f == 0)
    def _():
        o_ref[...] = jnp.zeros_like(o_ref)

    x = h_ref[...].astype(BF16)
    gate = jnp.dot(x, wg_ref[...], preferred_element_type=F32)
    up = jnp.dot(x, wu_ref[...], preferred_element_type=F32)
    act = (gate * jax.nn.sigmoid(gate)) * up
    o_ref[...] += jnp.dot(act.astype(BF16), wd_ref[...], preferred_element_type=F32)

    @pl.when(f == pl.num_programs(1) - 1)
    def _():
        o_ref[...] = _layernorm(ALPHA * h_ref[...] + o_ref[...], lnw_ref[...], lnb_ref[...])


def _ffn_ln(h, w_gate, w_up, w_down, layer, ln_w, ln_b, tm=2 * TM, tf=512):
    m, c = h.shape
    d_ff = w_gate.shape[2]
    vec = pl.BlockSpec((1, c), lambda i, f: (0, 0))
    return pl.pallas_call(
        _ffn_kernel,
        grid=(m // tm, d_ff // tf),
        in_specs=[pl.BlockSpec((tm, c), lambda i, f: (i, 0)),
                  pl.BlockSpec((None, c, tf), lambda i, f: (layer, 0, f)),
                  pl.BlockSpec((None, c, tf), lambda i, f: (layer, 0, f)),
                  pl.BlockSpec((None, tf, c), lambda i, f: (layer, f, 0)), vec, vec],
        out_specs=pl.BlockSpec((tm, c), lambda i, f: (i, 0)),
        out_shape=jax.ShapeDtypeStruct((m, c), F32),
        compiler_params=_params("parallel", "arbitrary", vmem_limit_bytes=FFN_VMEM_LIMIT_BYTES),
        name="swiglu_ffn_ln",
    )(h, w_gate, w_up, w_down, ln_w, ln_b)


def _kv_kernel(x_ref, wk_ref, wvt_ref, k_ref, mean_ref, vt_ref, xb_ref):
    @pl.when(pl.program_id(1) == 0)
    def _():
        xb_ref[...] = x_ref[...].astype(BF16)

    xb = xb_ref[...]
    y = jnp.dot(xb, wk_ref[...], preferred_element_type=F32)
    k_ref[...] = y.astype(k_ref.dtype)
    tm, tn = y.shape
    nblk = tm // MOBA_BLOCK
    mean_ref[0] = jnp.sum(y.reshape(nblk, MOBA_BLOCK, tn), axis=1) * (1.0 / MOBA_BLOCK)
    vt = lax.dot_general(wvt_ref[...], xb, (((1,), (1,)), ((), ())), preferred_element_type=F32)
    for blk in range(nblk):
        vt_ref[blk] = vt[:, blk * MOBA_BLOCK:(blk + 1) * MOBA_BLOCK].astype(vt_ref.dtype)


def _kv_proj(h, w_k, w_v_t, tm=2 * TM, tn=512):
    m, c = h.shape
    n = w_k.shape[1]
    nblk = tm // MOBA_BLOCK
    return pl.pallas_call(
        _kv_kernel,
        grid=(m // tm, n // tn),
        in_specs=[pl.BlockSpec((tm, c), lambda i, j: (i, 0)),
                  pl.BlockSpec((c, tn), lambda i, j: (0, j)),
                  pl.BlockSpec((tn, c), lambda i, j: (j, 0))],
        out_specs=[pl.BlockSpec((tm, tn), lambda i, j: (i, j)),
                   pl.BlockSpec((1, nblk, tn), lambda i, j: (i, 0, j)),
                   pl.BlockSpec((nblk, tn, MOBA_BLOCK), lambda i, j: (i, j, 0))],
        out_shape=[jax.ShapeDtypeStruct((m, n), BF16),
                   jax.ShapeDtypeStruct((m // tm, nblk, n), F32),
                   jax.ShapeDtypeStruct((m // MOBA_BLOCK, n, MOBA_BLOCK), BF16)],
        scratch_shapes=[pltpu.VMEM((tm, c), BF16)],
        compiler_params=_params("parallel", "arbitrary"),
        name="shared_kv_proj",
    )(h, w_k, w_v_t)


def _q_kernel(x_ref, w_ref, o_ref, xb_ref):
    @pl.when(pl.program_id(1) == 0)
    def _():
        xb_ref[...] = x_ref[...].astype(BF16)

    o_ref[...] = jnp.dot(xb_ref[...], w_ref[...], preferred_element_type=F32)


def _q_proj(h, w_q, tm=2 * TM, tn=1024):
    m, c = h.shape
    n = w_q.shape[1]
    return pl.pallas_call(
        _q_kernel,
        grid=(m // tm, n // tn),
        in_specs=[pl.BlockSpec((tm, c), lambda i, j: (i, 0)),
                  pl.BlockSpec((c, tn), lambda i, j: (0, j))],
        out_specs=pl.BlockSpec((tm, tn), lambda i, j: (i, j)),
        out_shape=jax.ShapeDtypeStruct((m, n), F32),
        scratch_shapes=[pltpu.VMEM((tm, c), BF16)],
        compiler_params=_params("parallel", "arbitrary"),
        name="moba_q_proj",
    )(h, w_q)


def _moba_kernel(q_ref, k_ref, vt_ref, km_ref, o_ref, sel_ref, *, n_blocks, top_k, heads):
    blk = pl.program_id(2)
    BLK, DH = MOBA_BLOCK, MOBA_HEAD_DIM
    start = pl.multiple_of(blk * BLK, BLK)
    ki = lax.broadcasted_iota(jnp.int32, (BLK, BLK), 0)
    qi = lax.broadcasted_iota(jnp.int32, (BLK, BLK), 1)
    bi = lax.broadcasted_iota(jnp.int32, (n_blocks, BLK), 0)
    cols = [slice(hh * DH, (hh + 1) * DH) for hh in range(heads)]

    R = range(heads)
    q = [q_ref[:, cols[h]] for h in R]
    qb = [(q[h] * DH ** -0.5).astype(BF16) for h in R]

    gate = [_dot_nt_split(km_ref[0, :, cols[h]], q[h]) for h in R]
    rank = [jnp.zeros((n_blocks, BLK), jnp.int32) for h in R]
    for mm in range(n_blocks):
        past = (mm < blk).astype(jnp.int32)
        for h in R:
            gm = gate[h][mm:mm + 1, :]
            ahead = (gm > gate[h]) | ((gm == gate[h]) & (mm < bi))
            rank[h] = rank[h] + jnp.where(ahead, 1, 0) * past
    for h in R:
        sel_ref[h] = jnp.where((bi < blk) & (rank[h] < top_k), 1.0, 0.0)

    s = [_dot_nt(k_ref[pl.ds(start, BLK), cols[h]], qb[h]) for h in R]
    s = [jnp.where(ki <= qi, s[h], NEG) for h in R]
    ones_rows = jnp.ones((ONES_ROWS, BLK), BF16)
    v_ext = lambda n, h: jnp.concatenate([vt_ref[n, cols[h], :], ones_rows], axis=0)
    m0 = [jnp.max(s[h], axis=0, keepdims=True) for h in R]
    p = [jnp.exp((s[h] - m0[h]).astype(BF16)) for h in R]
    acc0 = [jnp.dot(v_ext(blk, h), p[h], preferred_element_type=F32) for h in R]

    def body(n, carry):
        m_i, acc = carry
        off = pl.multiple_of(n * BLK, BLK)
        sn = [_dot_nt(k_ref[pl.ds(off, BLK), cols[h]], qb[h]) for h in R]
        sn = [jnp.where(sel_ref[h, pl.ds(n, 1), :] > 0.0, sn[h], NEG) for h in R]
        m_new = [jnp.maximum(m_i[h], jnp.max(sn[h], axis=0, keepdims=True)) for h in R]
        corr = [jnp.exp(m_i[h] - m_new[h]) for h in R]
        pn = [jnp.exp((sn[h] - m_new[h]).astype(BF16)) for h in R]
        pv = [jnp.dot(v_ext(n, h), pn[h], preferred_element_type=F32) for h in R]
        acc_new = [corr[h] * acc[h] + pv[h] for h in R]
        return m_new, acc_new

    _, acc_f = lax.fori_loop(0, blk, body, (m0, acc0))
    for h in R:
        out = acc_f[h][:DH] / acc_f[h][DH:DH + 1]
        o_ref[:, cols[h]] = out.T.astype(o_ref.dtype)


def _moba_attention(q, k, v_t, k_mean, batch, seq_len, heads=MOBA_HEADS_PER_STEP):
    m, hd = q.shape
    n_blocks = seq_len // MOBA_BLOCK
    top_k = max(1, min(MOBA_TOPK, n_blocks - 1))
    BLK, W = MOBA_BLOCK, heads * MOBA_HEAD_DIM
    return pl.pallas_call(
        functools.partial(_moba_kernel, n_blocks=n_blocks, top_k=top_k, heads=heads),
        grid=(batch, hd // W, n_blocks),
        in_specs=[pl.BlockSpec((BLK, W), lambda b, h, t: (b * n_blocks + t, h)),
                  pl.BlockSpec((seq_len, W), lambda b, h, t: (b, h)),
                  pl.BlockSpec((n_blocks, W, BLK), lambda b, h, t: (b, h, 0)),
                  pl.BlockSpec((1, n_blocks, W), lambda b, h, t: (b, 0, h))],
        out_specs=pl.BlockSpec((BLK, W), lambda b, h, t: (b * n_blocks + t, h)),
        out_shape=jax.ShapeDtypeStruct((m, hd), BF16),
        scratch_shapes=[pltpu.VMEM((heads, n_blocks, BLK), F32)],
        compiler_params=_params("parallel", "parallel", "arbitrary"),
        name="moba_attention",
    )(q, k, v_t, k_mean)


def _pad_lora(w_in, w_out):
    rank = w_in.shape[1]
    pad = -rank % LORA_PAD
    return (jnp.pad(w_in, ((0, 0), (0, pad))).astype(BF16),
            jnp.pad(w_out, ((0, pad), (0, 0))).astype(BF16))


def kernel(x, a_mix, a_w_r, a_w_k, a_w_v, a_w_o, a_w0, a_w1, a_w2, a_a0, a_a1, a_a2, a_g1, a_g2,
           a_k_k, a_k_a, a_r_k, a_lnx_w, a_lnx_b, kv_w_k, kv_w_v, b_w_q, b_w_o, ffn_w_gate,
           ffn_w_up, ffn_w_down, ln1_w, ln1_b, ln2_w, ln2_b):
    batch, seq_len, c = x.shape
    n_a = a_mix.shape[0]
    n_layers = ffn_w_gate.shape[0]
    assert seq_len % MOBA_BLOCK == 0 and seq_len % TM == 0 and c == D_MODEL
    h = x.reshape(batch * seq_len, c)
    row = lambda vec: vec.reshape(1, c)
    k_all = v_t = k_mean = None
    ffn_gate_b, ffn_up_b, ffn_down_b = (w.astype(BF16) for w in (ffn_w_gate, ffn_w_up, ffn_w_down))

    for layer in range(n_layers):
        if layer < n_a:
            i = layer
            w1, w2 = _pad_lora(a_w1[i], a_w2[i])
            a1, a2 = _pad_lora(a_a1[i], a_a2[i])
            g1, g2 = _pad_lora(a_g1[i], a_g2[i])
            *ops, bonus, g, gamma = _rwkv_front(
                h, a_mix[i], a_w_r[i].astype(BF16), a_w_k[i].astype(BF16), a_w_v[i].astype(BF16),
                w1, w2, row(a_w0[i]), a1, a2, row(a_a0[i]), g1, g2, row(a_k_k[i]),
                row(a_k_a[i]), row(a_r_k[i]), seq_len, SCAN_CHUNK)
            y = _rwkv_scan(ops, gamma, batch, seq_len).reshape(batch * seq_len, c)
            h = _gn_proj_ln(y, bonus, g, row(a_lnx_w[i]), row(a_lnx_b[i]),
                            a_w_o[i].astype(BF16), h, row(ln1_w[layer]), row(ln1_b[layer]))
        else:
            j = layer - n_a
            q = _q_proj(h, b_w_q[j].astype(BF16))
            z = _moba_attention(q, k_all, v_t, k_mean, batch, seq_len)
            h = _proj_ln(z, b_w_o[j].astype(BF16), h, row(ln1_w[layer]), row(ln1_b[layer]))
        h = _ffn_ln(h, ffn_gate_b, ffn_up_b, ffn_down_b, layer, row(ln2_w[layer]),
                    row(ln2_b[layer]))
        if layer == n_a - 1:
            k_all, k_mean, v_t = _kv_proj(h, kv_w_k.astype(BF16), kv_w_v.T.astype(BF16))
            k_mean = k_mean.reshape(batch, seq_len // MOBA_BLOCK, -1)
    return h.reshape(batch, seq_len, c)
```

```python
import functools

import jax
import jax.numpy as jnp
from jax import lax
from jax.experimental import pallas as pl
from jax.experimental.pallas import tpu as pltpu

D_MODEL = 2048
RWKV_HEAD = 64
MOBA_HEAD_DIM = 128
MOBA_HEADS = D_MODEL // MOBA_HEAD_DIM
MOBA_BLOCK = 256
MOBA_TOPK = 3
NEG = -1e30
GN_EPS = 64e-5
LN_EPS = 1e-5
DEPTH = 2
ALPHA = (2 * DEPTH) ** 0.25
LORA_PAD = 128
NEG_EXP_MINUS_HALF = -0.6065306597126334

VMEM_LIMIT_BYTES = 56 * 1024 * 1024
FFN_VMEM_LIMIT_BYTES = 60000 * 1024

TM = 512
SCAN_CHUNK = 64
SCAN_HEADS = 4
SCAN_GROUPS = 8
MOBA_HEADS_PER_STEP = 16
ONES_ROWS = 16
EW_ROWS, EW_LANES = 512, 512
SEG_LANES = 256

F32 = jnp.float32
BF16 = jnp.bfloat16
HIGHEST = lax.Precision.HIGHEST


def _params(*semantics, vmem_limit_bytes=VMEM_LIMIT_BYTES):
    return pltpu.CompilerParams(dimension_semantics=semantics, vmem_limit_bytes=vmem_limit_bytes)


def _dot(a, b):
    return jnp.dot(a.astype(BF16), b.astype(BF16), preferred_element_type=F32)


def _dot_nt(a, b):
    return lax.dot_general(a.astype(BF16), b.astype(BF16), (((1,), (1,)), ((), ())),
                           preferred_element_type=F32)


def _dot_nt_split(a, b):
    a_hi = a.astype(BF16).astype(F32)
    b_hi = b.astype(BF16)
    b_lo = (b - b_hi.astype(F32)).astype(BF16)
    rows = a.shape[0]
    both = _dot_nt(jnp.concatenate([a_hi, a - a_hi], axis=0), b_hi)
    return both[:rows] + both[rows:] + _dot_nt(a_hi, b_lo)


def _seg_sum(x, seg_ones, split=True):
    width = seg_ones.shape[0]
    parts = []
    for c0 in range(0, x.shape[1], width):
        xs = x[:, c0:c0 + width]
        hi = xs.astype(BF16)
        acc = jnp.dot(hi, seg_ones, preferred_element_type=F32)
        if split:
            lo = (xs - hi.astype(F32)).astype(BF16)
            acc = acc + jnp.dot(lo, seg_ones, preferred_element_type=F32)
        parts.append(acc)
    return parts[0] if len(parts) == 1 else jnp.concatenate(parts, axis=1)


def _seg_ones():
    head = jnp.arange(SEG_LANES) // RWKV_HEAD
    return (head[:, None] == head[None, :]).astype(BF16)


def _layernorm(t, w, b):
    mu = jnp.mean(t, axis=-1, keepdims=True)
    d = t - mu
    var = jnp.mean(d * d, axis=-1, keepdims=True)
    return d * lax.rsqrt(var + LN_EPS) * w + b


def _token_shift_delta(x_ref, prev_ref, seq_start):
    x = x_ref[...]
    rolled = pltpu.roll(x, 1, 0)
    prev_last = jnp.where(seq_start, 0.0, prev_ref[7:8, :])
    row = lax.broadcasted_iota(jnp.int32, x.shape, 0)
    x_prev = jnp.where(row == 0, prev_last, rolled)
    return x, x_prev - x


def _front_kernel(x_ref, prev_ref, mix_ref, wr_ref, wk_ref, wv_ref, w1_ref, w2_ref, w0_ref,
                  a1_ref, a2_ref, a0_ref, g1_ref, g2_ref, kk_ref, ka_ref, rk_ref, ones_ref,
                  at_ref, rt_ref, bt_ref, kt_ref, vb_ref, bonus_ref, g_ref,
                  gam_ref, xr_ref, xk_ref, xv_ref, hw_ref, ha_ref, hg_ref,
                  *, tiles_per_seq, chunk):
    @pl.when(pl.program_id(1) == 0)
    def _():
        seq_start = (pl.program_id(0) % tiles_per_seq) == 0
        x, xx = _token_shift_delta(x_ref, prev_ref, seq_start)
        mixed = lambda idx: x + xx * mix_ref[idx:idx + 1, :]
        xr_ref[...] = mixed(0).astype(BF16)
        xk_ref[...] = mixed(2).astype(BF16)
        xv_ref[...] = mixed(3).astype(BF16)
        hw_ref[...] = jnp.tanh(_dot(mixed(1), w1_ref[...])).astype(BF16)
        ha_ref[...] = _dot(mixed(4), a1_ref[...]).astype(BF16)
        hg_ref[...] = jax.nn.sigmoid(_dot(mixed(5), g1_ref[...])).astype(BF16)

    dot = functools.partial(jnp.dot, preferred_element_type=F32)
    rows, lanes = at_ref.shape
    n_chunks = rows // chunk
    blocks = [slice(c0, c0 + SEG_LANES) for c0 in range(0, lanes, SEG_LANES)]
    xr, xk, xv, hw, ha, hg = (ref[...] for ref in (xr_ref, xk_ref, xv_ref, hw_ref, ha_ref, hg_ref))
    r = [dot(xr, wr_ref[:, cb]) for cb in blocks]
    k = [dot(xk, wk_ref[:, cb]) for cb in blocks]
    v = [dot(xv, wv_ref[:, cb]) for cb in blocks]
    w_pre = [w0_ref[:, cb] + dot(hw, w2_ref[:, cb]) for cb in blocks]
    a_pre = [a0_ref[:, cb] + dot(ha, a2_ref[:, cb]) for cb in blocks]
    for cb in blocks:
        g_ref[:, cb] = dot(hg, g2_ref[:, cb])

    ones = ones_ref[...]
    pos = lax.broadcasted_iota(jnp.int32, (rows, SEG_LANES), 0) % chunk
    for idx, cb in enumerate(blocks):
        ld = jax.nn.sigmoid(w_pre[idx]) * NEG_EXP_MINUS_HALF
        a = jax.nn.sigmoid(a_pre[idx])
        kk = k[idx] * kk_ref[:, cb]
        kk = kk * lax.rsqrt(jnp.maximum(_seg_sum(kk * kk, ones), 1e-24))
        k_mod = k[idx] * (1.0 + (a - 1.0) * ka_ref[:, cb])
        b_vec = kk * a

        cs = ld
        step = 1
        while step < chunk:
            cs = cs + jnp.where(pos >= step, pltpu.roll(cs, step, 0), 0.0)
            step *= 2
        cs3 = cs.reshape(n_chunks, chunk, SEG_LANES)
        cs_last = cs3[:, chunk - 1:chunk, :]
        e_neg = jnp.exp(-cs)

        at_ref[:, cb] = (-kk * jnp.exp(cs - ld)).astype(BF16)
        rt_ref[:, cb] = (r[idx] * jnp.exp(cs)).astype(BF16)
        bt_ref[:, cb] = (b_vec * e_neg).astype(BF16)
        kt_ref[:, cb] = (k_mod * e_neg).astype(BF16)
        vb_ref[:, cb] = v[idx].astype(BF16)
        bonus_ref[:, cb] = _seg_sum(r[idx] * k_mod * rk_ref[:, cb], ones, split=False) * v[idx]
        gam_ref[:, cb] = jnp.exp(cs_last).reshape(n_chunks, SEG_LANES)


def _rwkv_front(h, mix, w_r, w_k, w_v, w1, w2, w0, a1, a2, a0, g1, g2, k_k, k_a, r_k,
                seq_len, chunk, rows=EW_ROWS, lanes=EW_LANES):
    m, c = h.shape
    full = lambda arr: pl.BlockSpec(arr.shape, lambda i, j: (0,) * arr.ndim)
    cols = lambda arr: pl.BlockSpec((arr.shape[0], lanes), lambda i, j: (0, j))
    tile = pl.BlockSpec((rows, lanes), lambda i, j: (i, j))
    ones = _seg_ones()
    bf = jax.ShapeDtypeStruct((m, c), BF16)
    f32 = jax.ShapeDtypeStruct((m, c), F32)
    return pl.pallas_call(
        functools.partial(_front_kernel, tiles_per_seq=seq_len // rows, chunk=chunk),
        grid=(m // rows, c // lanes),
        in_specs=[pl.BlockSpec((rows, c), lambda i, j: (i, 0)),
                  pl.BlockSpec((8, c), lambda i, j: (jnp.maximum(i * (rows // 8) - 1, 0), 0)),
                  full(mix), cols(w_r), cols(w_k), cols(w_v),
                  full(w1), cols(w2), cols(w0), full(a1), cols(a2), cols(a0),
                  full(g1), cols(g2), cols(k_k), cols(k_a), cols(r_k), full(ones)],
        out_specs=[tile] * 7 + [pl.BlockSpec((rows // chunk, lanes), lambda i, j: (i, j))],
        out_shape=[bf] * 5 + [f32, f32, jax.ShapeDtypeStruct((m // chunk, c), F32)],
        scratch_shapes=[pltpu.VMEM((rows, c), BF16)] * 3
        + [pltpu.VMEM((rows, w1.shape[1]), BF16), pltpu.VMEM((rows, a1.shape[1]), BF16),
           pltpu.VMEM((rows, g1.shape[1]), BF16)],
        compiler_params=_params("parallel", "arbitrary"),
        name="rwkv_front",
    )(h, h, mix, w_r, w_k, w_v, w1, w2, w0, a1, a2, a0, g1, g2, k_k, k_a, r_k, ones)


def _scan_kernel(at_ref, rt_ref, bt_ref, kt_ref, v_ref, gam_ref, y_ref, st_ref,
                 *, chunk, heads, groups):
    L, G, N = chunk, heads, RWKV_HEAD
    GN, GL = G * N, G * L
    assert L == N

    @pl.when(pl.program_id(2) == 0)
    def _():
        st_ref[...] = jnp.zeros_like(st_ref)

    seg_mask = (lax.broadcasted_iota(jnp.int32, (GN, GN), 0) // N
                == lax.broadcasted_iota(jnp.int32, (GN, GN), 1) // N)
    expand_mask = (lax.broadcasted_iota(jnp.int32, (GL, GN), 0) // L
                   == lax.broadcasted_iota(jnp.int32, (GL, GN), 1) // N)
    row_t = lax.broadcasted_iota(jnp.int32, (L, GN), 0)
    lane_s = lax.broadcasted_iota(jnp.int32, (L, GN), 1) % L
    strict_c = lane_s < row_t
    incl_c = lane_s <= row_t
    eye_c = (lane_s == row_t).astype(F32)

    def expand(x):
        xb = x.astype(BF16)
        return jnp.where(expand_mask, jnp.concatenate([xb] * G, axis=0), jnp.zeros((), BF16))

    def compact_dot(x, w_bd):
        return jnp.dot(x.astype(BF16), w_bd, preferred_element_type=F32)

    R = range(groups)
    lanes = [slice(gi * GN, (gi + 1) * GN) for gi in R]
    load = lambda ref: [ref[0, :, lanes[gi]] for gi in R]
    at, rt, bt, kt, v = (load(ref) for ref in (at_ref, rt_ref, bt_ref, kt_ref, v_ref))
    s0 = [st_ref[gi] for gi in R]
    gam = [gam_ref[0, :, lanes[gi]] for gi in R]

    ar = [jnp.concatenate([at[gi], rt[gi]], axis=0) for gi in R]
    bt_e = [expand(x) for x in bt]
    kt_e = [expand(x) for x in kt]
    v_e = [expand(x) for x in v]
    gram_b = [_dot_nt(ar[gi], bt_e[gi]) for gi in R]
    gram_k = [_dot_nt(ar[gi], kt_e[gi]) for gi in R]
    a_ab = [jnp.where(strict_c, gram_b[gi][:L], 0.0) for gi in R]
    a_rb = [jnp.where(incl_c, gram_b[gi][L:], 0.0) for gi in R]
    causal_2l = jnp.concatenate([strict_c, incl_c], axis=0)
    a_ak_rk = [jnp.where(causal_2l, gram_k[gi], 0.0) for gi in R]
    from_v = [compact_dot(a_ak_rk[gi], v_e[gi]) for gi in R]

    t_c = [eye_c + a_ab[gi] for gi in R]
    p_bd = [expand(a_ab[gi]) for gi in R]
    p_c = [compact_dot(a_ab[gi], p_bd[gi]) for gi in R]
    p_bd = [expand(p_c[gi]) for gi in R]
    for _ in range(L.bit_length() - 3):
        tp = [compact_dot(jnp.concatenate([t_c[gi], p_c[gi]], axis=0), p_bd[gi]) for gi in R]
        t_c = [t_c[gi] + tp[gi][:L] for gi in R]
        p_c = [tp[gi][L:] for gi in R]
        p_bd = [expand(p_c[gi]) for gi in R]
    t_c = [t_c[gi] + compact_dot(t_c[gi], p_bd[gi]) for gi in R]

    s0b = [x.astype(BF16) for x in s0]
    from_state = [_dot_nt(ar[gi], s0b[gi]) for gi in R]
    rhs = [from_state[gi][:L] + from_v[gi][:L] for gi in R]
    u = [compact_dot(t_c[gi], expand(rhs[gi])) for gi in R]
    y = [from_state[gi][L:] + from_v[gi][L:] + compact_dot(a_rb[gi], expand(u[gi]))
         for gi in R]
    uv = [jnp.concatenate([u[gi], v[gi].astype(F32)], axis=0) for gi in R]
    bk = [jnp.concatenate([bt[gi], kt[gi]], axis=0) for gi in R]
    upd = [_dot(uv[gi].T, bk[gi]) for gi in R]
    for gi in R:
        y_ref[0, :, lanes[gi]] = y[gi]
        st_ref[gi] = (s0[gi] + jnp.where(seg_mask, upd[gi], 0.0)) * gam[gi]


def _rwkv_scan(ops, gamma, batch, seq_len, chunk=SCAN_CHUNK, heads=SCAN_HEADS,
               groups=SCAN_GROUPS):
    c = gamma.shape[-1]
    gn = heads * RWKV_HEAD
    lanes = groups * gn
    n_chunks = seq_len // chunk
    seq = pl.BlockSpec((1, chunk, lanes), lambda b, hg, t: (b, t, hg))
    gam = pl.BlockSpec((1, 1, lanes), lambda b, hg, t: (b * n_chunks + t, 0, hg))
    return pl.pallas_call(
        functools.partial(_scan_kernel, chunk=chunk, heads=heads, groups=groups),
        grid=(batch, c // lanes, n_chunks),
        in_specs=[seq] * 5 + [gam],
        out_specs=seq,
        out_shape=jax.ShapeDtypeStruct((batch, seq_len, c), F32),
        scratch_shapes=[pltpu.VMEM((groups, gn, gn), F32)],
        compiler_params=_params("parallel", "parallel", "arbitrary"),
        name="rwkv_scan",
    )(*[o.reshape(batch, seq_len, c) for o in ops], gamma.reshape(batch * n_chunks, 1, c))


def _gn_proj_ln_kernel(y_ref, bonus_ref, g_ref, gnw_ref, gnb_ref, ones_ref, w_ref, h_ref,
                       lnw_ref, lnb_ref, o_ref):
    ones = ones_ref[...]
    inv_n = 1.0 / RWKV_HEAD
    half = y_ref.shape[0] // 2
    halves = [slice(0, half), slice(half, 2 * half)]

    def gated(rs):
        z = []
        for c0 in range(0, y_ref.shape[1], SEG_LANES):
            cb = slice(c0, c0 + SEG_LANES)
            y = y_ref[rs, cb]
            mu = _seg_sum(y, ones) * inv_n
            d = y - mu
            var = _seg_sum(d * d, ones) * inv_n
            yn = d * lax.rsqrt(var + GN_EPS) * gnw_ref[:, cb] + gnb_ref[:, cb]
            z.append(((yn + bonus_ref[rs, cb]) * g_ref[rs, cb]).astype(BF16))
        return jnp.concatenate(z, axis=1)

    z = [gated(rs) for rs in halves]
    proj = [jnp.dot(zh, w_ref[...], preferred_element_type=F32) for zh in z]
    for idx, rs in enumerate(halves):
        o_ref[rs, :] = _layernorm(ALPHA * h_ref[rs, :] + proj[idx], lnw_ref[...], lnb_ref[...])


def _gn_proj_ln(y, bonus, g, gn_w, gn_b, w, h, ln_w, ln_b, tm=TM // 2):
    m, c = y.shape
    tile = pl.BlockSpec((tm, c), lambda i: (i, 0))
    vec = pl.BlockSpec((1, c), lambda i: (0, 0))
    return pl.pallas_call(
        _gn_proj_ln_kernel,
        grid=(m // tm,),
        in_specs=[tile, tile, tile, vec, vec,
                  pl.BlockSpec((SEG_LANES, SEG_LANES), lambda i: (0, 0)),
                  pl.BlockSpec((c, c), lambda i: (0, 0)), tile, vec, vec],
        out_specs=tile,
        out_shape=jax.ShapeDtypeStruct((m, c), F32),
        compiler_params=_params("parallel"),
        name="rwkv_gn_proj_ln",
    )(y, bonus, g, gn_w, gn_b, _seg_ones(), w, h, ln_w, ln_b)


def _proj_ln_kernel(z_ref, w_ref, h_ref, lnw_ref, lnb_ref, o_ref):
    half = z_ref.shape[0] // 2
    halves = [slice(0, half), slice(half, 2 * half)]
    y = [jnp.dot(z_ref[rs, :], w_ref[...], preferred_element_type=F32) for rs in halves]
    for idx, rs in enumerate(halves):
        o_ref[rs, :] = _layernorm(ALPHA * h_ref[rs, :] + y[idx], lnw_ref[...], lnb_ref[...])


def _proj_ln(z, w, h, ln_w, ln_b, tm=TM):
    m, kdim = z.shape
    c = w.shape[1]
    vec = pl.BlockSpec((1, c), lambda i: (0, 0))
    return pl.pallas_call(
        _proj_ln_kernel,
        grid=(m // tm,),
        in_specs=[pl.BlockSpec((tm, kdim), lambda i: (i, 0)),
                  pl.BlockSpec((kdim, c), lambda i: (0, 0)),
                  pl.BlockSpec((tm, c), lambda i: (i, 0)), vec, vec],
        out_specs=pl.BlockSpec((tm, c), lambda i: (i, 0)),
        out_shape=jax.ShapeDtypeStruct((m, c), F32),
        compiler_params=_params("parallel"),
        name="proj_residual_ln",
    )(z, w, h, ln_w, ln_b)


def _ffn_kernel(h_ref, wg_ref, wu_ref, wd_ref, lnw_ref, lnb_ref, o_ref):
    f = pl.program_id(1)

    @pl.when(f == 0)
    def _():
        o_ref[...] = jnp.zeros_like(o_ref)

    x = h_ref[...].astype(BF16)
    gate = jnp.dot(x, wg_ref[...], preferred_element_type=F32)
    up = jnp.dot(x, wu_ref[...], preferred_element_type=F32)
    act = (gate * jax.nn.sigmoid(gate)) * up
    o_ref[...] += jnp.dot(act.astype(BF16), wd_ref[...], preferred_element_type=F32)

    @pl.when(f == pl.num_programs(1) - 1)
    def _():
        o_ref[...] = _layernorm(ALPHA * h_ref[...] + o_ref[...], lnw_ref[...], lnb_ref[...])


def _ffn_ln(h, w_gate, w_up, w_down, layer, ln_w, ln_b, tm=2 * TM, tf=512):
    m, c = h.shape
    d_ff = w_gate.shape[2]
    vec = pl.BlockSpec((1, c), lambda i, f: (0, 0))
    return pl.pallas_call(
        _ffn_kernel,
        grid=(m // tm, d_ff // tf),
        in_specs=[pl.BlockSpec((tm, c), lambda i, f: (i, 0)),
                  pl.BlockSpec((None, c, tf), lambda i, f: (layer, 0, f)),
                  pl.BlockSpec((None, c, tf), lambda i, f: (layer, 0, f)),
                  pl.BlockSpec((None, tf, c), lambda i, f: (layer, f, 0)), vec, vec],
        out_specs=pl.BlockSpec((tm, c), lambda i, f: (i, 0)),
        out_shape=jax.ShapeDtypeStruct((m, c), F32),
        compiler_params=_params("parallel", "arbitrary", vmem_limit_bytes=FFN_VMEM_LIMIT_BYTES),
        name="swiglu_ffn_ln",
    )(h, w_gate, w_up, w_down, ln_w, ln_b)


def _kv_kernel(x_ref, wk_ref, wvt_ref, k_ref, mean_ref, vt_ref, xb_ref):
    @pl.when(pl.program_id(1) == 0)
    def _():
        xb_ref[...] = x_ref[...].astype(BF16)

    xb = xb_ref[...]
    y = jnp.dot(xb, wk_ref[...], preferred_element_type=F32)
    k_ref[...] = y.astype(k_ref.dtype)
    tm, tn = y.shape
    nblk = tm // MOBA_BLOCK
    mean_ref[0] = jnp.sum(y.reshape(nblk, MOBA_BLOCK, tn), axis=1) * (1.0 / MOBA_BLOCK)
    vt = lax.dot_general(wvt_ref[...], xb, (((1,), (1,)), ((), ())), preferred_element_type=F32)
    for blk in range(nblk):
        vt_ref[blk] = vt[:, blk * MOBA_BLOCK:(blk + 1) * MOBA_BLOCK].astype(vt_ref.dtype)


def _kv_proj(h, w_k, w_v_t, tm=2 * TM, tn=512):
    m, c = h.shape
    n = w_k.shape[1]
    nblk = tm // MOBA_BLOCK
    return pl.pallas_call(
        _kv_kernel,
        grid=(m // tm, n // tn),
        in_specs=[pl.BlockSpec((tm, c), lambda i, j: (i, 0)),
                  pl.BlockSpec((c, tn), lambda i, j: (0, j)),
                  pl.BlockSpec((tn, c), lambda i, j: (j, 0))],
        out_specs=[pl.BlockSpec((tm, tn), lambda i, j: (i, j)),
                   pl.BlockSpec((1, nblk, tn), lambda i, j: (i, 0, j)),
                   pl.BlockSpec((nblk, tn, MOBA_BLOCK), lambda i, j: (i, j, 0))],
        out_shape=[jax.ShapeDtypeStruct((m, n), BF16),
                   jax.ShapeDtypeStruct((m // tm, nblk, n), F32),
                   jax.ShapeDtypeStruct((m // MOBA_BLOCK, n, MOBA_BLOCK), BF16)],
        scratch_shapes=[pltpu.VMEM((tm, c), BF16)],
        compiler_params=_params("parallel", "arbitrary"),
        name="shared_kv_proj",
    )(h, w_k, w_v_t)


def _q_kernel(x_ref, w_ref, o_ref, xb_ref):
    @pl.when(pl.program_id(1) == 0)
    def _():
        xb_ref[...] = x_ref[...].astype(BF16)

    o_ref[...] = jnp.dot(xb_ref[...], w_ref[...], preferred_element_type=F32)


def _q_proj(h, w_q, tm=2 * TM, tn=1024):
    m, c = h.shape
    n = w_q.shape[1]
    return pl.pallas_call(
        _q_kernel,
        grid=(m // tm, n // tn),
        in_specs=[pl.BlockSpec((tm, c), lambda i, j: (i, 0)),
                  pl.BlockSpec((c, tn), lambda i, j: (0, j))],
        out_specs=pl.BlockSpec((tm, tn), lambda i, j: (i, j)),
        out_shape=jax.ShapeDtypeStruct((m, n), F32),
        scratch_shapes=[pltpu.VMEM((tm, c), BF16)],
        compiler_params=_params("parallel", "arbitrary"),
        name="moba_q_proj",
    )(h, w_q)


def _moba_kernel(q_ref, k_ref, vt_ref, km_ref, o_ref, sel_ref, *, n_blocks, top_k, heads):
    blk = pl.program_id(2)
    BLK, DH = MOBA_BLOCK, MOBA_HEAD_DIM
    start = pl.multiple_of(blk * BLK, BLK)
    ki = lax.broadcasted_iota(jnp.int32, (BLK, BLK), 0)
    qi = lax.broadcasted_iota(jnp.int32, (BLK, BLK), 1)
    bi = lax.broadcasted_iota(jnp.int32, (n_blocks, BLK), 0)
    cols = [slice(hh * DH, (hh + 1) * DH) for hh in range(heads)]

    R = range(heads)
    q = [q_ref[:, cols[h]] for h in R]
    qb = [(q[h] * DH ** -0.5).astype(BF16) for h in R]

    gate = [_dot_nt_split(km_ref[0, :, cols[h]], q[h]) for h in R]
    rank = [jnp.zeros((n_blocks, BLK), jnp.int32) for h in R]
    for mm in range(n_blocks):
        past = (mm < blk).astype(jnp.int32)
        for h in R:
            gm = gate[h][mm:mm + 1, :]
            ahead = (gm > gate[h]) | ((gm == gate[h]) & (mm < bi))
            rank[h] = rank[h] + jnp.where(ahead, 1, 0) * past
    for h in R:
        sel_ref[h] = jnp.where((bi < blk) & (rank[h] < top_k), 1.0, 0.0)

    s = [_dot_nt(k_ref[pl.ds(start, BLK), cols[h]], qb[h]) for h in R]
    s = [jnp.where(ki <= qi, s[h], NEG) for h in R]
    ones_rows = jnp.ones((ONES_ROWS, BLK), BF16)
    v_ext = lambda n, h: jnp.concatenate([vt_ref[n, cols[h], :], ones_rows], axis=0)
    m0 = [jnp.max(s[h], axis=0, keepdims=True) for h in R]
    p = [jnp.exp((s[h] - m0[h]).astype(BF16)) for h in R]
    acc0 = [jnp.dot(v_ext(blk, h), p[h], preferred_element_type=F32) for h in R]

    def body(n, carry):
        m_i, acc = carry
        off = pl.multiple_of(n * BLK, BLK)
        sn = [_dot_nt(k_ref[pl.ds(off, BLK), cols[h]], qb[h]) for h in R]
        sn = [jnp.where(sel_ref[h, pl.ds(n, 1), :] > 0.0, sn[h], NEG) for h in R]
        m_new = [jnp.maximum(m_i[h], jnp.max(sn[h], axis=0, keepdims=True)) for h in R]
        corr = [jnp.exp(m_i[h] - m_new[h]) for h in R]
        pn = [jnp.exp((sn[h] - m_new[h]).astype(BF16)) for h in R]
        pv = [jnp.dot(v_ext(n, h), pn[h], preferred_element_type=F32) for h in R]
        acc_new = [corr[h] * acc[h] + pv[h] for h in R]
        return m_new, acc_new

    _, acc_f = lax.fori_loop(0, blk, body, (m0, acc0))
    for h in R:
        out = acc_f[h][:DH] / acc_f[h][DH:DH + 1]
        o_ref[:, cols[h]] = out.T.astype(o_ref.dtype)


def _moba_attention(q, k, v_t, k_mean, batch, seq_len, heads=MOBA_HEADS_PER_STEP):
    m, hd = q.shape
    n_blocks = seq_len // MOBA_BLOCK
    top_k = max(1, min(MOBA_TOPK, n_blocks - 1))
    BLK, W = MOBA_BLOCK, heads * MOBA_HEAD_DIM
    return pl.pallas_call(
        functools.partial(_moba_kernel, n_blocks=n_blocks, top_k=top_k, heads=heads),
        grid=(batch, hd // W, n_blocks),
        in_specs=[pl.BlockSpec((BLK, W), lambda b, h, t: (b * n_blocks + t, h)),
                  pl.BlockSpec((seq_len, W), lambda b, h, t: (b, h)),
                  pl.BlockSpec((n_blocks, W, BLK), lambda b, h, t: (b, h, 0)),
                  pl.BlockSpec((1, n_blocks, W), lambda b, h, t: (b, 0, h))],
        out_specs=pl.BlockSpec((BLK, W), lambda b, h, t: (b * n_blocks + t, h)),
        out_shape=jax.ShapeDtypeStruct((m, hd), BF16),
        scratch_shapes=[pltpu.VMEM((heads, n_blocks, BLK), F32)],
        compiler_params=_params("parallel", "parallel", "arbitrary"),
        name="moba_attention",
    )(q, k, v_t, k_mean)


def _pad_lora(w_in, w_out):
    rank = w_in.shape[1]
    pad = -rank % LORA_PAD
    return (jnp.pad(w_in, ((0, 0), (0, pad))).astype(BF16),
            jnp.pad(w_out, ((0, pad), (0, 0))).astype(BF16))


def kernel(x, a_mix, a_w_r, a_w_k, a_w_v, a_w_o, a_w0, a_w1, a_w2, a_a0, a_a1, a_a2, a_g1, a_g2,
           a_k_k, a_k_a, a_r_k, a_lnx_w, a_lnx_b, kv_w_k, kv_w_v, b_w_q, b_w_o, ffn_w_gate,
           ffn_w_up, ffn_w_down, ln1_w, ln1_b, ln2_w, ln2_b):
    batch, seq_len, c = x.shape
    n_a = a_mix.shape[0]
    n_layers = ffn_w_gate.shape[0]
    assert seq_len % MOBA_BLOCK == 0 and seq_len % TM == 0 and c == D_MODEL
    h = x.reshape(batch * seq_len, c)
    row = lambda vec: vec.reshape(1, c)
    k_all = v_t = k_mean = None
    ffn_gate_b, ffn_up_b, ffn_down_b = (w.astype(BF16) for w in (ffn_w_gate, ffn_w_up, ffn_w_down))

    for layer in range(n_layers):
        if layer < n_a:
            i = layer
            w1, w2 = _pad_lora(a_w1[i], a_w2[i])
            a1, a2 = _pad_lora(a_a1[i], a_a2[i])
            g1, g2 = _pad_lora(a_g1[i], a_g2[i])
            *ops, bonus, g, gamma = _rwkv_front(
                h, a_mix[i], a_w_r[i].astype(BF16), a_w_k[i].astype(BF16), a_w_v[i].astype(BF16),
                w1, w2, row(a_w0[i]), a1, a2, row(a_a0[i]), g1, g2, row(a_k_k[i]),
                row(a_k_a[i]), row(a_r_k[i]), seq_len, SCAN_CHUNK)
            y = _rwkv_scan(ops, gamma, batch, seq_len).reshape(batch * seq_len, c)
            h = _gn_proj_ln(y, bonus, g, row(a_lnx_w[i]), row(a_lnx_b[i]),
                            a_w_o[i].astype(BF16), h, row(ln1_w[layer]), row(ln1_b[layer]))
        else:
            j = layer - n_a
            q = _q_proj(h, b_w_q[j].astype(BF16))
            z = _moba_attention(q, k_all, v_t, k_mean, batch, seq_len)
            h = _proj_ln(z, b_w_o[j].astype(BF16), h, row(ln1_w[layer]), row(ln1_b[layer]))
        h = _ffn_ln(h, ffn_gate_b, ffn_up_b, ffn_down_b, layer, row(ln2_w[layer]),
                    row(ln2_b[layer]))
        if layer == n_a - 1:
            k_all, k_mean, v_t = _kv_proj(h, kv_w_k.astype(BF16), kv_w_v.T.astype(BF16))
            k_mean = k_mean.reshape(batch, seq_len // MOBA_BLOCK, -1)
    return h.reshape(batch, seq_len, c)
```

```python
import functools

import jax
import jax.numpy as jnp
from jax import lax
from jax.experimental import pallas as pl
from jax.experimental.pallas import tpu as pltpu

D_MODEL = 2048
RWKV_HEAD = 64
MOBA_HEAD_DIM = 128
MOBA_HEADS = D_MODEL // MOBA_HEAD_DIM
MOBA_BLOCK = 256
MOBA_TOPK = 3
NEG = -1e30
GN_EPS = 64e-5
LN_EPS = 1e-5
DEPTH = 2
ALPHA = (2 * DEPTH) ** 0.25
LORA_PAD = 128
NEG_EXP_MINUS_HALF = -0.6065306597126334

VMEM_LIMIT_BYTES = 56 * 1024 * 1024
FFN_VMEM_LIMIT_BYTES = 60000 * 1024

TM = 512
SCAN_CHUNK = 64
SCAN_HEADS = 4
SCAN_GROUPS = 8
MOBA_HEADS_PER_STEP = 16
ONES_ROWS = 16
EW_ROWS, EW_LANES = 512, 512
SEG_LANES = 256

F32 = jnp.float32
BF16 = jnp.bfloat16
HIGHEST = lax.Precision.HIGHEST


def _params(*semantics, vmem_limit_bytes=VMEM_LIMIT_BYTES):
    return pltpu.CompilerParams(dimension_semantics=semantics, vmem_limit_bytes=vmem_limit_bytes)


def _dot(a, b):
    return jnp.dot(a.astype(BF16), b.astype(BF16), preferred_element_type=F32)


def _dot_nt(a, b):
    return lax.dot_general(a.astype(BF16), b.astype(BF16), (((1,), (1,)), ((), ())),
                           preferred_element_type=F32)


def _dot_nt_split(a, b):
    a_hi = a.astype(BF16).astype(F32)
    b_hi = b.astype(BF16)
    b_lo = (b - b_hi.astype(F32)).astype(BF16)
    rows = a.shape[0]
    both = _dot_nt(jnp.concatenate([a_hi, a - a_hi], axis=0), b_hi)
    return both[:rows] + both[rows:] + _dot_nt(a_hi, b_lo)


def _seg_sum(x, seg_ones, split=True):
    width = seg_ones.shape[0]
    parts = []
    for c0 in range(0, x.shape[1], width):
        xs = x[:, c0:c0 + width]
        hi = xs.astype(BF16)
        acc = jnp.dot(hi, seg_ones, preferred_element_type=F32)
        if split:
            lo = (xs - hi.astype(F32)).astype(BF16)
            acc = acc + jnp.dot(lo, seg_ones, preferred_element_type=F32)
        parts.append(acc)
    return parts[0] if len(parts) == 1 else jnp.concatenate(parts, axis=1)


def _seg_ones():
    head = jnp.arange(SEG_LANES) // RWKV_HEAD
    return (head[:, None] == head[None, :]).astype(BF16)


def _layernorm(t, w, b):
    mu = jnp.mean(t, axis=-1, keepdims=True)
    d = t - mu
    var = jnp.mean(d * d, axis=-1, keepdims=True)
    return d * lax.rsqrt(var + LN_EPS) * w + b


def _token_shift_delta(x_ref, prev_ref, seq_start):
    x = x_ref[...]
    rolled = pltpu.roll(x, 1, 0)
    prev_last = jnp.where(seq_start, 0.0, prev_ref[7:8, :])
    row = lax.broadcasted_iota(jnp.int32, x.shape, 0)
    x_prev = jnp.where(row == 0, prev_last, rolled)
    return x, x_prev - x


def _front_kernel(x_ref, prev_ref, mix_ref, wr_ref, wk_ref, wv_ref, w1_ref, w2_ref, w0_ref,
                  a1_ref, a2_ref, a0_ref, g1_ref, g2_ref, kk_ref, ka_ref, rk_ref, ones_ref,
                  at_ref, rt_ref, bt_ref, kt_ref, vb_ref, bonus_ref, g_ref,
                  gam_ref, xr_ref, xk_ref, xv_ref, hw_ref, ha_ref, hg_ref,
                  *, tiles_per_seq, chunk):
    @pl.when(pl.program_id(1) == 0)
    def _():
        seq_start = (pl.program_id(0) % tiles_per_seq) == 0
        x, xx = _token_shift_delta(x_ref, prev_ref, seq_start)
        mixed = lambda idx: x + xx * mix_ref[idx:idx + 1, :]
        xr_ref[...] = mixed(0).astype(BF16)
        xk_ref[...] = mixed(2).astype(BF16)
        xv_ref[...] = mixed(3).astype(BF16)
        hw_ref[...] = jnp.tanh(_dot(mixed(1), w1_ref[...])).astype(BF16)
        ha_ref[...] = _dot(mixed(4), a1_ref[...]).astype(BF16)
        hg_ref[...] = jax.nn.sigmoid(_dot(mixed(5), g1_ref[...])).astype(BF16)

    dot = functools.partial(jnp.dot, preferred_element_type=F32)
    rows, lanes = at_ref.shape
    n_chunks = rows // chunk
    blocks = [slice(c0, c0 + SEG_LANES) for c0 in range(0, lanes, SEG_LANES)]
    xr, xk, xv, hw, ha, hg = (ref[...] for ref in (xr_ref, xk_ref, xv_ref, hw_ref, ha_ref, hg_ref))
    r = [dot(xr, wr_ref[:, cb]) for cb in blocks]
    k = [dot(xk, wk_ref[:, cb]) for cb in blocks]
    v = [dot(xv, wv_ref[:, cb]) for cb in blocks]
    w_pre = [w0_ref[:, cb] + dot(hw, w2_ref[:, cb]) for cb in blocks]
    a_pre = [a0_ref[:, cb] + dot(ha, a2_ref[:, cb]) for cb in blocks]
    for cb in blocks:
        g_ref[:, cb] = dot(hg, g2_ref[:, cb])

    ones = ones_ref[...]
    pos = lax.broadcasted_iota(jnp.int32, (rows, SEG_LANES), 0) % chunk
    for idx, cb in enumerate(blocks):
        ld = jax.nn.sigmoid(w_pre[idx]) * NEG_EXP_MINUS_HALF
        a = jax.nn.sigmoid(a_pre[idx])
        kk = k[idx] * kk_ref[:, cb]
        kk = kk * lax.rsqrt(jnp.maximum(_seg_sum(kk * kk, ones), 1e-24))
        k_mod = k[idx] * (1.0 + (a - 1.0) * ka_ref[:, cb])
        b_vec = kk * a

        cs = ld
        step = 1
        while step < chunk:
            cs = cs + jnp.where(pos >= step, pltpu.roll(cs, step, 0), 0.0)
            step *= 2
        cs3 = cs.reshape(n_chunks, chunk, SEG_LANES)
        cs_last = cs3[:, chunk - 1:chunk, :]
        e_neg = jnp.exp(-cs)

        at_ref[:, cb] = (-kk * jnp.exp(cs - ld)).astype(BF16)
        rt_ref[:, cb] = (r[idx] * jnp.exp(cs)).astype(BF16)
        bt_ref[:, cb] = (b_vec * e_neg).astype(BF16)
        kt_ref[:, cb] = (k_mod * e_neg).astype(BF16)
        vb_ref[:, cb] = v[idx].astype(BF16)
        bonus_ref[:, cb] = _seg_sum(r[idx] * k_mod * rk_ref[:, cb], ones, split=False) * v[idx]
        gam_ref[:, cb] = jnp.exp(cs_last).reshape(n_chunks, SEG_LANES)


def _rwkv_front(h, mix, w_r, w_k, w_v, w1, w2, w0, a1, a2, a0, g1, g2, k_k, k_a, r_k,
                seq_len, chunk, rows=EW_ROWS, lanes=EW_LANES):
    m, c = h.shape
    full = lambda arr: pl.BlockSpec(arr.shape, lambda i, j: (0,) * arr.ndim)
    cols = lambda arr: pl.BlockSpec((arr.shape[0], lanes), lambda i, j: (0, j))
    tile = pl.BlockSpec((rows, lanes), lambda i, j: (i, j))
    ones = _seg_ones()
    bf = jax.ShapeDtypeStruct((m, c), BF16)
    f32 = jax.ShapeDtypeStruct((m, c), F32)
    return pl.pallas_call(
        functools.partial(_front_kernel, tiles_per_seq=seq_len // rows, chunk=chunk),
        grid=(m // rows, c // lanes),
        in_specs=[pl.BlockSpec((rows, c), lambda i, j: (i, 0)),
                  pl.BlockSpec((8, c), lambda i, j: (jnp.maximum(i * (rows // 8) - 1, 0), 0)),
                  full(mix), cols(w_r), cols(w_k), cols(w_v),
                  full(w1), cols(w2), cols(w0), full(a1), cols(a2), cols(a0),
                  full(g1), cols(g2), cols(k_k), cols(k_a), cols(r_k), full(ones)],
        out_specs=[tile] * 7 + [pl.BlockSpec((rows // chunk, lanes), lambda i, j: (i, j))],
        out_shape=[bf] * 5 + [f32, f32, jax.ShapeDtypeStruct((m // chunk, c), F32)],
        scratch_shapes=[pltpu.VMEM((rows, c), BF16)] * 3
        + [pltpu.VMEM((rows, w1.shape[1]), BF16), pltpu.VMEM((rows, a1.shape[1]), BF16),
           pltpu.VMEM((rows, g1.shape[1]), BF16)],
        compiler_params=_params("parallel", "arbitrary"),
        name="rwkv_front",
    )(h, h, mix, w_r, w_k, w_v, w1, w2, w0, a1, a2, a0, g1, g2, k_k, k_a, r_k, ones)


def _scan_kernel(at_ref, rt_ref, bt_ref, kt_ref, v_ref, gam_ref, y_ref, st_ref,
                 *, chunk, heads, groups):
    L, G, N = chunk, heads, RWKV_HEAD
    GN, GL = G * N, G * L
    assert L == N

    @pl.when(pl.program_id(2) == 0)
    def _():
        st_ref[...] = jnp.zeros_like(st_ref)

    seg_mask = (lax.broadcasted_iota(jnp.int32, (GN, GN), 0) // N
                == lax.broadcasted_iota(jnp.int32, (GN, GN), 1) // N)
    expand_mask = (lax.broadcasted_iota(jnp.int32, (GL, GN), 0) // L
                   == lax.broadcasted_iota(jnp.int32, (GL, GN), 1) // N)
    row_t = lax.broadcasted_iota(jnp.int32, (L, GN), 0)
    lane_s = lax.broadcasted_iota(jnp.int32, (L, GN), 1) % L
    strict_c = lane_s < row_t
    incl_c = lane_s <= row_t
    eye_c = (lane_s == row_t).astype(F32)

    def expand(x):
        xb = x.astype(BF16)
        return jnp.where(expand_mask, jnp.concatenate([xb] * G, axis=0), jnp.zeros((), BF16))

    def compact_dot(x, w_bd):
        return jnp.dot(x.astype(BF16), w_bd, preferred_element_type=F32)

    R = range(groups)
    lanes = [slice(gi * GN, (gi + 1) * GN) for gi in R]
    load = lambda ref: [ref[0, :, lanes[gi]] for gi in R]
    at, rt, bt, kt, v = (load(ref) for ref in (at_ref, rt_ref, bt_ref, kt_ref, v_ref))
    s0 = [st_ref[gi] for gi in R]
    gam = [gam_ref[0, :, lanes[gi]] for gi in R]

    ar = [jnp.concatenate([at[gi], rt[gi]], axis=0) for gi in R]
    bt_e = [expand(x) for x in bt]
    kt_e = [expand(x) for x in kt]
    v_e = [expand(x) for x in v]
    gram_b = [_dot_nt(ar[gi], bt_e[gi]) for gi in R]
    gram_k = [_dot_nt(ar[gi], kt_e[gi]) for gi in R]
    a_ab = [jnp.where(strict_c, gram_b[gi][:L], 0.0) for gi in R]
    a_rb = [jnp.where(incl_c, gram_b[gi][L:], 0.0) for gi in R]
    causal_2l = jnp.concatenate([strict_c, incl_c], axis=0)
    a_ak_rk = [jnp.where(causal_2l, gram_k[gi], 0.0) for gi in R]
    from_v = [compact_dot(a_ak_rk[gi], v_e[gi]) for gi in R]

    t_c = [eye_c + a_ab[gi] for gi in R]
    p_bd = [expand(a_ab[gi]) for gi in R]
    p_c = [compact_dot(a_ab[gi], p_bd[gi]) for gi in R]
    p_bd = [expand(p_c[gi]) for gi in R]
    for _ in range(L.bit_length() - 3):
        tp = [compact_dot(jnp.concatenate([t_c[gi], p_c[gi]], axis=0), p_bd[gi]) for gi in R]
        t_c = [t_c[gi] + tp[gi][:L] for gi in R]
        p_c = [tp[gi][L:] for gi in R]
        p_bd = [expand(p_c[gi]) for gi in R]
    t_c = [t_c[gi] + compact_dot(t_c[gi], p_bd[gi]) for gi in R]

    s0b = [x.astype(BF16) for x in s0]
    from_state = [_dot_nt(ar[gi], s0b[gi]) for gi in R]
    rhs = [from_state[gi][:L] + from_v[gi][:L] for gi in R]
    u = [compact_dot(t_c[gi], expand(rhs[gi])) for gi in R]
    y = [from_state[gi][L:] + from_v[gi][L:] + compact_dot(a_rb[gi], expand(u[gi]))
         for gi in R]
    uv = [jnp.concatenate([u[gi], v[gi].astype(F32)], axis=0) for gi in R]
    bk = [jnp.concatenate([bt[gi], kt[gi]], axis=0) for gi in R]
    upd = [_dot(uv[gi].T, bk[gi]) for gi in R]
    for gi in R:
        y_ref[0, :, lanes[gi]] = y[gi]
        st_ref[gi] = (s0[gi] + jnp.where(seg_mask, upd[gi], 0.0)) * gam[gi]


def _rwkv_scan(ops, gamma, batch, seq_len, chunk=SCAN_CHUNK, heads=SCAN_HEADS,
               groups=SCAN_GROUPS):
    c = gamma.shape[-1]
    gn = heads * RWKV_HEAD
    lanes = groups * gn
    n_chunks = seq_len // chunk
    seq = pl.BlockSpec((1, chunk, lanes), lambda b, hg, t: (b, t, hg))
    gam = pl.BlockSpec((1, 1, lanes), lambda b, hg, t: (b * n_chunks + t, 0, hg))
    return pl.pallas_call(
        functools.partial(_scan_kernel, chunk=chunk, heads=heads, groups=groups),
        grid=(batch, c // lanes, n_chunks),
        in_specs=[seq] * 5 + [gam],
        out_specs=seq,
        out_shape=jax.ShapeDtypeStruct((batch, seq_len, c), F32),
        scratch_shapes=[pltpu.VMEM((groups, gn, gn), F32)],
        compiler_params=_params("parallel", "parallel", "arbitrary"),
        name="rwkv_scan",
    )(*[o.reshape(batch, seq_len, c) for o in ops], gamma.reshape(batch * n_chunks, 1, c))


def _gn_proj_ln_kernel(y_ref, bonus_ref, g_ref, gnw_ref, gnb_ref, ones_ref, w_ref, h_ref,
                       lnw_ref, lnb_ref, o_ref):
    ones = ones_ref[...]
    inv_n = 1.0 / RWKV_HEAD
    half = y_ref.shape[0] // 2
    halves = [slice(0, half), slice(half, 2 * half)]

    def gated(rs):
        z = []
        for c0 in range(0, y_ref.shape[1], SEG_LANES):
            cb = slice(c0, c0 + SEG_LANES)
            y = y_ref[rs, cb]
            mu = _seg_sum(y, ones) * inv_n
            d = y - mu
            var = _seg_sum(d * d, ones) * inv_n
            yn = d * lax.rsqrt(var + GN_EPS) * gnw_ref[:, cb] + gnb_ref[:, cb]
            z.append(((yn + bonus_ref[rs, cb]) * g_ref[rs, cb]).astype(BF16))
        return jnp.concatenate(z, axis=1)

    z = [gated(rs) for rs in halves]
    proj = [jnp.dot(zh, w_ref[...], preferred_element_type=F32) for zh in z]
    for idx, rs in enumerate(halves):
        o_ref[rs, :] = _layernorm(ALPHA * h_ref[rs, :] + proj[idx], lnw_ref[...], lnb_ref[...])


def _gn_proj_ln(y, bonus, g, gn_w, gn_b, w, h, ln_w, ln_b, tm=TM // 2):
    m, c = y.shape
    tile = pl.BlockSpec((tm, c), lambda i: (i, 0))
    vec = pl.BlockSpec((1, c), lambda i: (0, 0))
    return pl.pallas_call(
        _gn_proj_ln_kernel,
        grid=(m // tm,),
        in_specs=[tile, tile, tile, vec, vec,
                  pl.BlockSpec((SEG_LANES, SEG_LANES), lambda i: (0, 0)),
                  pl.BlockSpec((c, c), lambda i: (0, 0)), tile, vec, vec],
        out_specs=tile,
        out_shape=jax.ShapeDtypeStruct((m, c), F32),
        compiler_params=_params("parallel"),
        name="rwkv_gn_proj_ln",
    )(y, bonus, g, gn_w, gn_b, _seg_ones(), w, h, ln_w, ln_b)


def _proj_ln_kernel(z_ref, w_ref, h_ref, lnw_ref, lnb_ref, o_ref):
    half = z_ref.shape[0] // 2
    halves = [slice(0, half), slice(half, 2 * half)]
    y = [jnp.dot(z_ref[rs, :], w_ref[...], preferred_element_type=F32) for rs in halves]
    for idx, rs in enumerate(halves):
        o_ref[rs, :] = _layernorm(ALPHA * h_ref[rs, :] + y[idx], lnw_ref[...], lnb_ref[...])


def _proj_ln(z, w, h, ln_w, ln_b, tm=TM):
    m, kdim = z.shape
    c = w.shape[1]
    vec = pl.BlockSpec((1, c), lambda i: (0, 0))
    return pl.pallas_call(
        _proj_ln_kernel,
        grid=(m // tm,),
        in_specs=[pl.BlockSpec((tm, kdim), lambda i: (i, 0)),
                  pl.BlockSpec((kdim, c), lambda i: (0, 0)),
                  pl.BlockSpec((tm, c), lambda i: (i, 0)), vec, vec],
        out_specs=pl.BlockSpec((tm, c), lambda i: (i, 0)),
        out_shape=jax.ShapeDtypeStruct((m, c), F32),
        compiler_params=_params("parallel"),
        name="proj_residual_ln",
    )(z, w, h, ln_w, ln_b)


def _ffn_kernel(h_ref, wg_ref, wu_ref, wd_ref, lnw_ref, lnb_ref, o_ref):
    f = pl.program_id(1)

    @pl.when(f == 0)
    def _():
        o_ref[...] = jnp.zeros_like(o_ref)

    x = h_ref[...].astype(BF16)
    gate = jnp.dot(x, wg_ref[...].astype(BF16), preferred_element_type=F32)
    up = jnp.dot(x, wu_ref[...].astype(BF16), preferred_element_type=F32)
    act = (gate * jax.nn.sigmoid(gate)) * up
    o_ref[...] += jnp.dot(act.astype(BF16), wd_ref[...].astype(BF16),
                          preferred_element_type=F32)

    @pl.when(f == pl.num_programs(1) - 1)
    def _():
        o_ref[...] = _layernorm(ALPHA * h_ref[...] + o_ref[...], lnw_ref[...], lnb_ref[...])


def _ffn_ln(h, w_gate, w_up, w_down, layer, ln_w, ln_b, tm=2 * TM, tf=256):
    m, c = h.shape
    d_ff = w_gate.shape[2]
    vec = pl.BlockSpec((1, c), lambda i, f: (0, 0))
    return pl.pallas_call(
        _ffn_kernel,
        grid=(m // tm, d_ff // tf),
        in_specs=[pl.BlockSpec((tm, c), lambda i, f: (i, 0)),
                  pl.BlockSpec((None, c, tf), lambda i, f: (layer, 0, f)),
                  pl.BlockSpec((None, c, tf), lambda i, f: (layer, 0, f)),
                  pl.BlockSpec((None, tf, c), lambda i, f: (layer, f, 0)), vec, vec],
        out_specs=pl.BlockSpec((tm, c), lambda i, f: (i, 0)),
        out_shape=jax.ShapeDtypeStruct((m, c), F32),
        compiler_params=_params("parallel", "arbitrary", vmem_limit_bytes=FFN_VMEM_LIMIT_BYTES),
        name="swiglu_ffn_ln",
    )(h, w_gate, w_up, w_down, ln_w, ln_b)


def _kv_kernel(x_ref, wk_ref, wvt_ref, k_ref, mean_ref, vt_ref, xb_ref):
    @pl.when(pl.program_id(1) == 0)
    def _():
        xb_ref[...] = x_ref[...].astype(BF16)

    xb = xb_ref[...]
    y = jnp.dot(xb, wk_ref[...], preferred_element_type=F32)
    k_ref[...] = y.astype(k_ref.dtype)
    tm, tn = y.shape
    nblk = tm // MOBA_BLOCK
    mean_ref[0] = jnp.sum(y.reshape(nblk, MOBA_BLOCK, tn), axis=1) * (1.0 / MOBA_BLOCK)
    vt = lax.dot_general(wvt_ref[...], xb, (((1,), (1,)), ((), ())), preferred_element_type=F32)
    for blk in range(nblk):
        vt_ref[blk] = vt[:, blk * MOBA_BLOCK:(blk + 1) * MOBA_BLOCK].astype(vt_ref.dtype)


def _kv_proj(h, w_k, w_v_t, tm=2 * TM, tn=512):
    m, c = h.shape
    n = w_k.shape[1]
    nblk = tm // MOBA_BLOCK
    return pl.pallas_call(
        _kv_kernel,
        grid=(m // tm, n // tn),
        in_specs=[pl.BlockSpec((tm, c), lambda i, j: (i, 0)),
                  pl.BlockSpec((c, tn), lambda i, j: (0, j)),
                  pl.BlockSpec((tn, c), lambda i, j: (j, 0))],
        out_specs=[pl.BlockSpec((tm, tn), lambda i, j: (i, j)),
                   pl.BlockSpec((1, nblk, tn), lambda i, j: (i, 0, j)),
                   pl.BlockSpec((nblk, tn, MOBA_BLOCK), lambda i, j: (i, j, 0))],
        out_shape=[jax.ShapeDtypeStruct((m, n), BF16),
                   jax.ShapeDtypeStruct((m // tm, nblk, n), F32),
                   jax.ShapeDtypeStruct((m // MOBA_BLOCK, n, MOBA_BLOCK), BF16)],
        scratch_shapes=[pltpu.VMEM((tm, c), BF16)],
        compiler_params=_params("parallel", "arbitrary"),
        name="shared_kv_proj",
    )(h, w_k, w_v_t)


def _q_kernel(x_ref, w_ref, o_ref, xb_ref):
    @pl.when(pl.program_id(1) == 0)
    def _():
        xb_ref[...] = x_ref[...].astype(BF16)

    o_ref[...] = jnp.dot(xb_ref[...], w_ref[...], preferred_element_type=F32)


def _q_proj(h, w_q, tm=2 * TM, tn=1024):
    m, c = h.shape
    n = w_q.shape[1]
    return pl.pallas_call(
        _q_kernel,
        grid=(m // tm, n // tn),
        in_specs=[pl.BlockSpec((tm, c), lambda i, j: (i, 0)),
                  pl.BlockSpec((c, tn), lambda i, j: (0, j))],
        out_specs=pl.BlockSpec((tm, tn), lambda i, j: (i, j)),
        out_shape=jax.ShapeDtypeStruct((m, n), F32),
        scratch_shapes=[pltpu.VMEM((tm, c), BF16)],
        compiler_params=_params("parallel", "arbitrary"),
        name="moba_q_proj",
    )(h, w_q)


def _moba_kernel(q_ref, k_ref, vt_ref, km_ref, o_ref, sel_ref, *, n_blocks, top_k, heads):
    blk = pl.program_id(2)
    BLK, DH = MOBA_BLOCK, MOBA_HEAD_DIM
    start = pl.multiple_of(blk * BLK, BLK)
    ki = lax.broadcasted_iota(jnp.int32, (BLK, BLK), 0)
    qi = lax.broadcasted_iota(jnp.int32, (BLK, BLK), 1)
    bi = lax.broadcasted_iota(jnp.int32, (n_blocks, BLK), 0)
    cols = [slice(hh * DH, (hh + 1) * DH) for hh in range(heads)]

    R = range(heads)
    q = [q_ref[:, cols[h]] for h in R]
    qb = [(q[h] * DH ** -0.5).astype(BF16) for h in R]

    gate = [_dot_nt_split(km_ref[0, :, cols[h]], q[h]) for h in R]
    rank = [jnp.zeros((n_blocks, BLK), jnp.int32) for h in R]
    for mm in range(n_blocks):
        past = (mm < blk).astype(jnp.int32)
        for h in R:
            gm = gate[h][mm:mm + 1, :]
            ahead = (gm > gate[h]) | ((gm == gate[h]) & (mm < bi))
            rank[h] = rank[h] + jnp.where(ahead, 1, 0) * past
    for h in R:
        sel_ref[h] = jnp.where((bi < blk) & (rank[h] < top_k), 1.0, 0.0)

    s = [_dot_nt(k_ref[pl.ds(start, BLK), cols[h]], qb[h]) for h in R]
    s = [jnp.where(ki <= qi, s[h], NEG) for h in R]
    ones_rows = jnp.ones((ONES_ROWS, BLK), BF16)
    v_ext = lambda n, h: jnp.concatenate([vt_ref[n, cols[h], :], ones_rows], axis=0)
    m0 = [jnp.max(s[h], axis=0, keepdims=True) for h in R]
    p = [jnp.exp((s[h] - m0[h]).astype(BF16)) for h in R]
    acc0 = [jnp.dot(v_ext(blk, h), p[h], preferred_element_type=F32) for h in R]

    def body(n, carry):
        m_i, acc = carry
        off = pl.multiple_of(n * BLK, BLK)
        sn = [_dot_nt(k_ref[pl.ds(off, BLK), cols[h]], qb[h]) for h in R]
        sn = [jnp.where(sel_ref[h, pl.ds(n, 1), :] > 0.0, sn[h], NEG) for h in R]
        m_new = [jnp.maximum(m_i[h], jnp.max(sn[h], axis=0, keepdims=True)) for h in R]
        corr = [jnp.exp(m_i[h] - m_new[h]) for h in R]
        pn = [jnp.exp((sn[h] - m_new[h]).astype(BF16)) for h in R]
        pv = [jnp.dot(v_ext(n, h), pn[h], preferred_element_type=F32) for h in R]
        acc_new = [corr[h] * acc[h] + pv[h] for h in R]
        return m_new, acc_new

    _, acc_f = lax.fori_loop(0, blk, body, (m0, acc0))
    for h in R:
        out = acc_f[h][:DH] / acc_f[h][DH:DH + 1]
        o_ref[:, cols[h]] = out.T.astype(o_ref.dtype)


def _moba_attention(q, k, v_t, k_mean, batch, seq_len, heads=MOBA_HEADS_PER_STEP):
    m, hd = q.shape
    n_blocks = seq_len // MOBA_BLOCK
    top_k = max(1, min(MOBA_TOPK, n_blocks - 1))
    BLK, W = MOBA_BLOCK, heads * MOBA_HEAD_DIM
    return pl.pallas_call(
        functools.partial(_moba_kernel, n_blocks=n_blocks, top_k=top_k, heads=heads),
        grid=(batch, hd // W, n_blocks),
        in_specs=[pl.BlockSpec((BLK, W), lambda b, h, t: (b * n_blocks + t, h)),
                  pl.BlockSpec((seq_len, W), lambda b, h, t: (b, h)),
                  pl.BlockSpec((n_blocks, W, BLK), lambda b, h, t: (b, h, 0)),
                  pl.BlockSpec((1, n_blocks, W), lambda b, h, t: (b, 0, h))],
        out_specs=pl.BlockSpec((BLK, W), lambda b, h, t: (b * n_blocks + t, h)),
        out_shape=jax.ShapeDtypeStruct((m, hd), BF16),
        scratch_shapes=[pltpu.VMEM((heads, n_blocks, BLK), F32)],
        compiler_params=_params("parallel", "parallel", "arbitrary"),
        name="moba_attention",
    )(q, k, v_t, k_mean)


def _pad_lora(w_in, w_out):
    rank = w_in.shape[1]
    pad = -rank % LORA_PAD
    return (jnp.pad(w_in, ((0, 0), (0, pad))).astype(BF16),
            jnp.pad(w_out, ((0, pad), (0, 0))).astype(BF16))


def kernel(x, a_mix, a_w_r, a_w_k, a_w_v, a_w_o, a_w0, a_w1, a_w2, a_a0, a_a1, a_a2, a_g1, a_g2,
           a_k_k, a_k_a, a_r_k, a_lnx_w, a_lnx_b, kv_w_k, kv_w_v, b_w_q, b_w_o, ffn_w_gate,
           ffn_w_up, ffn_w_down, ln1_w, ln1_b, ln2_w, ln2_b):
    batch, seq_len, c = x.shape
    n_a = a_mix.shape[0]
    n_layers = ffn_w_gate.shape[0]
    assert seq_len % MOBA_BLOCK == 0 and seq_len % TM == 0 and c == D_MODEL
    h = x.reshape(batch * seq_len, c)
    row = lambda vec: vec.reshape(1, c)
    k_all = v_t = k_mean = None

    for layer in range(n_layers):
        if layer < n_a:
            i = layer
            w1, w2 = _pad_lora(a_w1[i], a_w2[i])
            a1, a2 = _pad_lora(a_a1[i], a_a2[i])
            g1, g2 = _pad_lora(a_g1[i], a_g2[i])
            *ops, bonus, g, gamma = _rwkv_front(
                h, a_mix[i], a_w_r[i].astype(BF16), a_w_k[i].astype(BF16), a_w_v[i].astype(BF16),
                w1, w2, row(a_w0[i]), a1, a2, row(a_a0[i]), g1, g2, row(a_k_k[i]),
                row(a_k_a[i]), row(a_r_k[i]), seq_len, SCAN_CHUNK)
            y = _rwkv_scan(ops, gamma, batch, seq_len).reshape(batch * seq_len, c)
            h = _gn_proj_ln(y, bonus, g, row(a_lnx_w[i]), row(a_lnx_b[i]),
                            a_w_o[i].astype(BF16), h, row(ln1_w[layer]), row(ln1_b[layer]))
        else:
            j = layer - n_a
            q = _q_proj(h, b_w_q[j].astype(BF16))
            z = _moba_attention(q, k_all, v_t, k_mean, batch, seq_len)
            h = _proj_ln(z, b_w_o[j].astype(BF16), h, row(ln1_w[layer]), row(ln1_b[layer]))
        h = _ffn_ln(h, ffn_w_gate, ffn_w_up, ffn_w_down, layer, row(ln2_w[layer]),
                    row(ln2_b[layer]))
        if layer == n_a - 1:
            k_all, k_mean, v_t = _kv_proj(h, kv_w_k.astype(BF16), kv_w_v.T.astype(BF16))
            k_mean = k_mean.reshape(batch, seq_len // MOBA_BLOCK, -1)
    return h.reshape(batch, seq_len, c)
```

```python
import functools

import jax
import jax.numpy as jnp
from jax import lax
from jax.experimental import pallas as pl
from jax.experimental.pallas import tpu as pltpu

D_MODEL = 2048
RWKV_HEAD = 64
MOBA_HEAD_DIM = 128
MOBA_HEADS = D_MODEL // MOBA_HEAD_DIM
MOBA_BLOCK = 256
MOBA_TOPK = 3
NEG = -1e30
GN_EPS = 64e-5
LN_EPS = 1e-5
DEPTH = 2
ALPHA = (2 * DEPTH) ** 0.25
LORA_PAD = 128
NEG_EXP_MINUS_HALF = -0.6065306597126334

VMEM_LIMIT_BYTES = 56 * 1024 * 1024
FFN_VMEM_LIMIT_BYTES = 60000 * 1024

TM = 512
SCAN_CHUNK = 64
SCAN_HEADS = 4
SCAN_GROUPS = 8
MOBA_HEADS_PER_STEP = 16
ONES_ROWS = 16
EW_ROWS, EW_LANES = 512, 512
FRONT_SLAB = 128
SEG_LANES = 256

F32 = jnp.float32
BF16 = jnp.bfloat16
HIGHEST = lax.Precision.HIGHEST


def _params(*semantics, vmem_limit_bytes=VMEM_LIMIT_BYTES):
    return pltpu.CompilerParams(dimension_semantics=semantics, vmem_limit_bytes=vmem_limit_bytes)


def _dot(a, b):
    return jnp.dot(a.astype(BF16), b.astype(BF16), preferred_element_type=F32)


def _dot_nt(a, b):
    return lax.dot_general(a.astype(BF16), b.astype(BF16), (((1,), (1,)), ((), ())),
                           preferred_element_type=F32)


def _dot_nt_split(a, b):
    a_hi = a.astype(BF16).astype(F32)
    b_hi = b.astype(BF16)
    b_lo = (b - b_hi.astype(F32)).astype(BF16)
    rows = a.shape[0]
    both = _dot_nt(jnp.concatenate([a_hi, a - a_hi], axis=0), b_hi)
    return both[:rows] + both[rows:] + _dot_nt(a_hi, b_lo)


def _seg_sum(x, seg_ones, split=True):
    width = seg_ones.shape[0]
    parts = []
    for c0 in range(0, x.shape[1], width):
        xs = x[:, c0:c0 + width]
        hi = xs.astype(BF16)
        acc = jnp.dot(hi, seg_ones, preferred_element_type=F32)
        if split:
            lo = (xs - hi.astype(F32)).astype(BF16)
            acc = acc + jnp.dot(lo, seg_ones, preferred_element_type=F32)
        parts.append(acc)
    return parts[0] if len(parts) == 1 else jnp.concatenate(parts, axis=1)


def _seg_ones():
    head = jnp.arange(SEG_LANES) // RWKV_HEAD
    return (head[:, None] == head[None, :]).astype(BF16)


def _layernorm(t, w, b):
    mu = jnp.mean(t, axis=-1, keepdims=True)
    d = t - mu
    var = jnp.mean(d * d, axis=-1, keepdims=True)
    return d * lax.rsqrt(var + LN_EPS) * w + b


def _token_shift_delta(x_ref, prev_ref, seq_start):
    x = x_ref[...]
    rolled = pltpu.roll(x, 1, 0)
    prev_last = jnp.where(seq_start, 0.0, prev_ref[7:8, :])
    row = lax.broadcasted_iota(jnp.int32, x.shape, 0)
    x_prev = jnp.where(row == 0, prev_last, rolled)
    return x, x_prev - x


def _front_kernel(x_ref, prev_ref, mix_ref, wr_ref, wk_ref, wv_ref, w1_ref, w2_ref, w0_ref,
                  a1_ref, a2_ref, a0_ref, g1_ref, g2_ref, kk_ref, ka_ref, rk_ref, ones_ref,
                  at_ref, rt_ref, bt_ref, kt_ref, vb_ref, bonus_ref, g_ref, gam_ref,
                  xr_ref, xk_ref, xv_ref, hw_ref, ha_ref, hg_ref, raw0_ref, raw1_ref,
                  *, tiles_per_seq, chunk, n_col):
    j = pl.program_id(1)
    rows, lanes = at_ref.shape
    n_chunks = rows // chunk
    blocks = [slice(c0, c0 + SEG_LANES) for c0 in range(0, lanes, SEG_LANES)]
    dot = functools.partial(jnp.dot, preferred_element_type=F32)

    def mix_steps():
        seq_start = (pl.program_id(0) % tiles_per_seq) == 0
        x, xx = _token_shift_delta(x_ref, prev_ref, seq_start)
        mixed = lambda idx: x + xx * mix_ref[idx:idx + 1, :]

        def fill(ref, make):
            def run():
                ref[...] = make().astype(BF16)
            return run

        return [fill(xr_ref, lambda: mixed(0)),
                fill(hw_ref, lambda: jnp.tanh(_dot(mixed(1), w1_ref[...]))),
                fill(xk_ref, lambda: mixed(2)),
                fill(ha_ref, lambda: _dot(mixed(4), a1_ref[...])),
                fill(xv_ref, lambda: mixed(3)),
                fill(hg_ref, lambda: jax.nn.sigmoid(_dot(mixed(5), g1_ref[...])))]

    def project_steps(raw_ref):
        def store(row, x_ref, w_ref, bias_ref, cb):
            def run():
                y = dot(x_ref[...], w_ref[:, cb])
                raw_ref[row, :, cb] = y if bias_ref is None else bias_ref[:, cb] + y
            return run

        def gate(cb):
            def run():
                g_ref[:, cb] = dot(hg_ref[...], g2_ref[:, cb])
            return run

        steps = []
        for cb in blocks:
            steps += [store(0, xr_ref, wr_ref, None, cb), store(3, hw_ref, w2_ref, w0_ref, cb),
                      store(1, xk_ref, wk_ref, None, cb), store(4, ha_ref, a2_ref, a0_ref, cb),
                      store(2, xv_ref, wv_ref, None, cb), gate(cb)]
        return steps

    def finish_steps(raw_ref):
        def piece(cb, rs, cs_rows):
            def run():
                ones = ones_ref[...]
                slab = rs.stop - rs.start
                pos = lax.broadcasted_iota(jnp.int32, (slab, SEG_LANES), 0) % chunk
                r, k, v = raw_ref[0, rs, cb], raw_ref[1, rs, cb], raw_ref[2, rs, cb]
                ld = jax.nn.sigmoid(raw_ref[3, rs, cb]) * NEG_EXP_MINUS_HALF
                a = jax.nn.sigmoid(raw_ref[4, rs, cb])
                kk = k * kk_ref[:, cb]
                kk = kk * lax.rsqrt(jnp.maximum(_seg_sum(kk * kk, ones), 1e-24))
                k_mod = k * (1.0 + (a - 1.0) * ka_ref[:, cb])
                b_vec = kk * a

                cs = ld
                step = 1
                while step < chunk:
                    cs = cs + jnp.where(pos >= step, pltpu.roll(cs, step, 0), 0.0)
                    step *= 2
                cs_last = cs.reshape(slab // chunk, chunk, SEG_LANES)[:, chunk - 1:chunk, :]
                e_neg = jnp.exp(-cs)

                at_ref[rs, cb] = (-kk * jnp.exp(cs - ld)).astype(BF16)
                rt_ref[rs, cb] = (r * jnp.exp(cs)).astype(BF16)
                bt_ref[rs, cb] = (b_vec * e_neg).astype(BF16)
                kt_ref[rs, cb] = (k_mod * e_neg).astype(BF16)
                vb_ref[rs, cb] = v.astype(BF16)
                bonus_ref[rs, cb] = _seg_sum(r * k_mod * rk_ref[:, cb], ones, split=False) * v
                gam_ref[cs_rows, cb] = jnp.exp(cs_last).reshape(slab // chunk, SEG_LANES)
            return run

        return [piece(cb, slice(r0, r0 + FRONT_SLAB), slice(r0 // chunk, (r0 + FRONT_SLAB) // chunk))
                for cb in blocks for r0 in range(0, rows, FRONT_SLAB)]

    def run_interleaved(first, second):
        done = 0
        for idx, step in enumerate(first):
            step()
            due = (idx + 1) * len(second) // len(first)
            for extra in second[done:due]:
                extra()
            done = due

    def run_all(steps):
        for step in steps:
            step()

    @pl.when(j == 0)
    def _():
        mixes, projections = mix_steps(), project_steps(raw0_ref)
        mixes[0]()
        for idx, projection in enumerate(projections):
            if idx + 1 < len(mixes):
                mixes[idx + 1]()
            projection()

    @pl.when((j > 0) & (j < n_col) & (j % 2 == 1))
    def _():
        run_interleaved(project_steps(raw1_ref), finish_steps(raw0_ref))

    @pl.when((j > 0) & (j < n_col) & (j % 2 == 0))
    def _():
        run_interleaved(project_steps(raw0_ref), finish_steps(raw1_ref))

    @pl.when(j == n_col)
    def _():
        run_all(finish_steps(raw1_ref if n_col % 2 == 0 else raw0_ref))


def _rwkv_front(h, mix, w_r, w_k, w_v, w1, w2, w0, a1, a2, a0, g1, g2, k_k, k_a, r_k,
                seq_len, chunk, rows=EW_ROWS, lanes=EW_LANES):
    m, c = h.shape
    n_col = c // lanes
    proj_col = lambda j: jnp.minimum(j, n_col - 1)
    done_col = lambda j: jnp.maximum(j - 1, 0)
    full = lambda arr: pl.BlockSpec(arr.shape, lambda i, j: (0,) * arr.ndim)
    cols_p = lambda arr: pl.BlockSpec((arr.shape[0], lanes), lambda i, j: (0, proj_col(j)))
    cols_d = lambda arr: pl.BlockSpec((arr.shape[0], lanes), lambda i, j: (0, done_col(j)))
    tile_d = pl.BlockSpec((rows, lanes), lambda i, j: (i, done_col(j)))
    ones = _seg_ones()
    bf = jax.ShapeDtypeStruct((m, c), BF16)
    f32 = jax.ShapeDtypeStruct((m, c), F32)
    return pl.pallas_call(
        functools.partial(_front_kernel, tiles_per_seq=seq_len // rows, chunk=chunk, n_col=n_col),
        grid=(m // rows, n_col + 1),
        in_specs=[pl.BlockSpec((rows, c), lambda i, j: (i, 0)),
                  pl.BlockSpec((8, c), lambda i, j: (jnp.maximum(i * (rows // 8) - 1, 0), 0)),
                  full(mix), cols_p(w_r), cols_p(w_k), cols_p(w_v),
                  full(w1), cols_p(w2), cols_p(w0), full(a1), cols_p(a2), cols_p(a0),
                  full(g1), cols_p(g2), cols_d(k_k), cols_d(k_a), cols_d(r_k), full(ones)],
        out_specs=[tile_d] * 6
        + [pl.BlockSpec((rows, lanes), lambda i, j: (i, proj_col(j))),
           pl.BlockSpec((rows // chunk, lanes), lambda i, j: (i, done_col(j)))],
        out_shape=[bf] * 5 + [f32, f32, jax.ShapeDtypeStruct((m // chunk, c), F32)],
        scratch_shapes=[pltpu.VMEM((rows, c), BF16)] * 3
        + [pltpu.VMEM((rows, w1.shape[1]), BF16), pltpu.VMEM((rows, a1.shape[1]), BF16),
           pltpu.VMEM((rows, g1.shape[1]), BF16)]
        + [pltpu.VMEM((5, rows, lanes), F32)] * 2,
        compiler_params=_params("parallel", "arbitrary"),
        name="rwkv_front",
    )(h, h, mix, w_r, w_k, w_v, w1, w2, w0, a1, a2, a0, g1, g2, k_k, k_a, r_k, ones)


def _scan_kernel(at_ref, rt_ref, bt_ref, kt_ref, v_ref, gam_ref, y_ref, st_ref,
                 *, chunk, heads, groups):
    L, G, N = chunk, heads, RWKV_HEAD
    GN, GL = G * N, G * L
    assert L == N

    @pl.when(pl.program_id(2) == 0)
    def _():
        st_ref[...] = jnp.zeros_like(st_ref)

    seg_mask = (lax.broadcasted_iota(jnp.int32, (GN, GN), 0) // N
                == lax.broadcasted_iota(jnp.int32, (GN, GN), 1) // N)
    expand_mask = (lax.broadcasted_iota(jnp.int32, (GL, GN), 0) // L
                   == lax.broadcasted_iota(jnp.int32, (GL, GN), 1) // N)
    row_t = lax.broadcasted_iota(jnp.int32, (L, GN), 0)
    lane_s = lax.broadcasted_iota(jnp.int32, (L, GN), 1) % L
    strict_c = lane_s < row_t
    incl_c = lane_s <= row_t
    eye_c = (lane_s == row_t).astype(F32)

    def expand(x):
        xb = x.astype(BF16)
        return jnp.where(expand_mask, jnp.concatenate([xb] * G, axis=0), jnp.zeros((), BF16))

    def compact_dot(x, w_bd):
        return jnp.dot(x.astype(BF16), w_bd, preferred_element_type=F32)

    R = range(groups)
    lanes = [slice(gi * GN, (gi + 1) * GN) for gi in R]
    load = lambda ref: [ref[0, :, lanes[gi]] for gi in R]
    at, rt, bt, kt, v = (load(ref) for ref in (at_ref, rt_ref, bt_ref, kt_ref, v_ref))
    s0 = [st_ref[gi] for gi in R]
    gam = [gam_ref[0, :, lanes[gi]] for gi in R]

    ar = [jnp.concatenate([at[gi], rt[gi]], axis=0) for gi in R]
    bt_e = [expand(x) for x in bt]
    kt_e = [expand(x) for x in kt]
    v_e = [expand(x) for x in v]
    gram_b = [_dot_nt(ar[gi], bt_e[gi]) for gi in R]
    gram_k = [_dot_nt(ar[gi], kt_e[gi]) for gi in R]
    a_ab = [jnp.where(strict_c, gram_b[gi][:L], 0.0) for gi in R]
    a_rb = [jnp.where(incl_c, gram_b[gi][L:], 0.0) for gi in R]
    causal_2l = jnp.concatenate([strict_c, incl_c], axis=0)
    a_ak_rk = [jnp.where(causal_2l, gram_k[gi], 0.0) for gi in R]
    from_v = [compact_dot(a_ak_rk[gi], v_e[gi]) for gi in R]

    t_c = [eye_c + a_ab[gi] for gi in R]
    p_bd = [expand(a_ab[gi]) for gi in R]
    p_c = [compact_dot(a_ab[gi], p_bd[gi]) for gi in R]
    p_bd = [expand(p_c[gi]) for gi in R]
    for _ in range(L.bit_length() - 3):
        tp = [compact_dot(jnp.concatenate([t_c[gi], p_c[gi]], axis=0), p_bd[gi]) for gi in R]
        t_c = [t_c[gi] + tp[gi][:L] for gi in R]
        p_c = [tp[gi][L:] for gi in R]
        p_bd = [expand(p_c[gi]) for gi in R]
    t_c = [t_c[gi] + compact_dot(t_c[gi], p_bd[gi]) for gi in R]

    s0b = [x.astype(BF16) for x in s0]
    from_state = [_dot_nt(ar[gi], s0b[gi]) for gi in R]
    rhs = [from_state[gi][:L] + from_v[gi][:L] for gi in R]
    u = [compact_dot(t_c[gi], expand(rhs[gi])) for gi in R]
    y = [from_state[gi][L:] + from_v[gi][L:] + compact_dot(a_rb[gi], expand(u[gi]))
         for gi in R]
    uv = [jnp.concatenate([u[gi], v[gi].astype(F32)], axis=0) for gi in R]
    bk = [jnp.concatenate([bt[gi], kt[gi]], axis=0) for gi in R]
    upd = [_dot(uv[gi].T, bk[gi]) for gi in R]
    for gi in R:
        y_ref[0, :, lanes[gi]] = y[gi]
        st_ref[gi] = (s0[gi] + jnp.where(seg_mask, upd[gi], 0.0)) * gam[gi]


def _rwkv_scan(ops, gamma, batch, seq_len, chunk=SCAN_CHUNK, heads=SCAN_HEADS,
               groups=SCAN_GROUPS):
    c = gamma.shape[-1]
    gn = heads * RWKV_HEAD
    lanes = groups * gn
    n_chunks = seq_len // chunk
    seq = pl.BlockSpec((1, chunk, lanes), lambda b, hg, t: (b, t, hg))
    gam = pl.BlockSpec((1, 1, lanes), lambda b, hg, t: (b * n_chunks + t, 0, hg))
    return pl.pallas_call(
        functools.partial(_scan_kernel, chunk=chunk, heads=heads, groups=groups),
        grid=(batch, c // lanes, n_chunks),
        in_specs=[seq] * 5 + [gam],
        out_specs=seq,
        out_shape=jax.ShapeDtypeStruct((batch, seq_len, c), F32),
        scratch_shapes=[pltpu.VMEM((groups, gn, gn), F32)],
        compiler_params=_params("parallel", "parallel", "arbitrary"),
        name="rwkv_scan",
    )(*[o.reshape(batch, seq_len, c) for o in ops], gamma.reshape(batch * n_chunks, 1, c))


def _gn_proj_ln_kernel(y_ref, bonus_ref, g_ref, gnw_ref, gnb_ref, ones_ref, w_ref, h_ref,
                       lnw_ref, lnb_ref, o_ref):
    ones = ones_ref[...]
    inv_n = 1.0 / RWKV_HEAD
    half = y_ref.shape[0] // 2
    halves = [slice(0, half), slice(half, 2 * half)]

    def gated(rs):
        z = []
        for c0 in range(0, y_ref.shape[1], SEG_LANES):
            cb = slice(c0, c0 + SEG_LANES)
            y = y_ref[rs, cb]
            mu = _seg_sum(y, ones) * inv_n
            d = y - mu
            var = _seg_sum(d * d, ones) * inv_n
            yn = d * lax.rsqrt(var + GN_EPS) * gnw_ref[:, cb] + gnb_ref[:, cb]
            z.append(((yn + bonus_ref[rs, cb]) * g_ref[rs, cb]).astype(BF16))
        return jnp.concatenate(z, axis=1)

    z = [gated(rs) for rs in halves]
    proj = [jnp.dot(zh, w_ref[...], preferred_element_type=F32) for zh in z]
    for idx, rs in enumerate(halves):
        o_ref[rs, :] = _layernorm(ALPHA * h_ref[rs, :] + proj[idx], lnw_ref[...], lnb_ref[...])


def _gn_proj_ln(y, bonus, g, gn_w, gn_b, w, h, ln_w, ln_b, tm=TM // 2):
    m, c = y.shape
    tile = pl.BlockSpec((tm, c), lambda i: (i, 0))
    vec = pl.BlockSpec((1, c), lambda i: (0, 0))
    return pl.pallas_call(
        _gn_proj_ln_kernel,
        grid=(m // tm,),
        in_specs=[tile, tile, tile, vec, vec,
                  pl.BlockSpec((SEG_LANES, SEG_LANES), lambda i: (0, 0)),
                  pl.BlockSpec((c, c), lambda i: (0, 0)), tile, vec, vec],
        out_specs=tile,
        out_shape=jax.ShapeDtypeStruct((m, c), F32),
        compiler_params=_params("parallel"),
        name="rwkv_gn_proj_ln",
    )(y, bonus, g, gn_w, gn_b, _seg_ones(), w, h, ln_w, ln_b)


def _proj_ln_kernel(z_ref, w_ref, h_ref, lnw_ref, lnb_ref, o_ref):
    half = z_ref.shape[0] // 2
    halves = [slice(0, half), slice(half, 2 * half)]
    y = [jnp.dot(z_ref[rs, :], w_ref[...], preferred_element_type=F32) for rs in halves]
    for idx, rs in enumerate(halves):
        o_ref[rs, :] = _layernorm(ALPHA * h_ref[rs, :] + y[idx], lnw_ref[...], lnb_ref[...])


def _proj_ln(z, w, h, ln_w, ln_b, tm=TM):
    m, kdim = z.shape
    c = w.shape[1]
    vec = pl.BlockSpec((1, c), lambda i: (0, 0))
    return pl.pallas_call(
        _proj_ln_kernel,
        grid=(m // tm,),
        in_specs=[pl.BlockSpec((tm, kdim), lambda i: (i, 0)),
                  pl.BlockSpec((kdim, c), lambda i: (0, 0)),
                  pl.BlockSpec((tm, c), lambda i: (i, 0)), vec, vec],
        out_specs=pl.BlockSpec((tm, c), lambda i: (i, 0)),
        out_shape=jax.ShapeDtypeStruct((m, c), F32),
        compiler_params=_params("parallel"),
        name="proj_residual_ln",
    )(z, w, h, ln_w, ln_b)


def _ffn_kernel(h_ref, wg_ref, wu_ref, wd_ref, lnw_ref, lnb_ref, o_ref):
    f = pl.program_id(1)

    @pl.when(f == 0)
    def _():
        o_ref[...] = jnp.zeros_like(o_ref)

    x = h_ref[...].astype(BF16)
    gate = jnp.dot(x, wg_ref[...].astype(BF16), preferred_element_type=F32)
    up = jnp.dot(x, wu_ref[...].astype(BF16), preferred_element_type=F32)
    act = (gate * jax.nn.sigmoid(gate)) * up
    o_ref[...] += jnp.dot(act.astype(BF16), wd_ref[...].astype(BF16),
                          preferred_element_type=F32)

    @pl.when(f == pl.num_programs(1) - 1)
    def _():
        o_ref[...] = _layernorm(ALPHA * h_ref[...] + o_ref[...], lnw_ref[...], lnb_ref[...])


def _ffn_ln(h, w_gate, w_up, w_down, layer, ln_w, ln_b, tm=2 * TM, tf=256):
    m, c = h.shape
    d_ff = w_gate.shape[2]
    vec = pl.BlockSpec((1, c), lambda i, f: (0, 0))
    return pl.pallas_call(
        _ffn_kernel,
        grid=(m // tm, d_ff // tf),
        in_specs=[pl.BlockSpec((tm, c), lambda i, f: (i, 0)),
                  pl.BlockSpec((None, c, tf), lambda i, f: (layer, 0, f)),
                  pl.BlockSpec((None, c, tf), lambda i, f: (layer, 0, f)),
                  pl.BlockSpec((None, tf, c), lambda i, f: (layer, f, 0)), vec, vec],
        out_specs=pl.BlockSpec((tm, c), lambda i, f: (i, 0)),
        out_shape=jax.ShapeDtypeStruct((m, c), F32),
        compiler_params=_params("parallel", "arbitrary", vmem_limit_bytes=FFN_VMEM_LIMIT_BYTES),
        name="swiglu_ffn_ln",
    )(h, w_gate, w_up, w_down, ln_w, ln_b)


def _kv_kernel(x_ref, wk_ref, wvt_ref, k_ref, mean_ref, vt_ref, xb_ref):
    @pl.when(pl.program_id(1) == 0)
    def _():
        xb_ref[...] = x_ref[...].astype(BF16)

    xb = xb_ref[...]
    y = jnp.dot(xb, wk_ref[...], preferred_element_type=F32)
    k_ref[...] = y.astype(k_ref.dtype)
    tm, tn = y.shape
    nblk = tm // MOBA_BLOCK
    mean_ref[0] = jnp.sum(y.reshape(nblk, MOBA_BLOCK, tn), axis=1) * (1.0 / MOBA_BLOCK)
    vt = lax.dot_general(wvt_ref[...], xb, (((1,), (1,)), ((), ())), preferred_element_type=F32)
    for blk in range(nblk):
        vt_ref[blk] = vt[:, blk * MOBA_BLOCK:(blk + 1) * MOBA_BLOCK].astype(vt_ref.dtype)


def _kv_proj(h, w_k, w_v_t, tm=2 * TM, tn=512):
    m, c = h.shape
    n = w_k.shape[1]
    nblk = tm // MOBA_BLOCK
    return pl.pallas_call(
        _kv_kernel,
        grid=(m // tm, n // tn),
        in_specs=[pl.BlockSpec((tm, c), lambda i, j: (i, 0)),
                  pl.BlockSpec((c, tn), lambda i, j: (0, j)),
                  pl.BlockSpec((tn, c), lambda i, j: (j, 0))],
        out_specs=[pl.BlockSpec((tm, tn), lambda i, j: (i, j)),
                   pl.BlockSpec((1, nblk, tn), lambda i, j: (i, 0, j)),
                   pl.BlockSpec((nblk, tn, MOBA_BLOCK), lambda i, j: (i, j, 0))],
        out_shape=[jax.ShapeDtypeStruct((m, n), BF16),
                   jax.ShapeDtypeStruct((m // tm, nblk, n), F32),
                   jax.ShapeDtypeStruct((m // MOBA_BLOCK, n, MOBA_BLOCK), BF16)],
        scratch_shapes=[pltpu.VMEM((tm, c), BF16)],
        compiler_params=_params("parallel", "arbitrary"),
        name="shared_kv_proj",
    )(h, w_k, w_v_t)


def _q_kernel(x_ref, w_ref, o_ref, xb_ref):
    @pl.when(pl.program_id(1) == 0)
    def _():
        xb_ref[...] = x_ref[...].astype(BF16)

    o_ref[...] = jnp.dot(xb_ref[...], w_ref[...], preferred_element_type=F32)


def _q_proj(h, w_q, tm=2 * TM, tn=1024):
    m, c = h.shape
    n = w_q.shape[1]
    return pl.pallas_call(
        _q_kernel,
        grid=(m // tm, n // tn),
        in_specs=[pl.BlockSpec((tm, c), lambda i, j: (i, 0)),
                  pl.BlockSpec((c, tn), lambda i, j: (0, j))],
        out_specs=pl.BlockSpec((tm, tn), lambda i, j: (i, j)),
        out_shape=jax.ShapeDtypeStruct((m, n), F32),
        scratch_shapes=[pltpu.VMEM((tm, c), BF16)],
        compiler_params=_params("parallel", "arbitrary"),
        name="moba_q_proj",
    )(h, w_q)


def _moba_kernel(q_ref, k_ref, vt_ref, km_ref, o_ref, sel_ref, *, n_blocks, top_k, heads):
    blk = pl.program_id(2)
    BLK, DH = MOBA_BLOCK, MOBA_HEAD_DIM
    start = pl.multiple_of(blk * BLK, BLK)
    ki = lax.broadcasted_iota(jnp.int32, (BLK, BLK), 0)
    qi = lax.broadcasted_iota(jnp.int32, (BLK, BLK), 1)
    bi = lax.broadcasted_iota(jnp.int32, (n_blocks, BLK), 0)
    cols = [slice(hh * DH, (hh + 1) * DH) for hh in range(heads)]

    R = range(heads)
    q = [q_ref[:, cols[h]] for h in R]
    qb = [(q[h] * DH ** -0.5).astype(BF16) for h in R]

    gate = [_dot_nt_split(km_ref[0, :, cols[h]], q[h]) for h in R]
    rank = [jnp.zeros((n_blocks, BLK), jnp.int32) for h in R]
    for mm in range(n_blocks):
        past = (mm < blk).astype(jnp.int32)
        for h in R:
            gm = gate[h][mm:mm + 1, :]
            ahead = (gm > gate[h]) | ((gm == gate[h]) & (mm < bi))
            rank[h] = rank[h] + jnp.where(ahead, 1, 0) * past
    for h in R:
        sel_ref[h] = jnp.where((bi < blk) & (rank[h] < top_k), 1.0, 0.0)

    s = [_dot_nt(k_ref[pl.ds(start, BLK), cols[h]], qb[h]) for h in R]
    s = [jnp.where(ki <= qi, s[h], NEG) for h in R]
    ones_rows = jnp.ones((ONES_ROWS, BLK), BF16)
    v_ext = lambda n, h: jnp.concatenate([vt_ref[n, cols[h], :], ones_rows], axis=0)
    m0 = [jnp.max(s[h], axis=0, keepdims=True) for h in R]
    p = [jnp.exp((s[h] - m0[h]).astype(BF16)) for h in R]
    acc0 = [jnp.dot(v_ext(blk, h), p[h], preferred_element_type=F32) for h in R]

    def body(n, carry):
        m_i, acc = carry
        off = pl.multiple_of(n * BLK, BLK)
        sn = [_dot_nt(k_ref[pl.ds(off, BLK), cols[h]], qb[h]) for h in R]
        sn = [jnp.where(sel_ref[h, pl.ds(n, 1), :] > 0.0, sn[h], NEG) for h in R]
        m_new = [jnp.maximum(m_i[h], jnp.max(sn[h], axis=0, keepdims=True)) for h in R]
        corr = [jnp.exp(m_i[h] - m_new[h]) for h in R]
        pn = [jnp.exp((sn[h] - m_new[h]).astype(BF16)) for h in R]
        pv = [jnp.dot(v_ext(n, h), pn[h], preferred_element_type=F32) for h in R]
        acc_new = [corr[h] * acc[h] + pv[h] for h in R]
        return m_new, acc_new

    _, acc_f = lax.fori_loop(0, blk, body, (m0, acc0))
    for h in R:
        out = acc_f[h][:DH] / acc_f[h][DH:DH + 1]
        o_ref[:, cols[h]] = out.T.astype(o_ref.dtype)


def _moba_attention(q, k, v_t, k_mean, batch, seq_len, heads=MOBA_HEADS_PER_STEP):
    m, hd = q.shape
    n_blocks = seq_len // MOBA_BLOCK
    top_k = max(1, min(MOBA_TOPK, n_blocks - 1))
    BLK, W = MOBA_BLOCK, heads * MOBA_HEAD_DIM
    return pl.pallas_call(
        functools.partial(_moba_kernel, n_blocks=n_blocks, top_k=top_k, heads=heads),
        grid=(batch, hd // W, n_blocks),
        in_specs=[pl.BlockSpec((BLK, W), lambda b, h, t: (b * n_blocks + t, h)),
                  pl.BlockSpec((seq_len, W), lambda b, h, t: (b, h)),
                  pl.BlockSpec((n_blocks, W, BLK), lambda b, h, t: (b, h, 0)),
                  pl.BlockSpec((1, n_blocks, W), lambda b, h, t: (b, 0, h))],
        out_specs=pl.BlockSpec((BLK, W), lambda b, h, t: (b * n_blocks + t, h)),
        out_shape=jax.ShapeDtypeStruct((m, hd), BF16),
        scratch_shapes=[pltpu.VMEM((heads, n_blocks, BLK), F32)],
        compiler_params=_params("parallel", "parallel", "arbitrary"),
        name="moba_attention",
    )(q, k, v_t, k_mean)


def _pad_lora(w_in, w_out):
    rank = w_in.shape[1]
    pad = -rank % LORA_PAD
    return (jnp.pad(w_in, ((0, 0), (0, pad))).astype(BF16),
            jnp.pad(w_out, ((0, pad), (0, 0))).astype(BF16))


def kernel(x, a_mix, a_w_r, a_w_k, a_w_v, a_w_o, a_w0, a_w1, a_w2, a_a0, a_a1, a_a2, a_g1, a_g2,
           a_k_k, a_k_a, a_r_k, a_lnx_w, a_lnx_b, kv_w_k, kv_w_v, b_w_q, b_w_o, ffn_w_gate,
           ffn_w_up, ffn_w_down, ln1_w, ln1_b, ln2_w, ln2_b):
    batch, seq_len, c = x.shape
    n_a = a_mix.shape[0]
    n_layers = ffn_w_gate.shape[0]
    assert seq_len % MOBA_BLOCK == 0 and seq_len % TM == 0 and c == D_MODEL
    h = x.reshape(batch * seq_len, c)
    row = lambda vec: vec.reshape(1, c)
    k_all = v_t = k_mean = None

    for layer in range(n_layers):
        if layer < n_a:
            i = layer
            w1, w2 = _pad_lora(a_w1[i], a_w2[i])
            a1, a2 = _pad_lora(a_a1[i], a_a2[i])
            g1, g2 = _pad_lora(a_g1[i], a_g2[i])
            *ops, bonus, g, gamma = _rwkv_front(
                h, a_mix[i], a_w_r[i].astype(BF16), a_w_k[i].astype(BF16), a_w_v[i].astype(BF16),
                w1, w2, row(a_w0[i]), a1, a2, row(a_a0[i]), g1, g2, row(a_k_k[i]),
                row(a_k_a[i]), row(a_r_k[i]), seq_len, SCAN_CHUNK)
            y = _rwkv_scan(ops, gamma, batch, seq_len).reshape(batch * seq_len, c)
            h = _gn_proj_ln(y, bonus, g, row(a_lnx_w[i]), row(a_lnx_b[i]),
                            a_w_o[i].astype(BF16), h, row(ln1_w[layer]), row(ln1_b[layer]))
        else:
            j = layer - n_a
            q = _q_proj(h, b_w_q[j].astype(BF16))
            z = _moba_attention(q, k_all, v_t, k_mean, batch, seq_len)
            h = _proj_ln(z, b_w_o[j].astype(BF16), h, row(ln1_w[layer]), row(ln1_b[layer]))
        h = _ffn_ln(h, ffn_w_gate, ffn_w_up, ffn_w_down, layer, row(ln2_w[layer]),
                    row(ln2_b[layer]))
        if layer == n_a - 1:
            k_all, k_mean, v_t = _kv_proj(h, kv_w_k.astype(BF16), kv_w_v.T.astype(BF16))
            k_mean = k_mean.reshape(batch, seq_len // MOBA_BLOCK, -1)
    return h.reshape(batch, seq_len, c)
```

```python
import functools

import jax
import jax.numpy as jnp
from jax import lax
from jax.experimental import pallas as pl
from jax.experimental.pallas import tpu as pltpu

D_MODEL = 2048
RWKV_HEAD = 64
MOBA_HEAD_DIM = 128
MOBA_HEADS = D_MODEL // MOBA_HEAD_DIM
MOBA_BLOCK = 256
MOBA_TOPK = 3
NEG = -1e30
GN_EPS = 64e-5
LN_EPS = 1e-5
DEPTH = 2
ALPHA = (2 * DEPTH) ** 0.25
LORA_PAD = 128
NEG_EXP_MINUS_HALF = -0.6065306597126334

VMEM_LIMIT_BYTES = 56 * 1024 * 1024
FFN_VMEM_LIMIT_BYTES = 60000 * 1024

TM = 512
SCAN_CHUNK = 64
SCAN_HEADS = 4
SCAN_GROUPS = 8
MOBA_HEADS_PER_STEP = 16
ONES_ROWS = 16
EW_ROWS, EW_LANES = 512, 512
SEG_LANES = 256

F32 = jnp.float32
BF16 = jnp.bfloat16


def _params(*semantics, vmem_limit_bytes=VMEM_LIMIT_BYTES):
    return pltpu.CompilerParams(dimension_semantics=semantics, vmem_limit_bytes=vmem_limit_bytes)


def _dot(a, b):
    return jnp.dot(a.astype(BF16), b.astype(BF16), preferred_element_type=F32)


def _dot_nt(a, b):
    return lax.dot_general(a.astype(BF16), b.astype(BF16), (((1,), (1,)), ((), ())),
                           preferred_element_type=F32)


def _dot_nt_split(a, b):
    a_hi = a.astype(BF16).astype(F32)
    b_hi = b.astype(BF16)
    b_lo = (b - b_hi.astype(F32)).astype(BF16)
    rows = a.shape[0]
    both = _dot_nt(jnp.concatenate([a_hi, a - a_hi], axis=0), b_hi)
    return both[:rows] + both[rows:] + _dot_nt(a_hi, b_lo)


def _seg_sum(x, seg_ones, split=True):
    width = seg_ones.shape[0]
    parts = []
    for c0 in range(0, x.shape[1], width):
        xs = x[:, c0:c0 + width]
        hi = xs.astype(BF16)
        acc = jnp.dot(hi, seg_ones, preferred_element_type=F32)
        if split:
            lo = (xs - hi.astype(F32)).astype(BF16)
            acc = acc + jnp.dot(lo, seg_ones, preferred_element_type=F32)
        parts.append(acc)
    return parts[0] if len(parts) == 1 else jnp.concatenate(parts, axis=1)


def _seg_ones():
    head = jnp.arange(SEG_LANES) // RWKV_HEAD
    return (head[:, None] == head[None, :]).astype(BF16)


def _layernorm(t, w, b):
    mu = jnp.mean(t, axis=-1, keepdims=True)
    d = t - mu
    var = jnp.mean(d * d, axis=-1, keepdims=True)
    return d * lax.rsqrt(var + LN_EPS) * w + b


def _token_shift_delta(x_ref, prev_ref, seq_start):
    x = x_ref[...]
    rolled = pltpu.roll(x, 1, 0)
    prev_last = jnp.where(seq_start, 0.0, prev_ref[7:8, :])
    row = lax.broadcasted_iota(jnp.int32, x.shape, 0)
    x_prev = jnp.where(row == 0, prev_last, rolled)
    return x, x_prev - x


def _front_kernel(x_ref, prev_ref, mix_ref, wr_ref, wk_ref, wv_ref, w1_ref, w2_ref, w0_ref,
                  a1_ref, a2_ref, a0_ref, g1_ref, g2_ref, kk_ref, ka_ref, rk_ref, ones_ref,
                  at_ref, rt_ref, bt_ref, kt_ref, vb_ref, bonus_ref, g_ref,
                  gam_ref, xr_ref, xk_ref, xv_ref, hw_ref, ha_ref, hg_ref,
                  *, tiles_per_seq, chunk):
    @pl.when(pl.program_id(1) == 0)
    def _():
        seq_start = (pl.program_id(0) % tiles_per_seq) == 0
        x, xx = _token_shift_delta(x_ref, prev_ref, seq_start)
        mixed = lambda idx: x + xx * mix_ref[idx:idx + 1, :]
        xr_ref[...] = mixed(0).astype(BF16)
        xk_ref[...] = mixed(2).astype(BF16)
        xv_ref[...] = mixed(3).astype(BF16)
        hw_ref[...] = jnp.tanh(_dot(mixed(1), w1_ref[...])).astype(BF16)
        ha_ref[...] = _dot(mixed(4), a1_ref[...]).astype(BF16)
        hg_ref[...] = jax.nn.sigmoid(_dot(mixed(5), g1_ref[...])).astype(BF16)

    dot = functools.partial(jnp.dot, preferred_element_type=F32)
    rows, lanes = at_ref.shape
    n_chunks = rows // chunk
    blocks = [slice(c0, c0 + SEG_LANES) for c0 in range(0, lanes, SEG_LANES)]
    xr, xk, xv, hw, ha, hg = (ref[...] for ref in (xr_ref, xk_ref, xv_ref, hw_ref, ha_ref, hg_ref))
    r = [dot(xr, wr_ref[:, cb]) for cb in blocks]
    k = [dot(xk, wk_ref[:, cb]) for cb in blocks]
    v = [dot(xv, wv_ref[:, cb]) for cb in blocks]
    w_pre = [w0_ref[:, cb] + dot(hw, w2_ref[:, cb]) for cb in blocks]
    a_pre = [a0_ref[:, cb] + dot(ha, a2_ref[:, cb]) for cb in blocks]
    for cb in blocks:
        g_ref[:, cb] = dot(hg, g2_ref[:, cb])

    ones = ones_ref[...]
    pos = lax.broadcasted_iota(jnp.int32, (rows, SEG_LANES), 0) % chunk
    for idx, cb in enumerate(blocks):
        ld = jax.nn.sigmoid(w_pre[idx]) * NEG_EXP_MINUS_HALF
        a = jax.nn.sigmoid(a_pre[idx])
        kk = k[idx] * kk_ref[:, cb]
        kk = kk * lax.rsqrt(jnp.maximum(_seg_sum(kk * kk, ones), 1e-24))
        k_mod = k[idx] * (1.0 + (a - 1.0) * ka_ref[:, cb])
        b_vec = kk * a

        cs = ld
        step = 1
        while step < chunk:
            cs = cs + jnp.where(pos >= step, pltpu.roll(cs, step, 0), 0.0)
            step *= 2
        cs3 = cs.reshape(n_chunks, chunk, SEG_LANES)
        cs_last = cs3[:, chunk - 1:chunk, :]
        e_neg = jnp.exp(-cs)

        at_ref[:, cb] = (-kk * jnp.exp(cs - ld)).astype(BF16)
        rt_ref[:, cb] = (r[idx] * jnp.exp(cs)).astype(BF16)
        bt_ref[:, cb] = (b_vec * e_neg).astype(BF16)
        kt_ref[:, cb] = (k_mod * e_neg).astype(BF16)
        vb_ref[:, cb] = v[idx].astype(BF16)
        bonus_ref[:, cb] = _seg_sum(r[idx] * k_mod * rk_ref[:, cb], ones, split=False) * v[idx]
        gam_ref[:, cb] = jnp.exp(cs_last).reshape(n_chunks, SEG_LANES)


def _rwkv_front(h, mix, w_r, w_k, w_v, w1, w2, w0, a1, a2, a0, g1, g2, k_k, k_a, r_k,
                seq_len, chunk, rows=EW_ROWS, lanes=EW_LANES):
    m, c = h.shape
    full = lambda arr: pl.BlockSpec(arr.shape, lambda i, j: (0,) * arr.ndim)
    cols = lambda arr: pl.BlockSpec((arr.shape[0], lanes), lambda i, j: (0, j))
    tile = pl.BlockSpec((rows, lanes), lambda i, j: (i, j))
    ones = _seg_ones()
    bf = jax.ShapeDtypeStruct((m, c), BF16)
    f32 = jax.ShapeDtypeStruct((m, c), F32)
    return pl.pallas_call(
        functools.partial(_front_kernel, tiles_per_seq=seq_len // rows, chunk=chunk),
        grid=(m // rows, c // lanes),
        in_specs=[pl.BlockSpec((rows, c), lambda i, j: (i, 0)),
                  pl.BlockSpec((8, c), lambda i, j: (jnp.maximum(i * (rows // 8) - 1, 0), 0)),
                  full(mix), cols(w_r), cols(w_k), cols(w_v),
                  full(w1), cols(w2), cols(w0), full(a1), cols(a2), cols(a0),
                  full(g1), cols(g2), cols(k_k), cols(k_a), cols(r_k), full(ones)],
        out_specs=[tile] * 7 + [pl.BlockSpec((rows // chunk, lanes), lambda i, j: (i, j))],
        out_shape=[bf] * 5 + [f32, f32, jax.ShapeDtypeStruct((m // chunk, c), F32)],
        scratch_shapes=[pltpu.VMEM((rows, c), BF16)] * 3
        + [pltpu.VMEM((rows, w1.shape[1]), BF16), pltpu.VMEM((rows, a1.shape[1]), BF16),
           pltpu.VMEM((rows, g1.shape[1]), BF16)],
        compiler_params=_params("parallel", "arbitrary"),
        name="rwkv_front",
    )(h, h, mix, w_r, w_k, w_v, w1, w2, w0, a1, a2, a0, g1, g2, k_k, k_a, r_k, ones)


def _scan_kernel(at_ref, rt_ref, bt_ref, kt_ref, v_ref, gam_ref, y_ref, st_ref,
                 *, chunk, heads, groups):
    L, G, N = chunk, heads, RWKV_HEAD
    GN, GL = G * N, G * L
    assert L == N

    @pl.when(pl.program_id(2) == 0)
    def _():
        st_ref[...] = jnp.zeros_like(st_ref)

    seg_mask = (lax.broadcasted_iota(jnp.int32, (GN, GN), 0) // N
                == lax.broadcasted_iota(jnp.int32, (GN, GN), 1) // N)
    expand_mask = (lax.broadcasted_iota(jnp.int32, (GL, GN), 0) // L
                   == lax.broadcasted_iota(jnp.int32, (GL, GN), 1) // N)
    row_t = lax.broadcasted_iota(jnp.int32, (L, GN), 0)
    lane_s = lax.broadcasted_iota(jnp.int32, (L, GN), 1) % L
    strict_c = lane_s < row_t
    incl_c = lane_s <= row_t
    eye_c = (lane_s == row_t).astype(F32)

    def expand(x):
        xb = x.astype(BF16)
        return jnp.where(expand_mask, jnp.concatenate([xb] * G, axis=0), jnp.zeros((), BF16))

    def compact_dot(x, w_bd):
        return jnp.dot(x.astype(BF16), w_bd, preferred_element_type=F32)

    R = range(groups)
    lanes = [slice(gi * GN, (gi + 1) * GN) for gi in R]
    load = lambda ref: [ref[0, :, lanes[gi]] for gi in R]
    at, rt, bt, kt, v = (load(ref) for ref in (at_ref, rt_ref, bt_ref, kt_ref, v_ref))
    s0 = [st_ref[gi] for gi in R]
    gam = [gam_ref[0, :, lanes[gi]] for gi in R]

    ar = [jnp.concatenate([at[gi], rt[gi]], axis=0) for gi in R]
    bt_e = [expand(x) for x in bt]
    kt_e = [expand(x) for x in kt]
    v_e = [expand(x) for x in v]
    gram_b = [_dot_nt(ar[gi], bt_e[gi]) for gi in R]
    gram_k = [_dot_nt(ar[gi], kt_e[gi]) for gi in R]
    a_ab = [jnp.where(strict_c, gram_b[gi][:L], 0.0) for gi in R]
    a_rb = [jnp.where(incl_c, gram_b[gi][L:], 0.0) for gi in R]
    causal_2l = jnp.concatenate([strict_c, incl_c], axis=0)
    a_ak_rk = [jnp.where(causal_2l, gram_k[gi], 0.0) for gi in R]
    from_v = [compact_dot(a_ak_rk[gi], v_e[gi]) for gi in R]

    t_c = [eye_c + a_ab[gi] for gi in R]
    p_bd = [expand(a_ab[gi]) for gi in R]
    p_c = [compact_dot(a_ab[gi], p_bd[gi]) for gi in R]
    p_bd = [expand(p_c[gi]) for gi in R]
    for _ in range(L.bit_length() - 3):
        tp = [compact_dot(jnp.concatenate([t_c[gi], p_c[gi]], axis=0), p_bd[gi]) for gi in R]
        t_c = [t_c[gi] + tp[gi][:L] for gi in R]
        p_c = [tp[gi][L:] for gi in R]
        p_bd = [expand(p_c[gi]) for gi in R]
    t_c = [t_c[gi] + compact_dot(t_c[gi], p_bd[gi]) for gi in R]

    s0b = [x.astype(BF16) for x in s0]
    from_state = [_dot_nt(ar[gi], s0b[gi]) for gi in R]
    rhs = [from_state[gi][:L] + from_v[gi][:L] for gi in R]
    u = [compact_dot(t_c[gi], expand(rhs[gi])) for gi in R]
    y = [from_state[gi][L:] + from_v[gi][L:] + compact_dot(a_rb[gi], expand(u[gi]))
         for gi in R]
    uv = [jnp.concatenate([u[gi], v[gi].astype(F32)], axis=0) for gi in R]
    bk = [jnp.concatenate([bt[gi], kt[gi]], axis=0) for gi in R]
    upd = [_dot(uv[gi].T, bk[gi]) for gi in R]
    for gi in R:
        y_ref[0, :, lanes[gi]] = y[gi]
        st_ref[gi] = (s0[gi] + jnp.where(seg_mask, upd[gi], 0.0)) * gam[gi]


def _rwkv_scan(ops, gamma, batch, seq_len, chunk=SCAN_CHUNK, heads=SCAN_HEADS,
               groups=SCAN_GROUPS):
    c = gamma.shape[-1]
    gn = heads * RWKV_HEAD
    lanes = groups * gn
    n_chunks = seq_len // chunk
    seq = pl.BlockSpec((1, chunk, lanes), lambda b, hg, t: (b, t, hg))
    gam = pl.BlockSpec((1, 1, lanes), lambda b, hg, t: (b * n_chunks + t, 0, hg))
    return pl.pallas_call(
        functools.partial(_scan_kernel, chunk=chunk, heads=heads, groups=groups),
        grid=(batch, c // lanes, n_chunks),
        in_specs=[seq] * 5 + [gam],
        out_specs=seq,
        out_shape=jax.ShapeDtypeStruct((batch, seq_len, c), F32),
        scratch_shapes=[pltpu.VMEM((groups, gn, gn), F32)],
        compiler_params=_params("parallel", "parallel", "arbitrary"),
        name="rwkv_scan",
    )(*[o.reshape(batch, seq_len, c) for o in ops], gamma.reshape(batch * n_chunks, 1, c))


def _gn_proj_ln_kernel(y_ref, bonus_ref, g_ref, gnw_ref, gnb_ref, ones_ref, w_ref, h_ref,
                       lnw_ref, lnb_ref, o_ref):
    ones = ones_ref[...]
    inv_n = 1.0 / RWKV_HEAD
    half = y_ref.shape[0] // 2
    halves = [slice(0, half), slice(half, 2 * half)]

    def gated(rs):
        z = []
        for c0 in range(0, y_ref.shape[1], SEG_LANES):
            cb = slice(c0, c0 + SEG_LANES)
            y = y_ref[rs, cb]
            mu = _seg_sum(y, ones) * inv_n
            d = y - mu
            var = _seg_sum(d * d, ones) * inv_n
            yn = d * lax.rsqrt(var + GN_EPS) * gnw_ref[:, cb] + gnb_ref[:, cb]
            z.append(((yn + bonus_ref[rs, cb]) * g_ref[rs, cb]).astype(BF16))
        return jnp.concatenate(z, axis=1)

    z = [gated(rs) for rs in halves]
    proj = [jnp.dot(zh, w_ref[...], preferred_element_type=F32) for zh in z]
    for idx, rs in enumerate(halves):
        o_ref[rs, :] = _layernorm(ALPHA * h_ref[rs, :] + proj[idx], lnw_ref[...], lnb_ref[...])


def _gn_proj_ln(y, bonus, g, gn_w, gn_b, w, h, ln_w, ln_b, tm=TM // 2):
    m, c = y.shape
    tile = pl.BlockSpec((tm, c), lambda i: (i, 0))
    vec = pl.BlockSpec((1, c), lambda i: (0, 0))
    return pl.pallas_call(
        _gn_proj_ln_kernel,
        grid=(m // tm,),
        in_specs=[tile, tile, tile, vec, vec,
                  pl.BlockSpec((SEG_LANES, SEG_LANES), lambda i: (0, 0)),
                  pl.BlockSpec((c, c), lambda i: (0, 0)), tile, vec, vec],
        out_specs=tile,
        out_shape=jax.ShapeDtypeStruct((m, c), F32),
        compiler_params=_params("parallel"),
        name="rwkv_gn_proj_ln",
    )(y, bonus, g, gn_w, gn_b, _seg_ones(), w, h, ln_w, ln_b)


def _proj_ln_kernel(z_ref, w_ref, h_ref, lnw_ref, lnb_ref, o_ref):
    half = z_ref.shape[0] // 2
    halves = [slice(0, half), slice(half, 2 * half)]
    y = [jnp.dot(z_ref[rs, :], w_ref[...], preferred_element_type=F32) for rs in halves]
    for idx, rs in enumerate(halves):
        o_ref[rs, :] = _layernorm(ALPHA * h_ref[rs, :] + y[idx], lnw_ref[...], lnb_ref[...])


def _proj_ln(z, w, h, ln_w, ln_b, tm=TM):
    m, kdim = z.shape
    c = w.shape[1]
    vec = pl.BlockSpec((1, c), lambda i: (0, 0))
    return pl.pallas_call(
        _proj_ln_kernel,
        grid=(m // tm,),
        in_specs=[pl.BlockSpec((tm, kdim), lambda i: (i, 0)),
                  pl.BlockSpec((kdim, c), lambda i: (0, 0)),
                  pl.BlockSpec((tm, c), lambda i: (i, 0)), vec, vec],
        out_specs=pl.BlockSpec((tm, c), lambda i: (i, 0)),
        out_shape=jax.ShapeDtypeStruct((m, c), F32),
        compiler_params=_params("parallel"),
        name="proj_residual_ln",
    )(z, w, h, ln_w, ln_b)


def _ffn_kernel(h_ref, wg_ref, wu_ref, wd_ref, lnw_ref, lnb_ref, o_ref):
    f = pl.program_id(1)

    @pl.when(f == 0)
    def _():
        o_ref[...] = jnp.zeros_like(o_ref)

    x = h_ref[...].astype(BF16)
    gate = jnp.dot(x, wg_ref[...].astype(BF16), preferred_element_type=F32)
    up = jnp.dot(x, wu_ref[...].astype(BF16), preferred_element_type=F32)
    act = (gate * jax.nn.sigmoid(gate)) * up
    o_ref[...] += jnp.dot(act.astype(BF16), wd_ref[...].astype(BF16),
                          preferred_element_type=F32)

    @pl.when(f == pl.num_programs(1) - 1)
    def _():
        o_ref[...] = _layernorm(ALPHA * h_ref[...] + o_ref[...], lnw_ref[...], lnb_ref[...])


def _ffn_ln(h, w_gate, w_up, w_down, layer, ln_w, ln_b, tm=2 * TM, tf=256):
    m, c = h.shape
    d_ff = w_gate.shape[2]
    vec = pl.BlockSpec((1, c), lambda i, f: (0, 0))
    return pl.pallas_call(
        _ffn_kernel,
        grid=(m // tm, d_ff // tf),
        in_specs=[pl.BlockSpec((tm, c), lambda i, f: (i, 0)),
                  pl.BlockSpec((None, c, tf), lambda i, f: (layer, 0, f)),
                  pl.BlockSpec((None, c, tf), lambda i, f: (layer, 0, f)),
                  pl.BlockSpec((None, tf, c), lambda i, f: (layer, f, 0)), vec, vec],
        out_specs=pl.BlockSpec((tm, c), lambda i, f: (i, 0)),
        out_shape=jax.ShapeDtypeStruct((m, c), F32),
        compiler_params=_params("parallel", "arbitrary", vmem_limit_bytes=FFN_VMEM_LIMIT_BYTES),
        name="swiglu_ffn_ln",
    )(h, w_gate, w_up, w_down, ln_w, ln_b)


def _kv_kernel(x_ref, wk_ref, wvt_ref, k_ref, mean_ref, vt_ref, xb_ref):
    @pl.when(pl.program_id(1) == 0)
    def _():
        xb_ref[...] = x_ref[...].astype(BF16)

    xb = xb_ref[...]
    y = jnp.dot(xb, wk_ref[...], preferred_element_type=F32)
    k_ref[...] = y.astype(k_ref.dtype)
    tm, tn = y.shape
    nblk = tm // MOBA_BLOCK
    mean_ref[0] = jnp.sum(y.reshape(nblk, MOBA_BLOCK, tn), axis=1) * (1.0 / MOBA_BLOCK)
    vt = lax.dot_general(wvt_ref[...], xb, (((1,), (1,)), ((), ())), preferred_element_type=F32)
    for blk in range(nblk):
        vt_ref[blk] = vt[:, blk * MOBA_BLOCK:(blk + 1) * MOBA_BLOCK].astype(vt_ref.dtype)


def _kv_proj(h, w_k, w_v_t, tm=2 * TM, tn=512):
    m, c = h.shape
    n = w_k.shape[1]
    nblk = tm // MOBA_BLOCK
    return pl.pallas_call(
        _kv_kernel,
        grid=(m // tm, n // tn),
        in_specs=[pl.BlockSpec((tm, c), lambda i, j: (i, 0)),
                  pl.BlockSpec((c, tn), lambda i, j: (0, j)),
                  pl.BlockSpec((tn, c), lambda i, j: (j, 0))],
        out_specs=[pl.BlockSpec((tm, tn), lambda i, j: (i, j)),
                   pl.BlockSpec((1, nblk, tn), lambda i, j: (i, 0, j)),
                   pl.BlockSpec((nblk, tn, MOBA_BLOCK), lambda i, j: (i, j, 0))],
        out_shape=[jax.ShapeDtypeStruct((m, n), BF16),
                   jax.ShapeDtypeStruct((m // tm, nblk, n), F32),
                   jax.ShapeDtypeStruct((m // MOBA_BLOCK, n, MOBA_BLOCK), BF16)],
        scratch_shapes=[pltpu.VMEM((tm, c), BF16)],
        compiler_params=_params("parallel", "arbitrary"),
        name="shared_kv_proj",
    )(h, w_k, w_v_t)


def _q_kernel(x_ref, w_ref, o_ref, xb_ref):
    @pl.when(pl.program_id(1) == 0)
    def _():
        xb_ref[...] = x_ref[...].astype(BF16)

    o_ref[...] = jnp.dot(xb_ref[...], w_ref[...], preferred_element_type=F32)


def _q_proj(h, w_q, tm=2 * TM, tn=1024):
    m, c = h.shape
    n = w_q.shape[1]
    return pl.pallas_call(
        _q_kernel,
        grid=(m // tm, n // tn),
        in_specs=[pl.BlockSpec((tm, c), lambda i, j: (i, 0)),
                  pl.BlockSpec((c, tn), lambda i, j: (0, j))],
        out_specs=pl.BlockSpec((tm, tn), lambda i, j: (i, j)),
        out_shape=jax.ShapeDtypeStruct((m, n), F32),
        scratch_shapes=[pltpu.VMEM((tm, c), BF16)],
        compiler_params=_params("parallel", "arbitrary"),
        name="moba_q_proj",
    )(h, w_q)


def _moba_kernel(q_ref, k_ref, vt_ref, km_ref, o_ref, sel_ref, *, n_blocks, top_k, heads):
    blk = pl.program_id(2)
    BLK, DH = MOBA_BLOCK, MOBA_HEAD_DIM
    start = pl.multiple_of(blk * BLK, BLK)
    ki = lax.broadcasted_iota(jnp.int32, (BLK, BLK), 0)
    qi = lax.broadcasted_iota(jnp.int32, (BLK, BLK), 1)
    bi = lax.broadcasted_iota(jnp.int32, (n_blocks, BLK), 0)
    cols = [slice(hh * DH, (hh + 1) * DH) for hh in range(heads)]

    R = range(heads)
    q = [q_ref[:, cols[h]] for h in R]
    qb = [(q[h] * DH ** -0.5).astype(BF16) for h in R]

    gate = [_dot_nt_split(km_ref[0, :, cols[h]], q[h]) for h in R]
    rank = [jnp.zeros((n_blocks, BLK), jnp.int32) for h in R]
    for mm in range(n_blocks):
        past = (mm < blk).astype(jnp.int32)
        for h in R:
            gm = gate[h][mm:mm + 1, :]
            ahead = (gm > gate[h]) | ((gm == gate[h]) & (mm < bi))
            rank[h] = rank[h] + jnp.where(ahead, 1, 0) * past
    for h in R:
        sel_ref[h] = jnp.where((bi < blk) & (rank[h] < top_k), 1.0, 0.0)

    s = [_dot_nt(k_ref[pl.ds(start, BLK), cols[h]], qb[h]) for h in R]
    s = [jnp.where(ki <= qi, s[h], NEG) for h in R]
    ones_rows = jnp.ones((ONES_ROWS, BLK), BF16)
    v_ext = lambda n, h: jnp.concatenate([vt_ref[n, cols[h], :], ones_rows], axis=0)
    m0 = [jnp.max(s[h], axis=0, keepdims=True) for h in R]
    p = [jnp.exp((s[h] - m0[h]).astype(BF16)) for h in R]
    acc0 = [jnp.dot(v_ext(blk, h), p[h], preferred_element_type=F32) for h in R]

    def body(n, carry):
        m_i, acc = carry
        off = pl.multiple_of(n * BLK, BLK)
        sn = [_dot_nt(k_ref[pl.ds(off, BLK), cols[h]], qb[h]) for h in R]
        sn = [jnp.where(sel_ref[h, pl.ds(n, 1), :] > 0.0, sn[h], NEG) for h in R]
        m_new = [jnp.maximum(m_i[h], jnp.max(sn[h], axis=0, keepdims=True)) for h in R]
        corr = [jnp.exp(m_i[h] - m_new[h]) for h in R]
        pn = [jnp.exp((sn[h] - m_new[h]).astype(BF16)) for h in R]
        pv = [jnp.dot(v_ext(n, h), pn[h], preferred_element_type=F32) for h in R]
        acc_new = [corr[h] * acc[h] + pv[h] for h in R]
        return m_new, acc_new

    _, acc_f = lax.fori_loop(0, blk, body, (m0, acc0))
    for h in R:
        out = acc_f[h][:DH] / acc_f[h][DH:DH + 1]
        o_ref[:, cols[h]] = out.T.astype(o_ref.dtype)


def _moba_attention(q, k, v_t, k_mean, batch, seq_len, heads=MOBA_HEADS_PER_STEP):
    m, hd = q.shape
    n_blocks = seq_len // MOBA_BLOCK
    top_k = max(1, min(MOBA_TOPK, n_blocks - 1))
    BLK, W = MOBA_BLOCK, heads * MOBA_HEAD_DIM
    return pl.pallas_call(
        functools.partial(_moba_kernel, n_blocks=n_blocks, top_k=top_k, heads=heads),
        grid=(batch, hd // W, n_blocks),
        in_specs=[pl.BlockSpec((BLK, W), lambda b, h, t: (b * n_blocks + t, h)),
                  pl.BlockSpec((seq_len, W), lambda b, h, t: (b, h)),
                  pl.BlockSpec((n_blocks, W, BLK), lambda b, h, t: (b, h, 0)),
                  pl.BlockSpec((1, n_blocks, W), lambda b, h, t: (b, 0, h))],
        out_specs=pl.BlockSpec((BLK, W), lambda b, h, t: (b * n_blocks + t, h)),
        out_shape=jax.ShapeDtypeStruct((m, hd), BF16),
        scratch_shapes=[pltpu.VMEM((heads, n_blocks, BLK), F32)],
        compiler_params=_params("parallel", "parallel", "arbitrary"),
        name="moba_attention",
    )(q, k, v_t, k_mean)


def _pad_lora(w_in, w_out):
    rank = w_in.shape[1]
    pad = -rank % LORA_PAD
    return (jnp.pad(w_in, ((0, 0), (0, pad))).astype(BF16),
            jnp.pad(w_out, ((0, pad), (0, 0))).astype(BF16))


def kernel(x, a_mix, a_w_r, a_w_k, a_w_v, a_w_o, a_w0, a_w1, a_w2, a_a0, a_a1, a_a2, a_g1, a_g2,
           a_k_k, a_k_a, a_r_k, a_lnx_w, a_lnx_b, kv_w_k, kv_w_v, b_w_q, b_w_o, ffn_w_gate,
           ffn_w_up, ffn_w_down, ln1_w, ln1_b, ln2_w, ln2_b):
    batch, seq_len, c = x.shape
    n_a = a_mix.shape[0]
    n_layers = ffn_w_gate.shape[0]
    assert seq_len % MOBA_BLOCK == 0 and seq_len % (2 * TM) == 0 and c == D_MODEL
    h = x.reshape(batch * seq_len, c)
    row = lambda vec: vec.reshape(1, c)
    k_all = v_t = k_mean = None

    for layer in range(n_layers):
        if layer < n_a:
            i = layer
            w1, w2 = _pad_lora(a_w1[i], a_w2[i])
            a1, a2 = _pad_lora(a_a1[i], a_a2[i])
            g1, g2 = _pad_lora(a_g1[i], a_g2[i])
            *ops, bonus, g, gamma = _rwkv_front(
                h, a_mix[i], a_w_r[i].astype(BF16), a_w_k[i].astype(BF16), a_w_v[i].astype(BF16),
                w1, w2, row(a_w0[i]), a1, a2, row(a_a0[i]), g1, g2, row(a_k_k[i]),
                row(a_k_a[i]), row(a_r_k[i]), seq_len, SCAN_CHUNK)
            y = _rwkv_scan(ops, gamma, batch, seq_len).reshape(batch * seq_len, c)
            h = _gn_proj_ln(y, bonus, g, row(a_lnx_w[i]), row(a_lnx_b[i]),
                            a_w_o[i].astype(BF16), h, row(ln1_w[layer]), row(ln1_b[layer]))
        else:
            j = layer - n_a
            q = _q_proj(h, b_w_q[j].astype(BF16))
            z = _moba_attention(q, k_all, v_t, k_mean, batch, seq_len)
            h = _proj_ln(z, b_w_o[j].astype(BF16), h, row(ln1_w[layer]), row(ln1_b[layer]))
        h = _ffn_ln(h, ffn_w_gate, ffn_w_up, ffn_w_down, layer, row(ln2_w[layer]),
                    row(ln2_b[layer]))
        if layer == n_a - 1:
            k_all, k_mean, v_t = _kv_proj(h, kv_w_k.astype(BF16), kv_w_v.T.astype(BF16))
            k_mean = k_mean.reshape(batch, seq_len // MOBA_BLOCK, -1)
    return h.reshape(batch, seq_len, c)
```

```python
import functools

import jax
import jax.numpy as jnp
from jax import lax
from jax.experimental import pallas as pl
from jax.experimental.pallas import tpu as pltpu

D_MODEL = 2048
RWKV_HEAD = 64
MOBA_HEAD_DIM = 128
MOBA_HEADS = D_MODEL // MOBA_HEAD_DIM
MOBA_BLOCK = 256
MOBA_TOPK = 3
NEG = -1e30
GN_EPS = 64e-5
LN_EPS = 1e-5
DEPTH = 2
ALPHA = (2 * DEPTH) ** 0.25
LORA_PAD = 128
NEG_EXP_MINUS_HALF = -0.6065306597126334

VMEM_LIMIT_BYTES = 56 * 1024 * 1024
FFN_VMEM_LIMIT_BYTES = 60000 * 1024

TM = 512
SCAN_CHUNK = 64
SCAN_HEADS = 4
SCAN_GROUPS = 8
SCAN_BATCHES = 4
MOBA_HEADS_PER_STEP = 16
ONES_ROWS = 16
EW_ROWS, EW_LANES = 512, 512
SEG_LANES = 256

F32 = jnp.float32
BF16 = jnp.bfloat16


def _params(*semantics, vmem_limit_bytes=VMEM_LIMIT_BYTES):
    return pltpu.CompilerParams(dimension_semantics=semantics, vmem_limit_bytes=vmem_limit_bytes)


def _dot(a, b):
    return jnp.dot(a.astype(BF16), b.astype(BF16), preferred_element_type=F32)


def _dot_nt(a, b):
    return lax.dot_general(a.astype(BF16), b.astype(BF16), (((1,), (1,)), ((), ())),
                           preferred_element_type=F32)


def _dot_nt_split(a, b):
    a_hi = a.astype(BF16).astype(F32)
    b_hi = b.astype(BF16)
    b_lo = (b - b_hi.astype(F32)).astype(BF16)
    rows = a.shape[0]
    both = _dot_nt(jnp.concatenate([a_hi, a - a_hi], axis=0), b_hi)
    return both[:rows] + both[rows:] + _dot_nt(a_hi, b_lo)


def _seg_sum(x, seg_ones, split=True):
    width = seg_ones.shape[0]
    parts = []
    for c0 in range(0, x.shape[1], width):
        xs = x[:, c0:c0 + width]
        hi = xs.astype(BF16)
        acc = jnp.dot(hi, seg_ones, preferred_element_type=F32)
        if split:
            lo = (xs - hi.astype(F32)).astype(BF16)
            acc = acc + jnp.dot(lo, seg_ones, preferred_element_type=F32)
        parts.append(acc)
    return parts[0] if len(parts) == 1 else jnp.concatenate(parts, axis=1)


def _seg_ones():
    head = jnp.arange(SEG_LANES) // RWKV_HEAD
    return (head[:, None] == head[None, :]).astype(BF16)


def _layernorm(t, w, b):
    mu = jnp.mean(t, axis=-1, keepdims=True)
    d = t - mu
    var = jnp.mean(d * d, axis=-1, keepdims=True)
    return d * lax.rsqrt(var + LN_EPS) * w + b


def _token_shift_delta(x_ref, prev_ref, seq_start):
    x = x_ref[...]
    rolled = pltpu.roll(x, 1, 0)
    prev_last = jnp.where(seq_start, 0.0, prev_ref[7:8, :])
    row = lax.broadcasted_iota(jnp.int32, x.shape, 0)
    x_prev = jnp.where(row == 0, prev_last, rolled)
    return x, x_prev - x


def _front_kernel(x_ref, prev_ref, mix_ref, wr_ref, wk_ref, wv_ref, w1_ref, w2_ref, w0_ref,
                  a1_ref, a2_ref, a0_ref, g1_ref, g2_ref, kk_ref, ka_ref, rk_ref, ones_ref,
                  at_ref, rt_ref, bt_ref, kt_ref, vb_ref, bonus_ref, g_ref,
                  gam_ref, xr_ref, xk_ref, xv_ref, hw_ref, ha_ref, hg_ref,
                  *, tiles_per_seq, chunk):
    @pl.when(pl.program_id(1) == 0)
    def _():
        seq_start = (pl.program_id(0) % tiles_per_seq) == 0
        x, xx = _token_shift_delta(x_ref, prev_ref, seq_start)
        mixed = lambda idx: x + xx * mix_ref[idx:idx + 1, :]
        xr_ref[...] = mixed(0).astype(BF16)
        xk_ref[...] = mixed(2).astype(BF16)
        xv_ref[...] = mixed(3).astype(BF16)
        hw_ref[...] = jnp.tanh(_dot(mixed(1), w1_ref[...])).astype(BF16)
        ha_ref[...] = _dot(mixed(4), a1_ref[...]).astype(BF16)
        hg_ref[...] = jax.nn.sigmoid(_dot(mixed(5), g1_ref[...])).astype(BF16)

    dot = functools.partial(jnp.dot, preferred_element_type=F32)
    rows, lanes = at_ref.shape
    n_chunks = rows // chunk
    blocks = [slice(c0, c0 + SEG_LANES) for c0 in range(0, lanes, SEG_LANES)]
    xr, xk, xv, hw, ha, hg = (ref[...] for ref in (xr_ref, xk_ref, xv_ref, hw_ref, ha_ref, hg_ref))
    r = [dot(xr, wr_ref[:, cb]) for cb in blocks]
    k = [dot(xk, wk_ref[:, cb]) for cb in blocks]
    v = [dot(xv, wv_ref[:, cb]) for cb in blocks]
    w_pre = [w0_ref[:, cb] + dot(hw, w2_ref[:, cb]) for cb in blocks]
    a_pre = [a0_ref[:, cb] + dot(ha, a2_ref[:, cb]) for cb in blocks]
    for cb in blocks:
        g_ref[:, cb] = dot(hg, g2_ref[:, cb])

    ones = ones_ref[...]
    pos = lax.broadcasted_iota(jnp.int32, (rows, SEG_LANES), 0) % chunk
    for idx, cb in enumerate(blocks):
        ld = jax.nn.sigmoid(w_pre[idx]) * NEG_EXP_MINUS_HALF
        a = jax.nn.sigmoid(a_pre[idx])
        kk = k[idx] * kk_ref[:, cb]
        kk = kk * lax.rsqrt(jnp.maximum(_seg_sum(kk * kk, ones), 1e-24))
        k_mod = k[idx] * (1.0 + (a - 1.0) * ka_ref[:, cb])
        b_vec = kk * a

        cs = ld
        step = 1
        while step < chunk:
            cs = cs + jnp.where(pos >= step, pltpu.roll(cs, step, 0), 0.0)
            step *= 2
        cs3 = cs.reshape(n_chunks, chunk, SEG_LANES)
        cs_last = cs3[:, chunk - 1:chunk, :]
        e_neg = jnp.exp(-cs)

        at_ref[:, cb] = (-kk * jnp.exp(cs - ld)).astype(BF16)
        rt_ref[:, cb] = (r[idx] * jnp.exp(cs)).astype(BF16)
        bt_ref[:, cb] = (b_vec * e_neg).astype(BF16)
        kt_ref[:, cb] = (k_mod * e_neg).astype(BF16)
        vb_ref[:, cb] = v[idx].astype(BF16)
        bonus_ref[:, cb] = _seg_sum(r[idx] * k_mod * rk_ref[:, cb], ones, split=False) * v[idx]
        gam_ref[:, cb] = jnp.exp(cs_last).reshape(n_chunks, SEG_LANES)


def _rwkv_front(h, mix, w_r, w_k, w_v, w1, w2, w0, a1, a2, a0, g1, g2, k_k, k_a, r_k,
                seq_len, chunk, rows=EW_ROWS, lanes=EW_LANES):
    m, c = h.shape
    full = lambda arr: pl.BlockSpec(arr.shape, lambda i, j: (0,) * arr.ndim)
    cols = lambda arr: pl.BlockSpec((arr.shape[0], lanes), lambda i, j: (0, j))
    tile = pl.BlockSpec((rows, lanes), lambda i, j: (i, j))
    ones = _seg_ones()
    bf = jax.ShapeDtypeStruct((m, c), BF16)
    f32 = jax.ShapeDtypeStruct((m, c), F32)
    return pl.pallas_call(
        functools.partial(_front_kernel, tiles_per_seq=seq_len // rows, chunk=chunk),
        grid=(m // rows, c // lanes),
        in_specs=[pl.BlockSpec((rows, c), lambda i, j: (i, 0)),
                  pl.BlockSpec((8, c), lambda i, j: (jnp.maximum(i * (rows // 8) - 1, 0), 0)),
                  full(mix), cols(w_r), cols(w_k), cols(w_v),
                  full(w1), cols(w2), cols(w0), full(a1), cols(a2), cols(a0),
                  full(g1), cols(g2), cols(k_k), cols(k_a), cols(r_k), full(ones)],
        out_specs=[tile] * 7 + [pl.BlockSpec((rows // chunk, lanes), lambda i, j: (i, j))],
        out_shape=[bf] * 5 + [f32, f32, jax.ShapeDtypeStruct((m // chunk, c), F32)],
        scratch_shapes=[pltpu.VMEM((rows, c), BF16)] * 3
        + [pltpu.VMEM((rows, w1.shape[1]), BF16), pltpu.VMEM((rows, a1.shape[1]), BF16),
           pltpu.VMEM((rows, g1.shape[1]), BF16)],
        compiler_params=_params("parallel", "arbitrary"),
        name="rwkv_front",
    )(h, h, mix, w_r, w_k, w_v, w1, w2, w0, a1, a2, a0, g1, g2, k_k, k_a, r_k, ones)


def _scan_kernel(at_ref, rt_ref, bt_ref, kt_ref, v_ref, gam_ref, y_ref, st_ref,
                 *, chunk, heads, groups, batches):
    L, G, N = chunk, heads, RWKV_HEAD
    GN, GL = G * N, G * L
    assert L == N

    @pl.when(pl.program_id(2) == 0)
    def _():
        st_ref[...] = jnp.zeros_like(st_ref)

    seg_mask = (lax.broadcasted_iota(jnp.int32, (GN, GN), 0) // N
                == lax.broadcasted_iota(jnp.int32, (GN, GN), 1) // N)
    expand_mask = (lax.broadcasted_iota(jnp.int32, (GL, GN), 0) // L
                   == lax.broadcasted_iota(jnp.int32, (GL, GN), 1) // N)
    row_t = lax.broadcasted_iota(jnp.int32, (L, GN), 0)
    lane_s = lax.broadcasted_iota(jnp.int32, (L, GN), 1) % L
    strict_c = lane_s < row_t
    incl_c = lane_s <= row_t
    eye_c = (lane_s == row_t).astype(F32)

    def expand(x):
        xb = x.astype(BF16)
        return jnp.where(expand_mask, jnp.concatenate([xb] * G, axis=0), jnp.zeros((), BF16))

    def compact_dot(x, w_bd):
        return jnp.dot(x.astype(BF16), w_bd, preferred_element_type=F32)

    R = range(batches * groups)
    where = [(ci // groups, slice((ci % groups) * GN, (ci % groups + 1) * GN)) for ci in R]
    load = lambda ref: [ref[bb, :, cols] for bb, cols in where]
    at, rt, bt, kt, v = (load(ref) for ref in (at_ref, rt_ref, bt_ref, kt_ref, v_ref))
    s0 = [st_ref[gi] for gi in R]
    gam = [gam_ref[bb, 0, :, cols] for bb, cols in where]

    ar = [jnp.concatenate([at[gi], rt[gi]], axis=0) for gi in R]
    bt_e = [expand(x) for x in bt]
    kt_e = [expand(x) for x in kt]
    v_e = [expand(x) for x in v]
    gram_b = [_dot_nt(ar[gi], bt_e[gi]) for gi in R]
    gram_k = [_dot_nt(ar[gi], kt_e[gi]) for gi in R]
    a_ab = [jnp.where(strict_c, gram_b[gi][:L], 0.0) for gi in R]
    a_rb = [jnp.where(incl_c, gram_b[gi][L:], 0.0) for gi in R]
    causal_2l = jnp.concatenate([strict_c, incl_c], axis=0)
    a_ak_rk = [jnp.where(causal_2l, gram_k[gi], 0.0) for gi in R]
    from_v = [compact_dot(a_ak_rk[gi], v_e[gi]) for gi in R]

    t_c = [eye_c + a_ab[gi] for gi in R]
    p_bd = [expand(a_ab[gi]) for gi in R]
    p_c = [compact_dot(a_ab[gi], p_bd[gi]) for gi in R]
    p_bd = [expand(p_c[gi]) for gi in R]
    for _ in range(L.bit_length() - 3):
        tp = [compact_dot(jnp.concatenate([t_c[gi], p_c[gi]], axis=0), p_bd[gi]) for gi in R]
        t_c = [t_c[gi] + tp[gi][:L] for gi in R]
        p_c = [tp[gi][L:] for gi in R]
        p_bd = [expand(p_c[gi]) for gi in R]
    t_c = [t_c[gi] + compact_dot(t_c[gi], p_bd[gi]) for gi in R]

    s0b = [x.astype(BF16) for x in s0]
    from_state = [_dot_nt(ar[gi], s0b[gi]) for gi in R]
    rhs = [from_state[gi][:L] + from_v[gi][:L] for gi in R]
    u = [compact_dot(t_c[gi], expand(rhs[gi])) for gi in R]
    y = [from_state[gi][L:] + from_v[gi][L:] + compact_dot(a_rb[gi], expand(u[gi]))
         for gi in R]
    uv = [jnp.concatenate([u[gi], v[gi].astype(F32)], axis=0) for gi in R]
    bk = [jnp.concatenate([bt[gi], kt[gi]], axis=0) for gi in R]
    upd = [_dot(uv[gi].T, bk[gi]) for gi in R]
    for gi, (bb, cols) in enumerate(where):
        y_ref[bb, :, cols] = y[gi]
        st_ref[gi] = (s0[gi] + jnp.where(seg_mask, upd[gi], 0.0)) * gam[gi]


def _rwkv_scan(ops, gamma, batch, seq_len, chunk=SCAN_CHUNK, heads=SCAN_HEADS,
               groups=SCAN_GROUPS, batches=SCAN_BATCHES):
    c = gamma.shape[-1]
    gn = heads * RWKV_HEAD
    lanes = groups * gn
    n_chunks = seq_len // chunk
    seq = pl.BlockSpec((batches, chunk, lanes), lambda b, hg, t: (b, t, hg))
    gam = pl.BlockSpec((batches, 1, 1, lanes), lambda b, hg, t: (b, t, 0, hg))
    return pl.pallas_call(
        functools.partial(_scan_kernel, chunk=chunk, heads=heads, groups=groups,
                          batches=batches),
        grid=(batch // batches, c // lanes, n_chunks),
        in_specs=[seq] * 5 + [gam],
        out_specs=seq,
        out_shape=jax.ShapeDtypeStruct((batch, seq_len, c), F32),
        scratch_shapes=[pltpu.VMEM((batches * groups, gn, gn), F32)],
        compiler_params=_params("parallel", "parallel", "arbitrary"),
        name="rwkv_scan",
    )(*[o.reshape(batch, seq_len, c) for o in ops], gamma.reshape(batch, n_chunks, 1, c))


def _gn_proj_ln_kernel(y_ref, bonus_ref, g_ref, gnw_ref, gnb_ref, ones_ref, w_ref, h_ref,
                       lnw_ref, lnb_ref, o_ref):
    ones = ones_ref[...]
    inv_n = 1.0 / RWKV_HEAD
    half = y_ref.shape[0] // 2
    halves = [slice(0, half), slice(half, 2 * half)]

    def gated(rs):
        z = []
        for c0 in range(0, y_ref.shape[1], SEG_LANES):
            cb = slice(c0, c0 + SEG_LANES)
            y = y_ref[rs, cb]
            mu = _seg_sum(y, ones) * inv_n
            d = y - mu
            var = _seg_sum(d * d, ones) * inv_n
            yn = d * lax.rsqrt(var + GN_EPS) * gnw_ref[:, cb] + gnb_ref[:, cb]
            z.append(((yn + bonus_ref[rs, cb]) * g_ref[rs, cb]).astype(BF16))
        return jnp.concatenate(z, axis=1)

    z = [gated(rs) for rs in halves]
    proj = [jnp.dot(zh, w_ref[...], preferred_element_type=F32) for zh in z]
    for idx, rs in enumerate(halves):
        o_ref[rs, :] = _layernorm(ALPHA * h_ref[rs, :] + proj[idx], lnw_ref[...], lnb_ref[...])


def _gn_proj_ln(y, bonus, g, gn_w, gn_b, w, h, ln_w, ln_b, tm=TM // 2):
    m, c = y.shape
    tile = pl.BlockSpec((tm, c), lambda i: (i, 0))
    vec = pl.BlockSpec((1, c), lambda i: (0, 0))
    return pl.pallas_call(
        _gn_proj_ln_kernel,
        grid=(m // tm,),
        in_specs=[tile, tile, tile, vec, vec,
                  pl.BlockSpec((SEG_LANES, SEG_LANES), lambda i: (0, 0)),
                  pl.BlockSpec((c, c), lambda i: (0, 0)), tile, vec, vec],
        out_specs=tile,
        out_shape=jax.ShapeDtypeStruct((m, c), F32),
        compiler_params=_params("parallel"),
        name="rwkv_gn_proj_ln",
    )(y, bonus, g, gn_w, gn_b, _seg_ones(), w, h, ln_w, ln_b)


def _proj_ln_kernel(z_ref, w_ref, h_ref, lnw_ref, lnb_ref, o_ref):
    half = z_ref.shape[0] // 2
    halves = [slice(0, half), slice(half, 2 * half)]
    y = [jnp.dot(z_ref[rs, :], w_ref[...], preferred_element_type=F32) for rs in halves]
    for idx, rs in enumerate(halves):
        o_ref[rs, :] = _layernorm(ALPHA * h_ref[rs, :] + y[idx], lnw_ref[...], lnb_ref[...])


def _proj_ln(z, w, h, ln_w, ln_b, tm=TM):
    m, kdim = z.shape
    c = w.shape[1]
    vec = pl.BlockSpec((1, c), lambda i: (0, 0))
    return pl.pallas_call(
        _proj_ln_kernel,
        grid=(m // tm,),
        in_specs=[pl.BlockSpec((tm, kdim), lambda i: (i, 0)),
                  pl.BlockSpec((kdim, c), lambda i: (0, 0)),
                  pl.BlockSpec((tm, c), lambda i: (i, 0)), vec, vec],
        out_specs=pl.BlockSpec((tm, c), lambda i: (i, 0)),
        out_shape=jax.ShapeDtypeStruct((m, c), F32),
        compiler_params=_params("parallel"),
        name="proj_residual_ln",
    )(z, w, h, ln_w, ln_b)


def _ffn_kernel(h_ref, wg_ref, wu_ref, wd_ref, lnw_ref, lnb_ref, o_ref):
    f = pl.program_id(1)

    @pl.when(f == 0)
    def _():
        o_ref[...] = jnp.zeros_like(o_ref)

    x = h_ref[...].astype(BF16)
    gate = jnp.dot(x, wg_ref[...].astype(BF16), preferred_element_type=F32)
    up = jnp.dot(x, wu_ref[...].astype(BF16), preferred_element_type=F32)
    act = (gate * jax.nn.sigmoid(gate)) * up
    o_ref[...] += jnp.dot(act.astype(BF16), wd_ref[...].astype(BF16),
                          preferred_element_type=F32)

    @pl.when(f == pl.num_programs(1) - 1)
    def _():
        o_ref[...] = _layernorm(ALPHA * h_ref[...] + o_ref[...], lnw_ref[...], lnb_ref[...])


def _ffn_ln(h, w_gate, w_up, w_down, layer, ln_w, ln_b, tm=2 * TM, tf=256):
    m, c = h.shape
    d_ff = w_gate.shape[2]
    vec = pl.BlockSpec((1, c), lambda i, f: (0, 0))
    return pl.pallas_call(
        _ffn_kernel,
        grid=(m // tm, d_ff // tf),
        in_specs=[pl.BlockSpec((tm, c), lambda i, f: (i, 0)),
                  pl.BlockSpec((None, c, tf), lambda i, f: (layer, 0, f)),
                  pl.BlockSpec((None, c, tf), lambda i, f: (layer, 0, f)),
                  pl.BlockSpec((None, tf, c), lambda i, f: (layer, f, 0)), vec, vec],
        out_specs=pl.BlockSpec((tm, c), lambda i, f: (i, 0)),
        out_shape=jax.ShapeDtypeStruct((m, c), F32),
        compiler_params=_params("parallel", "arbitrary", vmem_limit_bytes=FFN_VMEM_LIMIT_BYTES),
        name="swiglu_ffn_ln",
    )(h, w_gate, w_up, w_down, ln_w, ln_b)


def _kv_kernel(x_ref, wk_ref, wvt_ref, k_ref, mean_ref, vt_ref, xb_ref):
    @pl.when(pl.program_id(1) == 0)
    def _():
        xb_ref[...] = x_ref[...].astype(BF16)

    xb = xb_ref[...]
    y = jnp.dot(xb, wk_ref[...], preferred_element_type=F32)
    k_ref[...] = y.astype(k_ref.dtype)
    tm, tn = y.shape
    nblk = tm // MOBA_BLOCK
    mean_ref[0] = jnp.sum(y.reshape(nblk, MOBA_BLOCK, tn), axis=1) * (1.0 / MOBA_BLOCK)
    vt = lax.dot_general(wvt_ref[...], xb, (((1,), (1,)), ((), ())), preferred_element_type=F32)
    for blk in range(nblk):
        vt_ref[blk] = vt[:, blk * MOBA_BLOCK:(blk + 1) * MOBA_BLOCK].astype(vt_ref.dtype)


def _kv_proj(h, w_k, w_v_t, tm=2 * TM, tn=512):
    m, c = h.shape
    n = w_k.shape[1]
    nblk = tm // MOBA_BLOCK
    return pl.pallas_call(
        _kv_kernel,
        grid=(m // tm, n // tn),
        in_specs=[pl.BlockSpec((tm, c), lambda i, j: (i, 0)),
                  pl.BlockSpec((c, tn), lambda i, j: (0, j)),
                  pl.BlockSpec((tn, c), lambda i, j: (j, 0))],
        out_specs=[pl.BlockSpec((tm, tn), lambda i, j: (i, j)),
                   pl.BlockSpec((1, nblk, tn), lambda i, j: (i, 0, j)),
                   pl.BlockSpec((nblk, tn, MOBA_BLOCK), lambda i, j: (i, j, 0))],
        out_shape=[jax.ShapeDtypeStruct((m, n), BF16),
                   jax.ShapeDtypeStruct((m // tm, nblk, n), F32),
                   jax.ShapeDtypeStruct((m // MOBA_BLOCK, n, MOBA_BLOCK), BF16)],
        scratch_shapes=[pltpu.VMEM((tm, c), BF16)],
        compiler_params=_params("parallel", "arbitrary"),
        name="shared_kv_proj",
    )(h, w_k, w_v_t)


def _q_kernel(x_ref, w_ref, o_ref, xb_ref):
    @pl.when(pl.program_id(1) == 0)
    def _():
        xb_ref[...] = x_ref[...].astype(BF16)

    o_ref[...] = jnp.dot(xb_ref[...], w_ref[...], preferred_element_type=F32)


def _q_proj(h, w_q, tm=2 * TM, tn=1024):
    m, c = h.shape
    n = w_q.shape[1]
    return pl.pallas_call(
        _q_kernel,
        grid=(m // tm, n // tn),
        in_specs=[pl.BlockSpec((tm, c), lambda i, j: (i, 0)),
                  pl.BlockSpec((c, tn), lambda i, j: (0, j))],
        out_specs=pl.BlockSpec((tm, tn), lambda i, j: (i, j)),
        out_shape=jax.ShapeDtypeStruct((m, n), F32),
        scratch_shapes=[pltpu.VMEM((tm, c), BF16)],
        compiler_params=_params("parallel", "arbitrary"),
        name="moba_q_proj",
    )(h, w_q)


def _moba_kernel(q_ref, k_ref, vt_ref, km_ref, o_ref, sel_ref, *, n_blocks, top_k, heads):
    blk = pl.program_id(2)
    BLK, DH = MOBA_BLOCK, MOBA_HEAD_DIM
    start = pl.multiple_of(blk * BLK, BLK)
    ki = lax.broadcasted_iota(jnp.int32, (BLK, BLK), 0)
    qi = lax.broadcasted_iota(jnp.int32, (BLK, BLK), 1)
    bi = lax.broadcasted_iota(jnp.int32, (n_blocks, BLK), 0)
    cols = [slice(hh * DH, (hh + 1) * DH) for hh in range(heads)]

    R = range(heads)
    q = [q_ref[:, cols[h]] for h in R]
    qb = [(q[h] * DH ** -0.5).astype(BF16) for h in R]

    gate = [_dot_nt_split(km_ref[0, :, cols[h]], q[h]) for h in R]
    rank = [jnp.zeros((n_blocks, BLK), jnp.int32) for h in R]
    for mm in range(n_blocks):
        past = (mm < blk).astype(jnp.int32)
        for h in R:
            gm = gate[h][mm:mm + 1, :]
            ahead = (gm > gate[h]) | ((gm == gate[h]) & (mm < bi))
            rank[h] = rank[h] + jnp.where(ahead, 1, 0) * past
    for h in R:
        sel_ref[h] = jnp.where((bi < blk) & (rank[h] < top_k), 1.0, 0.0)

    s = [_dot_nt(k_ref[pl.ds(start, BLK), cols[h]], qb[h]) for h in R]
    s = [jnp.where(ki <= qi, s[h], NEG) for h in R]
    ones_rows = jnp.ones((ONES_ROWS, BLK), BF16)
    v_ext = lambda n, h: jnp.concatenate([vt_ref[n, cols[h], :], ones_rows], axis=0)
    m0 = [jnp.max(s[h], axis=0, keepdims=True) for h in R]
    p = [jnp.exp((s[h] - m0[h]).astype(BF16)) for h in R]
    acc0 = [jnp.dot(v_ext(blk, h), p[h], preferred_element_type=F32) for h in R]

    def body(n, carry):
        m_i, acc = carry
        off = pl.multiple_of(n * BLK, BLK)
        sn = [_dot_nt(k_ref[pl.ds(off, BLK), cols[h]], qb[h]) for h in R]
        sn = [jnp.where(sel_ref[h, pl.ds(n, 1), :] > 0.0, sn[h], NEG) for h in R]
        m_new = [jnp.maximum(m_i[h], jnp.max(sn[h], axis=0, keepdims=True)) for h in R]
        corr = [jnp.exp(m_i[h] - m_new[h]) for h in R]
        pn = [jnp.exp((sn[h] - m_new[h]).astype(BF16)) for h in R]
        pv = [jnp.dot(v_ext(n, h), pn[h], preferred_element_type=F32) for h in R]
        acc_new = [corr[h] * acc[h] + pv[h] for h in R]
        return m_new, acc_new

    _, acc_f = lax.fori_loop(0, blk, body, (m0, acc0))
    for h in R:
        out = acc_f[h][:DH] / acc_f[h][DH:DH + 1]
        o_ref[:, cols[h]] = out.T.astype(o_ref.dtype)


def _moba_attention(q, k, v_t, k_mean, batch, seq_len, heads=MOBA_HEADS_PER_STEP):
    m, hd = q.shape
    n_blocks = seq_len // MOBA_BLOCK
    top_k = max(1, min(MOBA_TOPK, n_blocks - 1))
    BLK, W = MOBA_BLOCK, heads * MOBA_HEAD_DIM
    return pl.pallas_call(
        functools.partial(_moba_kernel, n_blocks=n_blocks, top_k=top_k, heads=heads),
        grid=(batch, hd // W, n_blocks),
        in_specs=[pl.BlockSpec((BLK, W), lambda b, h, t: (b * n_blocks + t, h)),
                  pl.BlockSpec((seq_len, W), lambda b, h, t: (b, h)),
                  pl.BlockSpec((n_blocks, W, BLK), lambda b, h, t: (b, h, 0)),
                  pl.BlockSpec((1, n_blocks, W), lambda b, h, t: (b, 0, h))],
        out_specs=pl.BlockSpec((BLK, W), lambda b, h, t: (b * n_blocks + t, h)),
        out_shape=jax.ShapeDtypeStruct((m, hd), BF16),
        scratch_shapes=[pltpu.VMEM((heads, n_blocks, BLK), F32)],
        compiler_params=_params("parallel", "parallel", "arbitrary"),
        name="moba_attention",
    )(q, k, v_t, k_mean)


def _pad_lora(w_in, w_out):
    rank = w_in.shape[1]
    pad = -rank % LORA_PAD
    return (jnp.pad(w_in, ((0, 0), (0, pad))).astype(BF16),
            jnp.pad(w_out, ((0, pad), (0, 0))).astype(BF16))


def kernel(x, a_mix, a_w_r, a_w_k, a_w_v, a_w_o, a_w0, a_w1, a_w2, a_a0, a_a1, a_a2, a_g1, a_g2,
           a_k_k, a_k_a, a_r_k, a_lnx_w, a_lnx_b, kv_w_k, kv_w_v, b_w_q, b_w_o, ffn_w_gate,
           ffn_w_up, ffn_w_down, ln1_w, ln1_b, ln2_w, ln2_b):
    batch, seq_len, c = x.shape
    n_a = a_mix.shape[0]
    n_layers = ffn_w_gate.shape[0]
    assert seq_len % MOBA_BLOCK == 0 and seq_len % (2 * TM) == 0 and c == D_MODEL
    h = x.reshape(batch * seq_len, c)
    row = lambda vec: vec.reshape(1, c)
    k_all = v_t = k_mean = None

    for layer in range(n_layers):
        if layer < n_a:
            i = layer
            w1, w2 = _pad_lora(a_w1[i], a_w2[i])
            a1, a2 = _pad_lora(a_a1[i], a_a2[i])
            g1, g2 = _pad_lora(a_g1[i], a_g2[i])
            *ops, bonus, g, gamma = _rwkv_front(
                h, a_mix[i], a_w_r[i].astype(BF16), a_w_k[i].astype(BF16), a_w_v[i].astype(BF16),
                w1, w2, row(a_w0[i]), a1, a2, row(a_a0[i]), g1, g2, row(a_k_k[i]),
                row(a_k_a[i]), row(a_r_k[i]), seq_len, SCAN_CHUNK)
            y = _rwkv_scan(ops, gamma, batch, seq_len).reshape(batch * seq_len, c)
            h = _gn_proj_ln(y, bonus, g, row(a_lnx_w[i]), row(a_lnx_b[i]),
                            a_w_o[i].astype(BF16), h, row(ln1_w[layer]), row(ln1_b[layer]))
        else:
            j = layer - n_a
            q = _q_proj(h, b_w_q[j].astype(BF16))
            z = _moba_attention(q, k_all, v_t, k_mean, batch, seq_len)
            h = _proj_ln(z, b_w_o[j].astype(BF16), h, row(ln1_w[layer]), row(ln1_b[layer]))
        h = _ffn_ln(h, ffn_w_gate, ffn_w_up, ffn_w_down, layer, row(ln2_w[layer]),
                    row(ln2_b[layer]))
        if layer == n_a - 1:
            k_all, k_mean, v_t = _kv_proj(h, kv_w_k.astype(BF16), kv_w_v.astype(BF16).T)
            k_mean = k_mean.reshape(batch, seq_len // MOBA_BLOCK, -1)
    return h.reshape(batch, seq_len, c)
```

```python
import functools

import jax
import jax.numpy as jnp
from jax import lax
from jax.experimental import pallas as pl
from jax.experimental.pallas import tpu as pltpu

D_MODEL = 2048
RWKV_HEAD = 64
MOBA_HEAD_DIM = 128
MOBA_HEADS = D_MODEL // MOBA_HEAD_DIM
MOBA_BLOCK = 256
MOBA_TOPK = 3
NEG = -1e30
GN_EPS = 64e-5
LN_EPS = 1e-5
DEPTH = 2
ALPHA = (2 * DEPTH) ** 0.25
LORA_PAD = 128
NEG_EXP_MINUS_HALF = -0.6065306597126334

VMEM_LIMIT_BYTES = 56 * 1024 * 1024
FFN_VMEM_LIMIT_BYTES = 60000 * 1024

TM = 512
SCAN_CHUNK = 64
SCAN_HEADS = 4
SCAN_GROUPS = 8
SCAN_BATCHES = 2
MOBA_HEADS_PER_STEP = 16
ONES_ROWS = 16
EW_ROWS, EW_LANES = 512, 512
SEG_LANES = 256

F32 = jnp.float32
BF16 = jnp.bfloat16


def _params(*semantics, vmem_limit_bytes=VMEM_LIMIT_BYTES):
    return pltpu.CompilerParams(dimension_semantics=semantics, vmem_limit_bytes=vmem_limit_bytes)


def _dot(a, b):
    return jnp.dot(a.astype(BF16), b.astype(BF16), preferred_element_type=F32)


def _dot_nt(a, b):
    return lax.dot_general(a.astype(BF16), b.astype(BF16), (((1,), (1,)), ((), ())),
                           preferred_element_type=F32)


def _dot_nt_split(a, b):
    a_hi = a.astype(BF16).astype(F32)
    b_hi = b.astype(BF16)
    b_lo = (b - b_hi.astype(F32)).astype(BF16)
    rows = a.shape[0]
    both = _dot_nt(jnp.concatenate([a_hi, a - a_hi], axis=0), b_hi)
    return both[:rows] + both[rows:] + _dot_nt(a_hi, b_lo)


def _seg_sum(x, seg_ones, split=True):
    width = seg_ones.shape[0]
    parts = []
    for c0 in range(0, x.shape[1], width):
        xs = x[:, c0:c0 + width]
        hi = xs.astype(BF16)
        acc = jnp.dot(hi, seg_ones, preferred_element_type=F32)
        if split:
            lo = (xs - hi.astype(F32)).astype(BF16)
            acc = acc + jnp.dot(lo, seg_ones, preferred_element_type=F32)
        parts.append(acc)
    return parts[0] if len(parts) == 1 else jnp.concatenate(parts, axis=1)


def _seg_ones():
    head = jnp.arange(SEG_LANES) // RWKV_HEAD
    return (head[:, None] == head[None, :]).astype(BF16)


def _layernorm(t, w, b):
    mu = jnp.mean(t, axis=-1, keepdims=True)
    d = t - mu
    var = jnp.mean(d * d, axis=-1, keepdims=True)
    return d * lax.rsqrt(var + LN_EPS) * w + b


def _token_shift_delta(x_ref, prev_ref, seq_start):
    x = x_ref[...]
    rolled = pltpu.roll(x, 1, 0)
    prev_last = jnp.where(seq_start, 0.0, prev_ref[7:8, :])
    row = lax.broadcasted_iota(jnp.int32, x.shape, 0)
    x_prev = jnp.where(row == 0, prev_last, rolled)
    return x, x_prev - x


def _front_kernel(x_ref, prev_ref, mix_ref, wr_ref, wk_ref, wv_ref, w1_ref, w2_ref, w0_ref,
                  a1_ref, a2_ref, a0_ref, g1_ref, g2_ref, kk_ref, ka_ref, rk_ref, ones_ref,
                  at_ref, rt_ref, bt_ref, kt_ref, vb_ref, bonus_ref, g_ref,
                  gam_ref, xr_ref, xk_ref, xv_ref, hw_ref, ha_ref, hg_ref,
                  *, tiles_per_seq, chunk):
    @pl.when(pl.program_id(1) == 0)
    def _():
        seq_start = (pl.program_id(0) % tiles_per_seq) == 0
        x, xx = _token_shift_delta(x_ref, prev_ref, seq_start)
        mixed = lambda idx: x + xx * mix_ref[idx:idx + 1, :]
        xr_ref[...] = mixed(0).astype(BF16)
        xk_ref[...] = mixed(2).astype(BF16)
        xv_ref[...] = mixed(3).astype(BF16)
        hw_ref[...] = jnp.tanh(_dot(mixed(1), w1_ref[...])).astype(BF16)
        ha_ref[...] = _dot(mixed(4), a1_ref[...]).astype(BF16)
        hg_ref[...] = jax.nn.sigmoid(_dot(mixed(5), g1_ref[...])).astype(BF16)

    dot = functools.partial(jnp.dot, preferred_element_type=F32)
    rows, lanes = at_ref.shape
    n_chunks = rows // chunk
    blocks = [slice(c0, c0 + SEG_LANES) for c0 in range(0, lanes, SEG_LANES)]
    xr, xk, xv, hw, ha, hg = (ref[...] for ref in (xr_ref, xk_ref, xv_ref, hw_ref, ha_ref, hg_ref))
    r = [dot(xr, wr_ref[:, cb]) for cb in blocks]
    k = [dot(xk, wk_ref[:, cb]) for cb in blocks]
    v = [dot(xv, wv_ref[:, cb]) for cb in blocks]
    w_pre = [w0_ref[:, cb] + dot(hw, w2_ref[:, cb]) for cb in blocks]
    a_pre = [a0_ref[:, cb] + dot(ha, a2_ref[:, cb]) for cb in blocks]
    for cb in blocks:
        g_ref[:, cb] = dot(hg, g2_ref[:, cb])

    ones = ones_ref[...]
    pos = lax.broadcasted_iota(jnp.int32, (rows, SEG_LANES), 0) % chunk
    for idx, cb in enumerate(blocks):
        ld = jax.nn.sigmoid(w_pre[idx]) * NEG_EXP_MINUS_HALF
        a = jax.nn.sigmoid(a_pre[idx])
        kk = k[idx] * kk_ref[:, cb]
        kk = kk * lax.rsqrt(jnp.maximum(_seg_sum(kk * kk, ones), 1e-24))
        k_mod = k[idx] * (1.0 + (a - 1.0) * ka_ref[:, cb])
        b_vec = kk * a

        cs = ld
        step = 1
        while step < chunk:
            cs = cs + jnp.where(pos >= step, pltpu.roll(cs, step, 0), 0.0)
            step *= 2
        cs3 = cs.reshape(n_chunks, chunk, SEG_LANES)
        cs_last = cs3[:, chunk - 1:chunk, :]
        e_neg = jnp.exp(-cs)

        at_ref[:, cb] = (-kk * jnp.exp(cs - ld)).astype(BF16)
        rt_ref[:, cb] = (r[idx] * jnp.exp(cs)).astype(BF16)
        bt_ref[:, cb] = (b_vec * e_neg).astype(BF16)
        kt_ref[:, cb] = (k_mod * e_neg).astype(BF16)
        vb_ref[:, cb] = v[idx].astype(BF16)
        bonus_ref[:, cb] = _seg_sum(r[idx] * k_mod * rk_ref[:, cb], ones, split=False) * v[idx]
        gam_ref[:, cb] = jnp.exp(cs_last).reshape(n_chunks, SEG_LANES)


def _rwkv_front(h, mix, w_r, w_k, w_v, w1, w2, w0, a1, a2, a0, g1, g2, k_k, k_a, r_k,
                seq_len, chunk, rows=EW_ROWS, lanes=EW_LANES):
    m, c = h.shape
    full = lambda arr: pl.BlockSpec(arr.shape, lambda i, j: (0,) * arr.ndim)
    cols = lambda arr: pl.BlockSpec((arr.shape[0], lanes), lambda i, j: (0, j))
    tile = pl.BlockSpec((rows, lanes), lambda i, j: (i, j))
    ones = _seg_ones()
    bf = jax.ShapeDtypeStruct((m, c), BF16)
    f32 = jax.ShapeDtypeStruct((m, c), F32)
    return pl.pallas_call(
        functools.partial(_front_kernel, tiles_per_seq=seq_len // rows, chunk=chunk),
        grid=(m // rows, c // lanes),
        in_specs=[pl.BlockSpec((rows, c), lambda i, j: (i, 0)),
                  pl.BlockSpec((8, c), lambda i, j: (jnp.maximum(i * (rows // 8) - 1, 0), 0)),
                  full(mix), cols(w_r), cols(w_k), cols(w_v),
                  full(w1), cols(w2), cols(w0), full(a1), cols(a2), cols(a0),
                  full(g1), cols(g2), cols(k_k), cols(k_a), cols(r_k), full(ones)],
        out_specs=[tile] * 7 + [pl.BlockSpec((rows // chunk, lanes), lambda i, j: (i, j))],
        out_shape=[bf] * 5 + [f32, f32, jax.ShapeDtypeStruct((m // chunk, c), F32)],
        scratch_shapes=[pltpu.VMEM((rows, c), BF16)] * 3
        + [pltpu.VMEM((rows, w1.shape[1]), BF16), pltpu.VMEM((rows, a1.shape[1]), BF16),
           pltpu.VMEM((rows, g1.shape[1]), BF16)],
        compiler_params=_params("parallel", "arbitrary"),
        name="rwkv_front",
    )(h, h, mix, w_r, w_k, w_v, w1, w2, w0, a1, a2, a0, g1, g2, k_k, k_a, r_k, ones)


def _scan_kernel(at_ref, rt_ref, bt_ref, kt_ref, v_ref, gam_ref, y_ref, st_ref,
                 *, chunk, heads, groups, batches):
    L, G, N = chunk, heads, RWKV_HEAD
    GN, GL = G * N, G * L
    assert L == N

    @pl.when(pl.program_id(2) == 0)
    def _():
        st_ref[...] = jnp.zeros_like(st_ref)

    seg_mask = (lax.broadcasted_iota(jnp.int32, (GN, GN), 0) // N
                == lax.broadcasted_iota(jnp.int32, (GN, GN), 1) // N)
    expand_mask = (lax.broadcasted_iota(jnp.int32, (GL, GN), 0) // L
                   == lax.broadcasted_iota(jnp.int32, (GL, GN), 1) // N)
    row_t = lax.broadcasted_iota(jnp.int32, (L, GN), 0)
    lane_s = lax.broadcasted_iota(jnp.int32, (L, GN), 1) % L
    strict_c = lane_s < row_t
    incl_c = lane_s <= row_t
    eye_c = (lane_s == row_t).astype(F32)

    def expand(x):
        xb = x.astype(BF16)
        return jnp.where(expand_mask, jnp.concatenate([xb] * G, axis=0), jnp.zeros((), BF16))

    def compact_dot(x, w_bd):
        return jnp.dot(x.astype(BF16), w_bd, preferred_element_type=F32)

    R = range(batches * groups)
    where = [(ci // groups, slice((ci % groups) * GN, (ci % groups + 1) * GN)) for ci in R]
    load = lambda ref: [ref[bb, :, cols] for bb, cols in where]
    at, rt, bt, kt, v = (load(ref) for ref in (at_ref, rt_ref, bt_ref, kt_ref, v_ref))
    s0 = [st_ref[gi] for gi in R]
    gam = [gam_ref[bb, 0, :, cols] for bb, cols in where]

    ar = [jnp.concatenate([at[gi], rt[gi]], axis=0) for gi in R]
    bt_e = [expand(x) for x in bt]
    kt_e = [expand(x) for x in kt]
    v_e = [expand(x) for x in v]
    gram_b = [_dot_nt(ar[gi], bt_e[gi]) for gi in R]
    gram_k = [_dot_nt(ar[gi], kt_e[gi]) for gi in R]
    a_ab = [jnp.where(strict_c, gram_b[gi][:L], 0.0) for gi in R]
    a_rb = [jnp.where(incl_c, gram_b[gi][L:], 0.0) for gi in R]
    causal_2l = jnp.concatenate([strict_c, incl_c], axis=0)
    a_ak_rk = [jnp.where(causal_2l, gram_k[gi], 0.0) for gi in R]
    from_v = [compact_dot(a_ak_rk[gi], v_e[gi]) for gi in R]

    t_c = [eye_c + a_ab[gi] for gi in R]
    p_bd = [expand(a_ab[gi]) for gi in R]
    p_c = [compact_dot(a_ab[gi], p_bd[gi]) for gi in R]
    p_bd = [expand(p_c[gi]) for gi in R]
    for _ in range(L.bit_length() - 3):
        tp = [compact_dot(jnp.concatenate([t_c[gi], p_c[gi]], axis=0), p_bd[gi]) for gi in R]
        t_c = [t_c[gi] + tp[gi][:L] for gi in R]
        p_c = [tp[gi][L:] for gi in R]
        p_bd = [expand(p_c[gi]) for gi in R]
    t_c = [t_c[gi] + compact_dot(t_c[gi], p_bd[gi]) for gi in R]

    s0b = [x.astype(BF16) for x in s0]
    from_state = [_dot_nt(ar[gi], s0b[gi]) for gi in R]
    rhs = [from_state[gi][:L] + from_v[gi][:L] for gi in R]
    u = [compact_dot(t_c[gi], expand(rhs[gi])) for gi in R]
    y = [from_state[gi][L:] + from_v[gi][L:] + compact_dot(a_rb[gi], expand(u[gi]))
         for gi in R]
    uv = [jnp.concatenate([u[gi], v[gi].astype(F32)], axis=0) for gi in R]
    bk = [jnp.concatenate([bt[gi], kt[gi]], axis=0) for gi in R]
    upd = [_dot(uv[gi].T, bk[gi]) for gi in R]
    for gi, (bb, cols) in enumerate(where):
        y_ref[bb, :, cols] = y[gi]
        st_ref[gi] = (s0[gi] + jnp.where(seg_mask, upd[gi], 0.0)) * gam[gi]


def _rwkv_scan(ops, gamma, batch, seq_len, chunk=SCAN_CHUNK, heads=SCAN_HEADS,
               groups=SCAN_GROUPS, batches=SCAN_BATCHES):
    c = gamma.shape[-1]
    gn = heads * RWKV_HEAD
    lanes = groups * gn
    n_chunks = seq_len // chunk
    seq = pl.BlockSpec((batches, chunk, lanes), lambda b, hg, t: (b, t, hg))
    gam = pl.BlockSpec((batches, 1, 1, lanes), lambda b, hg, t: (b, t, 0, hg))
    return pl.pallas_call(
        functools.partial(_scan_kernel, chunk=chunk, heads=heads, groups=groups,
                          batches=batches),
        grid=(batch // batches, c // lanes, n_chunks),
        in_specs=[seq] * 5 + [gam],
        out_specs=seq,
        out_shape=jax.ShapeDtypeStruct((batch, seq_len, c), F32),
        scratch_shapes=[pltpu.VMEM((batches * groups, gn, gn), F32)],
        compiler_params=_params("parallel", "parallel", "arbitrary"),
        name="rwkv_scan",
    )(*[o.reshape(batch, seq_len, c) for o in ops], gamma.reshape(batch, n_chunks, 1, c))


def _gn_proj_ln_kernel(y_ref, bonus_ref, g_ref, gnw_ref, gnb_ref, ones_ref, w_ref, h_ref,
                       lnw_ref, lnb_ref, o_ref):
    ones = ones_ref[...]
    inv_n = 1.0 / RWKV_HEAD
    half = y_ref.shape[0] // 2
    halves = [slice(0, half), slice(half, 2 * half)]

    def gated(rs):
        z = []
        for c0 in range(0, y_ref.shape[1], SEG_LANES):
            cb = slice(c0, c0 + SEG_LANES)
            y = y_ref[rs, cb]
            mu = _seg_sum(y, ones) * inv_n
            d = y - mu
            var = _seg_sum(d * d, ones) * inv_n
            yn = d * lax.rsqrt(var + GN_EPS) * gnw_ref[:, cb] + gnb_ref[:, cb]
            z.append(((yn + bonus_ref[rs, cb]) * g_ref[rs, cb]).astype(BF16))
        return jnp.concatenate(z, axis=1)

    z = [gated(rs) for rs in halves]
    proj = [jnp.dot(zh, w_ref[...], preferred_element_type=F32) for zh in z]
    for idx, rs in enumerate(halves):
        o_ref[rs, :] = _layernorm(ALPHA * h_ref[rs, :] + proj[idx], lnw_ref[...], lnb_ref[...])


def _gn_proj_ln(y, bonus, g, gn_w, gn_b, w, h, ln_w, ln_b, tm=TM // 2):
    m, c = y.shape
    tile = pl.BlockSpec((tm, c), lambda i: (i, 0))
    vec = pl.BlockSpec((1, c), lambda i: (0, 0))
    return pl.pallas_call(
        _gn_proj_ln_kernel,
        grid=(m // tm,),
        in_specs=[tile, tile, tile, vec, vec,
                  pl.BlockSpec((SEG_LANES, SEG_LANES), lambda i: (0, 0)),
                  pl.BlockSpec((c, c), lambda i: (0, 0)), tile, vec, vec],
        out_specs=tile,
        out_shape=jax.ShapeDtypeStruct((m, c), F32),
        compiler_params=_params("parallel"),
        name="rwkv_gn_proj_ln",
    )(y, bonus, g, gn_w, gn_b, _seg_ones(), w, h, ln_w, ln_b)


def _proj_ln_kernel(z_ref, w_ref, h_ref, lnw_ref, lnb_ref, o_ref):
    half = z_ref.shape[0] // 2
    halves = [slice(0, half), slice(half, 2 * half)]
    y = [jnp.dot(z_ref[rs, :], w_ref[...], preferred_element_type=F32) for rs in halves]
    for idx, rs in enumerate(halves):
        o_ref[rs, :] = _layernorm(ALPHA * h_ref[rs, :] + y[idx], lnw_ref[...], lnb_ref[...])


def _proj_ln(z, w, h, ln_w, ln_b, tm=TM):
    m, kdim = z.shape
    c = w.shape[1]
    vec = pl.BlockSpec((1, c), lambda i: (0, 0))
    return pl.pallas_call(
        _proj_ln_kernel,
        grid=(m // tm,),
        in_specs=[pl.BlockSpec((tm, kdim), lambda i: (i, 0)),
                  pl.BlockSpec((kdim, c), lambda i: (0, 0)),
                  pl.BlockSpec((tm, c), lambda i: (i, 0)), vec, vec],
        out_specs=pl.BlockSpec((tm, c), lambda i: (i, 0)),
        out_shape=jax.ShapeDtypeStruct((m, c), F32),
        compiler_params=_params("parallel"),
        name="proj_residual_ln",
    )(z, w, h, ln_w, ln_b)


def _ffn_kernel(h_ref, wg_ref, wu_ref, wd_ref, lnw_ref, lnb_ref, o_ref):
    f = pl.program_id(1)

    @pl.when(f == 0)
    def _():
        o_ref[...] = jnp.zeros_like(o_ref)

    x = h_ref[...].astype(BF16)
    gate = jnp.dot(x, wg_ref[...].astype(BF16), preferred_element_type=F32)
    up = jnp.dot(x, wu_ref[...].astype(BF16), preferred_element_type=F32)
    act = (gate * jax.nn.sigmoid(gate)) * up
    o_ref[...] += jnp.dot(act.astype(BF16), wd_ref[...].astype(BF16),
                          preferred_element_type=F32)

    @pl.when(f == pl.num_programs(1) - 1)
    def _():
        o_ref[...] = _layernorm(ALPHA * h_ref[...] + o_ref[...], lnw_ref[...], lnb_ref[...])


def _ffn_ln(h, w_gate, w_up, w_down, layer, ln_w, ln_b, tm=2 * TM, tf=256):
    m, c = h.shape
    d_ff = w_gate.shape[2]
    vec = pl.BlockSpec((1, c), lambda i, f: (0, 0))
    return pl.pallas_call(
        _ffn_kernel,
        grid=(m // tm, d_ff // tf),
        in_specs=[pl.BlockSpec((tm, c), lambda i, f: (i, 0)),
                  pl.BlockSpec((None, c, tf), lambda i, f: (layer, 0, f)),
                  pl.BlockSpec((None, c, tf), lambda i, f: (layer, 0, f)),
                  pl.BlockSpec((None, tf, c), lambda i, f: (layer, f, 0)), vec, vec],
        out_specs=pl.BlockSpec((tm, c), lambda i, f: (i, 0)),
        out_shape=jax.ShapeDtypeStruct((m, c), F32),
        compiler_params=_params("parallel", "arbitrary", vmem_limit_bytes=FFN_VMEM_LIMIT_BYTES),
        name="swiglu_ffn_ln",
    )(h, w_gate, w_up, w_down, ln_w, ln_b)


def _kv_kernel(x_ref, wk_ref, wvt_ref, k_ref, mean_ref, vt_ref, xb_ref):
    @pl.when(pl.program_id(1) == 0)
    def _():
        xb_ref[...] = x_ref[...].astype(BF16)

    xb = xb_ref[...]
    y = jnp.dot(xb, wk_ref[...], preferred_element_type=F32)
    k_ref[...] = y.astype(k_ref.dtype)
    tm, tn = y.shape
    nblk = tm // MOBA_BLOCK
    mean_ref[0] = jnp.sum(y.reshape(nblk, MOBA_BLOCK, tn), axis=1) * (1.0 / MOBA_BLOCK)
    vt = lax.dot_general(wvt_ref[...], xb, (((1,), (1,)), ((), ())), preferred_element_type=F32)
    for blk in range(nblk):
        vt_ref[blk] = vt[:, blk * MOBA_BLOCK:(blk + 1) * MOBA_BLOCK].astype(vt_ref.dtype)


def _kv_proj(h, w_k, w_v_t, tm=2 * TM, tn=512):
    m, c = h.shape
    n = w_k.shape[1]
    nblk = tm // MOBA_BLOCK
    return pl.pallas_call(
        _kv_kernel,
        grid=(m // tm, n // tn),
        in_specs=[pl.BlockSpec((tm, c), lambda i, j: (i, 0)),
                  pl.BlockSpec((c, tn), lambda i, j: (0, j)),
                  pl.BlockSpec((tn, c), lambda i, j: (j, 0))],
        out_specs=[pl.BlockSpec((tm, tn), lambda i, j: (i, j)),
                   pl.BlockSpec((1, nblk, tn), lambda i, j: (i, 0, j)),
                   pl.BlockSpec((nblk, tn, MOBA_BLOCK), lambda i, j: (i, j, 0))],
        out_shape=[jax.ShapeDtypeStruct((m, n), BF16),
                   jax.ShapeDtypeStruct((m // tm, nblk, n), F32),
                   jax.ShapeDtypeStruct((m // MOBA_BLOCK, n, MOBA_BLOCK), BF16)],
        scratch_shapes=[pltpu.VMEM((tm, c), BF16)],
        compiler_params=_params("parallel", "arbitrary"),
        name="shared_kv_proj",
    )(h, w_k, w_v_t)


def _q_kernel(x_ref, w_ref, o_ref, xb_ref):
    @pl.when(pl.program_id(1) == 0)
    def _():
        xb_ref[...] = x_ref[...].astype(BF16)

    o_ref[...] = jnp.dot(xb_ref[...], w_ref[...], preferred_element_type=F32)


def _q_proj(h, w_q, tm=2 * TM, tn=1024):
    m, c = h.shape
    n = w_q.shape[1]
    return pl.pallas_call(
        _q_kernel,
        grid=(m // tm, n // tn),
        in_specs=[pl.BlockSpec((tm, c), lambda i, j: (i, 0)),
                  pl.BlockSpec((c, tn), lambda i, j: (0, j))],
        out_specs=pl.BlockSpec((tm, tn), lambda i, j: (i, j)),
        out_shape=jax.ShapeDtypeStruct((m, n), F32),
        scratch_shapes=[pltpu.VMEM((tm, c), BF16)],
        compiler_params=_params("parallel", "arbitrary"),
        name="moba_q_proj",
    )(h, w_q)


def _moba_kernel(q_ref, k_ref, vt_ref, km_ref, o_ref, sel_ref, *, n_blocks, top_k, heads):
    blk = pl.program_id(2)
    BLK, DH = MOBA_BLOCK, MOBA_HEAD_DIM
    start = pl.multiple_of(blk * BLK, BLK)
    ki = lax.broadcasted_iota(jnp.int32, (BLK, BLK), 0)
    qi = lax.broadcasted_iota(jnp.int32, (BLK, BLK), 1)
    bi = lax.broadcasted_iota(jnp.int32, (n_blocks, BLK), 0)
    cols = [slice(hh * DH, (hh + 1) * DH) for hh in range(heads)]

    R = range(heads)
    q = [q_ref[:, cols[h]] for h in R]
    qb = [(q[h] * DH ** -0.5).astype(BF16) for h in R]

    gate = [_dot_nt_split(km_ref[0, :, cols[h]], q[h]) for h in R]
    rank = [jnp.zeros((n_blocks, BLK), jnp.int32) for h in R]
    for mm in range(n_blocks):
        past = (mm < blk).astype(jnp.int32)
        for h in R:
            gm = gate[h][mm:mm + 1, :]
            ahead = (gm > gate[h]) | ((gm == gate[h]) & (mm < bi))
            rank[h] = rank[h] + jnp.where(ahead, 1, 0) * past
    for h in R:
        sel_ref[h] = jnp.where((bi < blk) & (rank[h] < top_k), 1.0, 0.0)

    s = [_dot_nt(k_ref[pl.ds(start, BLK), cols[h]], qb[h]) for h in R]
    s = [jnp.where(ki <= qi, s[h], NEG) for h in R]
    ones_rows = jnp.ones((ONES_ROWS, BLK), BF16)
    v_ext = lambda n, h: jnp.concatenate([vt_ref[n, cols[h], :], ones_rows], axis=0)
    m0 = [jnp.max(s[h], axis=0, keepdims=True).astype(BF16).astype(F32) for h in R]
    p = [jnp.exp((s[h] - m0[h]).astype(BF16)) for h in R]
    acc0 = [jnp.dot(v_ext(blk, h), p[h], preferred_element_type=F32) for h in R]

    def body(n, carry):
        m_i, acc = carry
        off = pl.multiple_of(n * BLK, BLK)
        sn = [_dot_nt(k_ref[pl.ds(off, BLK), cols[h]], qb[h]).astype(BF16) for h in R]
        sn = [jnp.where(sel_ref[h, pl.ds(n, 1), :] > 0.0, sn[h], jnp.asarray(NEG, BF16))
              for h in R]
        m_new = [jnp.maximum(m_i[h], jnp.max(sn[h], axis=0, keepdims=True).astype(F32))
                 for h in R]
        corr = [jnp.exp(m_i[h] - m_new[h]) for h in R]
        pn = [jnp.exp(sn[h] - m_new[h].astype(BF16)) for h in R]
        pv = [jnp.dot(v_ext(n, h), pn[h], preferred_element_type=F32) for h in R]
        acc_new = [corr[h] * acc[h] + pv[h] for h in R]
        return m_new, acc_new

    _, acc_f = lax.fori_loop(0, blk, body, (m0, acc0))
    for h in R:
        out = acc_f[h][:DH] / acc_f[h][DH:DH + 1]
        o_ref[:, cols[h]] = out.T.astype(o_ref.dtype)


def _moba_attention(q, k, v_t, k_mean, batch, seq_len, heads=MOBA_HEADS_PER_STEP):
    m, hd = q.shape
    n_blocks = seq_len // MOBA_BLOCK
    top_k = max(1, min(MOBA_TOPK, n_blocks - 1))
    BLK, W = MOBA_BLOCK, heads * MOBA_HEAD_DIM
    return pl.pallas_call(
        functools.partial(_moba_kernel, n_blocks=n_blocks, top_k=top_k, heads=heads),
        grid=(batch, hd // W, n_blocks),
        in_specs=[pl.BlockSpec((BLK, W), lambda b, h, t: (b * n_blocks + t, h)),
                  pl.BlockSpec((seq_len, W), lambda b, h, t: (b, h)),
                  pl.BlockSpec((n_blocks, W, BLK), lambda b, h, t: (b, h, 0)),
                  pl.BlockSpec((1, n_blocks, W), lambda b, h, t: (b, 0, h))],
        out_specs=pl.BlockSpec((BLK, W), lambda b, h, t: (b * n_blocks + t, h)),
        out_shape=jax.ShapeDtypeStruct((m, hd), BF16),
        scratch_shapes=[pltpu.VMEM((heads, n_blocks, BLK), F32)],
        compiler_params=_params("parallel", "parallel", "arbitrary"),
        name="moba_attention",
    )(q, k, v_t, k_mean)


def _pad_lora(w_in, w_out):
    rank = w_in.shape[1]
    pad = -rank % LORA_PAD
    return (jnp.pad(w_in, ((0, 0), (0, pad))).astype(BF16),
            jnp.pad(w_out, ((0, pad), (0, 0))).astype(BF16))


def kernel(x, a_mix, a_w_r, a_w_k, a_w_v, a_w_o, a_w0, a_w1, a_w2, a_a0, a_a1, a_a2, a_g1, a_g2,
           a_k_k, a_k_a, a_r_k, a_lnx_w, a_lnx_b, kv_w_k, kv_w_v, b_w_q, b_w_o, ffn_w_gate,
           ffn_w_up, ffn_w_down, ln1_w, ln1_b, ln2_w, ln2_b):
    batch, seq_len, c = x.shape
    n_a = a_mix.shape[0]
    n_layers = ffn_w_gate.shape[0]
    assert seq_len % MOBA_BLOCK == 0 and seq_len % (2 * TM) == 0 and c == D_MODEL
    h = x.reshape(batch * seq_len, c)
    row = lambda vec: vec.reshape(1, c)
    k_all = v_t = k_mean = None

    for layer in range(n_layers):
        if layer < n_a:
            i = layer
            w1, w2 = _pad_lora(a_w1[i], a_w2[i])
            a1, a2 = _pad_lora(a_a1[i], a_a2[i])
            g1, g2 = _pad_lora(a_g1[i], a_g2[i])
            *ops, bonus, g, gamma = _rwkv_front(
                h, a_mix[i], a_w_r[i].astype(BF16), a_w_k[i].astype(BF16), a_w_v[i].astype(BF16),
                w1, w2, row(a_w0[i]), a1, a2, row(a_a0[i]), g1, g2, row(a_k_k[i]),
                row(a_k_a[i]), row(a_r_k[i]), seq_len, SCAN_CHUNK)
            y = _rwkv_scan(ops, gamma, batch, seq_len).reshape(batch * seq_len, c)
            h = _gn_proj_ln(y, bonus, g, row(a_lnx_w[i]), row(a_lnx_b[i]),
                            a_w_o[i].astype(BF16), h, row(ln1_w[layer]), row(ln1_b[layer]))
        else:
            j = layer - n_a
            q = _q_proj(h, b_w_q[j].astype(BF16))
            z = _moba_attention(q, k_all, v_t, k_mean, batch, seq_len)
            h = _proj_ln(z, b_w_o[j].astype(BF16), h, row(ln1_w[layer]), row(ln1_b[layer]))
        h = _ffn_ln(h, ffn_w_gate, ffn_w_up, ffn_w_down, layer, row(ln2_w[layer]),
                    row(ln2_b[layer]))
        if layer == n_a - 1:
            k_all, k_mean, v_t = _kv_proj(h, kv_w_k.astype(BF16), kv_w_v.T.astype(BF16))
            k_mean = k_mean.reshape(batch, seq_len // MOBA_BLOCK, -1)
    return h.reshape(batch, seq_len, c)
```

```python
import functools

import jax
import jax.numpy as jnp
from jax import lax
from jax.experimental import pallas as pl
from jax.experimental.pallas import tpu as pltpu

D_MODEL = 2048
RWKV_HEAD = 64
MOBA_HEAD_DIM = 128
MOBA_HEADS = D_MODEL // MOBA_HEAD_DIM
MOBA_BLOCK = 256
MOBA_TOPK = 3
NEG = -1e30
GN_EPS = 64e-5
LN_EPS = 1e-5
DEPTH = 2
ALPHA = (2 * DEPTH) ** 0.25
LORA_PAD = 128
NEG_EXP_MINUS_HALF = -0.6065306597126334

VMEM_LIMIT_BYTES = 56 * 1024 * 1024
FFN_VMEM_LIMIT_BYTES = 60000 * 1024

TM = 512
SCAN_CHUNK = 64
SCAN_HEADS = 4
SCAN_GROUPS = 8
SCAN_BATCHES = 2
MOBA_HEADS_PER_STEP = 16
ONES_ROWS = 16
EW_ROWS, EW_LANES = 512, 512
SEG_LANES = 256

F32 = jnp.float32
BF16 = jnp.bfloat16


def _params(*semantics, vmem_limit_bytes=VMEM_LIMIT_BYTES):
    return pltpu.CompilerParams(dimension_semantics=semantics, vmem_limit_bytes=vmem_limit_bytes)


def _dot(a, b):
    return jnp.dot(a.astype(BF16), b.astype(BF16), preferred_element_type=F32)


def _dot_nt(a, b):
    return lax.dot_general(a.astype(BF16), b.astype(BF16), (((1,), (1,)), ((), ())),
                           preferred_element_type=F32)


def _dot_nt_split(a, b):
    a_hi = a.astype(BF16).astype(F32)
    b_hi = b.astype(BF16)
    rows = a.shape[0]
    both = _dot_nt(jnp.concatenate([a_hi, a - a_hi], axis=0), b_hi)
    if b.dtype == BF16:
        return both[:rows] + both[rows:]
    b_lo = (b - b_hi.astype(F32)).astype(BF16)
    return both[:rows] + both[rows:] + _dot_nt(a_hi, b_lo)


def _seg_sum(x, seg_ones, split=True):
    width = seg_ones.shape[0]
    parts = []
    for c0 in range(0, x.shape[1], width):
        xs = x[:, c0:c0 + width]
        hi = xs.astype(BF16)
        acc = jnp.dot(hi, seg_ones, preferred_element_type=F32)
        if split:
            lo = (xs - hi.astype(F32)).astype(BF16)
            acc = acc + jnp.dot(lo, seg_ones, preferred_element_type=F32)
        parts.append(acc)
    return parts[0] if len(parts) == 1 else jnp.concatenate(parts, axis=1)


def _seg_ones():
    head = jnp.arange(SEG_LANES) // RWKV_HEAD
    return (head[:, None] == head[None, :]).astype(BF16)


def _layernorm(t, w, b):
    mu = jnp.mean(t, axis=-1, keepdims=True)
    d = t - mu
    var = jnp.mean(d * d, axis=-1, keepdims=True)
    return d * lax.rsqrt(var + LN_EPS) * w + b


def _token_shift_delta(x_ref, prev_ref, seq_start):
    x = x_ref[...]
    rolled = pltpu.roll(x, 1, 0)
    prev_last = jnp.where(seq_start, 0.0, prev_ref[7:8, :])
    row = lax.broadcasted_iota(jnp.int32, x.shape, 0)
    x_prev = jnp.where(row == 0, prev_last, rolled)
    return x, x_prev - x


def _front_kernel(x_ref, prev_ref, mix_ref, wr_ref, wk_ref, wv_ref, w1_ref, w2_ref, w0_ref,
                  a1_ref, a2_ref, a0_ref, g1_ref, g2_ref, kk_ref, ka_ref, rk_ref, ones_ref,
                  at_ref, rt_ref, bt_ref, kt_ref, vb_ref, bonus_ref, g_ref,
                  gam_ref, xr_ref, xk_ref, xv_ref, hw_ref, ha_ref, hg_ref,
                  *, tiles_per_seq, chunk):
    @pl.when(pl.program_id(1) == 0)
    def _():
        seq_start = (pl.program_id(0) % tiles_per_seq) == 0
        x, xx = _token_shift_delta(x_ref, prev_ref, seq_start)
        mixed = lambda idx: x + xx * mix_ref[idx:idx + 1, :]
        xr_ref[...] = mixed(0).astype(BF16)
        xk_ref[...] = mixed(2).astype(BF16)
        xv_ref[...] = mixed(3).astype(BF16)
        hw_ref[...] = jnp.tanh(_dot(mixed(1), w1_ref[...])).astype(BF16)
        ha_ref[...] = _dot(mixed(4), a1_ref[...]).astype(BF16)
        hg_ref[...] = jax.nn.sigmoid(_dot(mixed(5), g1_ref[...])).astype(BF16)

    dot = functools.partial(jnp.dot, preferred_element_type=F32)
    rows, lanes = at_ref.shape
    n_chunks = rows // chunk
    blocks = [slice(c0, c0 + SEG_LANES) for c0 in range(0, lanes, SEG_LANES)]
    xr, xk, xv, hw, ha, hg = (ref[...] for ref in (xr_ref, xk_ref, xv_ref, hw_ref, ha_ref, hg_ref))
    r = [dot(xr, wr_ref[:, cb]) for cb in blocks]
    k = [dot(xk, wk_ref[:, cb]) for cb in blocks]
    v = [dot(xv, wv_ref[:, cb]) for cb in blocks]
    w_pre = [w0_ref[:, cb] + dot(hw, w2_ref[:, cb]) for cb in blocks]
    a_pre = [a0_ref[:, cb] + dot(ha, a2_ref[:, cb]) for cb in blocks]
    for cb in blocks:
        g_ref[:, cb] = dot(hg, g2_ref[:, cb])

    ones = ones_ref[...]
    pos = lax.broadcasted_iota(jnp.int32, (rows, SEG_LANES), 0) % chunk
    for idx, cb in enumerate(blocks):
        ld = jax.nn.sigmoid(w_pre[idx]) * NEG_EXP_MINUS_HALF
        a = jax.nn.sigmoid(a_pre[idx])
        kk = k[idx] * kk_ref[:, cb]
        kk = kk * lax.rsqrt(jnp.maximum(_seg_sum(kk * kk, ones), 1e-24))
        k_mod = k[idx] * (1.0 + (a - 1.0) * ka_ref[:, cb])
        b_vec = kk * a

        cs = ld
        step = 1
        while step < chunk:
            cs = cs + jnp.where(pos >= step, pltpu.roll(cs, step, 0), 0.0)
            step *= 2
        cs3 = cs.reshape(n_chunks, chunk, SEG_LANES)
        cs_last = cs3[:, chunk - 1:chunk, :]
        e_neg = jnp.exp(-cs)

        at_ref[:, cb] = (-kk * jnp.exp(cs - ld)).astype(BF16)
        rt_ref[:, cb] = (r[idx] * jnp.exp(cs)).astype(BF16)
        bt_ref[:, cb] = (b_vec * e_neg).astype(BF16)
        kt_ref[:, cb] = (k_mod * e_neg).astype(BF16)
        vb_ref[:, cb] = v[idx].astype(BF16)
        bonus_ref[:, cb] = _seg_sum(r[idx] * k_mod * rk_ref[:, cb], ones, split=False) * v[idx]
        gam_ref[:, cb] = jnp.exp(cs_last).reshape(n_chunks, SEG_LANES)


def _rwkv_front(h, mix, w_r, w_k, w_v, w1, w2, w0, a1, a2, a0, g1, g2, k_k, k_a, r_k,
                seq_len, chunk, rows=EW_ROWS, lanes=EW_LANES):
    m, c = h.shape
    full = lambda arr: pl.BlockSpec(arr.shape, lambda i, j: (0,) * arr.ndim)
    cols = lambda arr: pl.BlockSpec((arr.shape[0], lanes), lambda i, j: (0, j))
    tile = pl.BlockSpec((rows, lanes), lambda i, j: (i, j))
    ones = _seg_ones()
    bf = jax.ShapeDtypeStruct((m, c), BF16)
    f32 = jax.ShapeDtypeStruct((m, c), F32)
    return pl.pallas_call(
        functools.partial(_front_kernel, tiles_per_seq=seq_len // rows, chunk=chunk),
        grid=(m // rows, c // lanes),
        in_specs=[pl.BlockSpec((rows, c), lambda i, j: (i, 0)),
                  pl.BlockSpec((8, c), lambda i, j: (jnp.maximum(i * (rows // 8) - 1, 0), 0)),
                  full(mix), cols(w_r), cols(w_k), cols(w_v),
                  full(w1), cols(w2), cols(w0), full(a1), cols(a2), cols(a0),
                  full(g1), cols(g2), cols(k_k), cols(k_a), cols(r_k), full(ones)],
        out_specs=[tile] * 7 + [pl.BlockSpec((rows // chunk, lanes), lambda i, j: (i, j))],
        out_shape=[bf] * 5 + [f32, f32, jax.ShapeDtypeStruct((m // chunk, c), F32)],
        scratch_shapes=[pltpu.VMEM((rows, c), BF16)] * 3
        + [pltpu.VMEM((rows, w1.shape[1]), BF16), pltpu.VMEM((rows, a1.shape[1]), BF16),
           pltpu.VMEM((rows, g1.shape[1]), BF16)],
        compiler_params=_params("parallel", "arbitrary"),
        name="rwkv_front",
    )(h, h, mix, w_r, w_k, w_v, w1, w2, w0, a1, a2, a0, g1, g2, k_k, k_a, r_k, ones)


def _scan_kernel(at_ref, rt_ref, bt_ref, kt_ref, v_ref, gam_ref, y_ref, st_ref,
                 *, chunk, heads, groups, batches):
    L, G, N = chunk, heads, RWKV_HEAD
    GN, GL = G * N, G * L
    assert L == N

    @pl.when(pl.program_id(2) == 0)
    def _():
        st_ref[...] = jnp.zeros_like(st_ref)

    seg_mask = (lax.broadcasted_iota(jnp.int32, (GN, GN), 0) // N
                == lax.broadcasted_iota(jnp.int32, (GN, GN), 1) // N)
    expand_mask = (lax.broadcasted_iota(jnp.int32, (GL, GN), 0) // L
                   == lax.broadcasted_iota(jnp.int32, (GL, GN), 1) // N)
    row_t = lax.broadcasted_iota(jnp.int32, (L, GN), 0)
    lane_s = lax.broadcasted_iota(jnp.int32, (L, GN), 1) % L
    strict_c = lane_s < row_t
    incl_c = lane_s <= row_t
    eye_c = (lane_s == row_t).astype(F32)

    def expand(x):
        xb = x.astype(BF16)
        return jnp.where(expand_mask, jnp.concatenate([xb] * G, axis=0), jnp.zeros((), BF16))

    def compact_dot(x, w_bd):
        return jnp.dot(x.astype(BF16), w_bd, preferred_element_type=F32)

    R = range(batches * groups)
    where = [(ci // groups, slice((ci % groups) * GN, (ci % groups + 1) * GN)) for ci in R]
    load = lambda ref: [ref[bb, :, cols] for bb, cols in where]
    at, rt, bt, kt, v = (load(ref) for ref in (at_ref, rt_ref, bt_ref, kt_ref, v_ref))
    s0 = [st_ref[gi] for gi in R]
    gam = [gam_ref[bb, 0, :, cols] for bb, cols in where]

    ar = [jnp.concatenate([at[gi], rt[gi]], axis=0) for gi in R]
    bt_e = [expand(x) for x in bt]
    kt_e = [expand(x) for x in kt]
    v_e = [expand(x) for x in v]
    gram_b = [_dot_nt(ar[gi], bt_e[gi]) for gi in R]
    gram_k = [_dot_nt(ar[gi], kt_e[gi]) for gi in R]
    a_ab = [jnp.where(strict_c, gram_b[gi][:L], 0.0) for gi in R]
    a_rb = [jnp.where(incl_c, gram_b[gi][L:], 0.0) for gi in R]
    causal_2l = jnp.concatenate([strict_c, incl_c], axis=0)
    a_ak_rk = [jnp.where(causal_2l, gram_k[gi], 0.0) for gi in R]
    from_v = [compact_dot(a_ak_rk[gi], v_e[gi]) for gi in R]

    t_c = [eye_c + a_ab[gi] for gi in R]
    p_bd = [expand(a_ab[gi]) for gi in R]
    p_c = [compact_dot(a_ab[gi], p_bd[gi]) for gi in R]
    p_bd = [expand(p_c[gi]) for gi in R]
    for _ in range(L.bit_length() - 3):
        tp = [compact_dot(jnp.concatenate([t_c[gi], p_c[gi]], axis=0), p_bd[gi]) for gi in R]
        t_c = [t_c[gi] + tp[gi][:L] for gi in R]
        p_c = [tp[gi][L:] for gi in R]
        p_bd = [expand(p_c[gi]) for gi in R]
    t_c = [t_c[gi] + compact_dot(t_c[gi], p_bd[gi]) for gi in R]

    s0b = [x.astype(BF16) for x in s0]
    from_state = [_dot_nt(ar[gi], s0b[gi]) for gi in R]
    rhs = [from_state[gi][:L] + from_v[gi][:L] for gi in R]
    u = [compact_dot(t_c[gi], expand(rhs[gi])) for gi in R]
    y = [from_state[gi][L:] + from_v[gi][L:] + compact_dot(a_rb[gi], expand(u[gi]))
         for gi in R]
    uv = [jnp.concatenate([u[gi], v[gi].astype(F32)], axis=0) for gi in R]
    bk = [jnp.concatenate([bt[gi], kt[gi]], axis=0) for gi in R]
    upd = [_dot(uv[gi].T, bk[gi]) for gi in R]
    for gi, (bb, cols) in enumerate(where):
        y_ref[bb, :, cols] = y[gi]
        st_ref[gi] = (s0[gi] + jnp.where(seg_mask, upd[gi], 0.0)) * gam[gi]


def _rwkv_scan(ops, gamma, batch, seq_len, chunk=SCAN_CHUNK, heads=SCAN_HEADS,
               groups=SCAN_GROUPS, batches=SCAN_BATCHES):
    c = gamma.shape[-1]
    gn = heads * RWKV_HEAD
    lanes = groups * gn
    n_chunks = seq_len // chunk
    seq = pl.BlockSpec((batches, chunk, lanes), lambda b, hg, t: (b, t, hg))
    gam = pl.BlockSpec((batches, 1, 1, lanes), lambda b, hg, t: (b, t, 0, hg))
    return pl.pallas_call(
        functools.partial(_scan_kernel, chunk=chunk, heads=heads, groups=groups,
                          batches=batches),
        grid=(batch // batches, c // lanes, n_chunks),
        in_specs=[seq] * 5 + [gam],
        out_specs=seq,
        out_shape=jax.ShapeDtypeStruct((batch, seq_len, c), F32),
        scratch_shapes=[pltpu.VMEM((batches * groups, gn, gn), F32)],
        compiler_params=_params("parallel", "parallel", "arbitrary"),
        name="rwkv_scan",
    )(*[o.reshape(batch, seq_len, c) for o in ops], gamma.reshape(batch, n_chunks, 1, c))


def _gn_proj_ln_kernel(y_ref, bonus_ref, g_ref, gnw_ref, gnb_ref, ones_ref, w_ref, h_ref,
                       lnw_ref, lnb_ref, o_ref):
    ones = ones_ref[...]
    inv_n = 1.0 / RWKV_HEAD
    half = y_ref.shape[0] // 2
    halves = [slice(0, half), slice(half, 2 * half)]

    def gated(rs):
        z = []
        for c0 in range(0, y_ref.shape[1], SEG_LANES):
            cb = slice(c0, c0 + SEG_LANES)
            y = y_ref[rs, cb]
            mu = _seg_sum(y, ones) * inv_n
            d = y - mu
            var = _seg_sum(d * d, ones) * inv_n
            yn = d * lax.rsqrt(var + GN_EPS) * gnw_ref[:, cb] + gnb_ref[:, cb]
            z.append(((yn + bonus_ref[rs, cb]) * g_ref[rs, cb]).astype(BF16))
        return jnp.concatenate(z, axis=1)

    z = [gated(rs) for rs in halves]
    proj = [jnp.dot(zh, w_ref[...], preferred_element_type=F32) for zh in z]
    for idx, rs in enumerate(halves):
        o_ref[rs, :] = _layernorm(ALPHA * h_ref[rs, :] + proj[idx], lnw_ref[...], lnb_ref[...])


def _gn_proj_ln(y, bonus, g, gn_w, gn_b, w, h, ln_w, ln_b, tm=TM // 2):
    m, c = y.shape
    tile = pl.BlockSpec((tm, c), lambda i: (i, 0))
    vec = pl.BlockSpec((1, c), lambda i: (0, 0))
    return pl.pallas_call(
        _gn_proj_ln_kernel,
        grid=(m // tm,),
        in_specs=[tile, tile, tile, vec, vec,
                  pl.BlockSpec((SEG_LANES, SEG_LANES), lambda i: (0, 0)),
                  pl.BlockSpec((c, c), lambda i: (0, 0)), tile, vec, vec],
        out_specs=tile,
        out_shape=jax.ShapeDtypeStruct((m, c), F32),
        compiler_params=_params("parallel"),
        name="rwkv_gn_proj_ln",
    )(y, bonus, g, gn_w, gn_b, _seg_ones(), w, h, ln_w, ln_b)


def _proj_ln_kernel(z_ref, w_ref, h_ref, lnw_ref, lnb_ref, o_ref):
    half = z_ref.shape[0] // 2
    halves = [slice(0, half), slice(half, 2 * half)]
    y = [jnp.dot(z_ref[rs, :], w_ref[...], preferred_element_type=F32) for rs in halves]
    for idx, rs in enumerate(halves):
        o_ref[rs, :] = _layernorm(ALPHA * h_ref[rs, :] + y[idx], lnw_ref[...], lnb_ref[...])


def _proj_ln(z, w, h, ln_w, ln_b, tm=TM):
    m, kdim = z.shape
    c = w.shape[1]
    vec = pl.BlockSpec((1, c), lambda i: (0, 0))
    return pl.pallas_call(
        _proj_ln_kernel,
        grid=(m // tm,),
        in_specs=[pl.BlockSpec((tm, kdim), lambda i: (i, 0)),
                  pl.BlockSpec((kdim, c), lambda i: (0, 0)),
                  pl.BlockSpec((tm, c), lambda i: (i, 0)), vec, vec],
        out_specs=pl.BlockSpec((tm, c), lambda i: (i, 0)),
        out_shape=jax.ShapeDtypeStruct((m, c), F32),
        compiler_params=_params("parallel"),
        name="proj_residual_ln",
    )(z, w, h, ln_w, ln_b)


def _ffn_kernel(h_ref, wg_ref, wu_ref, wd_ref, lnw_ref, lnb_ref, o_ref):
    f = pl.program_id(1)

    @pl.when(f == 0)
    def _():
        o_ref[...] = jnp.zeros_like(o_ref)

    x = h_ref[...].astype(BF16)
    gate = jnp.dot(x, wg_ref[...].astype(BF16), preferred_element_type=F32)
    up = jnp.dot(x, wu_ref[...].astype(BF16), preferred_element_type=F32)
    act = (gate * jax.nn.sigmoid(gate)) * up
    o_ref[...] += jnp.dot(act.astype(BF16), wd_ref[...].astype(BF16),
                          preferred_element_type=F32)

    @pl.when(f == pl.num_programs(1) - 1)
    def _():
        o_ref[...] = _layernorm(ALPHA * h_ref[...] + o_ref[...], lnw_ref[...], lnb_ref[...])


def _ffn_ln(h, w_gate, w_up, w_down, layer, ln_w, ln_b, tm=2 * TM, tf=256):
    m, c = h.shape
    d_ff = w_gate.shape[2]
    vec = pl.BlockSpec((1, c), lambda i, f: (0, 0))
    return pl.pallas_call(
        _ffn_kernel,
        grid=(m // tm, d_ff // tf),
        in_specs=[pl.BlockSpec((tm, c), lambda i, f: (i, 0)),
                  pl.BlockSpec((None, c, tf), lambda i, f: (layer, 0, f)),
                  pl.BlockSpec((None, c, tf), lambda i, f: (layer, 0, f)),
                  pl.BlockSpec((None, tf, c), lambda i, f: (layer, f, 0)), vec, vec],
        out_specs=pl.BlockSpec((tm, c), lambda i, f: (i, 0)),
        out_shape=jax.ShapeDtypeStruct((m, c), F32),
        compiler_params=_params("parallel", "arbitrary", vmem_limit_bytes=FFN_VMEM_LIMIT_BYTES),
        name="swiglu_ffn_ln",
    )(h, w_gate, w_up, w_down, ln_w, ln_b)


def _kv_kernel(x_ref, wk_ref, wvt_ref, k_ref, mean_ref, vt_ref, xb_ref):
    @pl.when(pl.program_id(1) == 0)
    def _():
        xb_ref[...] = x_ref[...].astype(BF16)

    xb = xb_ref[...]
    y = jnp.dot(xb, wk_ref[...], preferred_element_type=F32)
    k_ref[...] = y.astype(k_ref.dtype)
    tm, tn = y.shape
    nblk = tm // MOBA_BLOCK
    mean_ref[0] = jnp.sum(y.reshape(nblk, MOBA_BLOCK, tn), axis=1) * (1.0 / MOBA_BLOCK)
    vt = lax.dot_general(wvt_ref[...], xb, (((1,), (1,)), ((), ())), preferred_element_type=F32)
    for blk in range(nblk):
        vt_ref[blk] = vt[:, blk * MOBA_BLOCK:(blk + 1) * MOBA_BLOCK].astype(vt_ref.dtype)


def _kv_proj(h, w_k, w_v_t, tm=2 * TM, tn=512):
    m, c = h.shape
    n = w_k.shape[1]
    nblk = tm // MOBA_BLOCK
    return pl.pallas_call(
        _kv_kernel,
        grid=(m // tm, n // tn),
        in_specs=[pl.BlockSpec((tm, c), lambda i, j: (i, 0)),
                  pl.BlockSpec((c, tn), lambda i, j: (0, j)),
                  pl.BlockSpec((tn, c), lambda i, j: (j, 0))],
        out_specs=[pl.BlockSpec((tm, tn), lambda i, j: (i, j)),
                   pl.BlockSpec((1, nblk, tn), lambda i, j: (i, 0, j)),
                   pl.BlockSpec((nblk, tn, MOBA_BLOCK), lambda i, j: (i, j, 0))],
        out_shape=[jax.ShapeDtypeStruct((m, n), BF16),
                   jax.ShapeDtypeStruct((m // tm, nblk, n), F32),
                   jax.ShapeDtypeStruct((m // MOBA_BLOCK, n, MOBA_BLOCK), BF16)],
        scratch_shapes=[pltpu.VMEM((tm, c), BF16)],
        compiler_params=_params("parallel", "arbitrary"),
        name="shared_kv_proj",
    )(h, w_k, w_v_t)


def _q_kernel(x_ref, w_ref, o_ref, xb_ref):
    @pl.when(pl.program_id(1) == 0)
    def _():
        xb_ref[...] = x_ref[...].astype(BF16)

    q = jnp.dot(xb_ref[...], w_ref[...], preferred_element_type=F32)
    o_ref[...] = (q * MOBA_HEAD_DIM ** -0.5).astype(o_ref.dtype)


def _q_proj(h, w_q, tm=2 * TM, tn=1024):
    m, c = h.shape
    n = w_q.shape[1]
    return pl.pallas_call(
        _q_kernel,
        grid=(m // tm, n // tn),
        in_specs=[pl.BlockSpec((tm, c), lambda i, j: (i, 0)),
                  pl.BlockSpec((c, tn), lambda i, j: (0, j))],
        out_specs=pl.BlockSpec((tm, tn), lambda i, j: (i, j)),
        out_shape=jax.ShapeDtypeStruct((m, n), BF16),
        scratch_shapes=[pltpu.VMEM((tm, c), BF16)],
        compiler_params=_params("parallel", "arbitrary"),
        name="moba_q_proj",
    )(h, w_q)


def _moba_kernel(q_ref, k_ref, vt_ref, km_ref, o_ref, sel_ref, *, n_blocks, top_k, heads):
    blk = pl.program_id(2)
    BLK, DH = MOBA_BLOCK, MOBA_HEAD_DIM
    start = pl.multiple_of(blk * BLK, BLK)
    ki = lax.broadcasted_iota(jnp.int32, (BLK, BLK), 0)
    qi = lax.broadcasted_iota(jnp.int32, (BLK, BLK), 1)
    bi = lax.broadcasted_iota(jnp.int32, (n_blocks, BLK), 0)
    cols = [slice(hh * DH, (hh + 1) * DH) for hh in range(heads)]

    R = range(heads)
    qb = [q_ref[:, cols[h]] for h in R]

    gate = [_dot_nt_split(km_ref[0, :, cols[h]], qb[h]) for h in R]
    rank = [jnp.zeros((n_blocks, BLK), jnp.int32) for h in R]
    for mm in range(n_blocks):
        past = (mm < blk).astype(jnp.int32)
        for h in R:
            gm = gate[h][mm:mm + 1, :]
            ahead = (gm > gate[h]) | ((gm == gate[h]) & (mm < bi))
            rank[h] = rank[h] + jnp.where(ahead, 1, 0) * past
    for h in R:
        sel_ref[h] = jnp.where((bi < blk) & (rank[h] < top_k), 1.0, 0.0)

    s = [_dot_nt(k_ref[pl.ds(start, BLK), cols[h]], qb[h]) for h in R]
    s = [jnp.where(ki <= qi, s[h], NEG) for h in R]
    ones_rows = jnp.ones((ONES_ROWS, BLK), BF16)
    v_ext = lambda n, h: jnp.concatenate([vt_ref[n, cols[h], :], ones_rows], axis=0)
    m0 = [jnp.max(s[h], axis=0, keepdims=True).astype(BF16).astype(F32) for h in R]
    p = [jnp.exp((s[h] - m0[h]).astype(BF16)) for h in R]
    acc0 = [jnp.dot(v_ext(blk, h), p[h], preferred_element_type=F32) for h in R]

    def body(n, carry):
        m_i, acc = carry
        off = pl.multiple_of(n * BLK, BLK)
        sn = [_dot_nt(k_ref[pl.ds(off, BLK), cols[h]], qb[h]).astype(BF16) for h in R]
        sn = [jnp.where(sel_ref[h, pl.ds(n, 1), :] > 0.0, sn[h], jnp.asarray(NEG, BF16))
              for h in R]
        m_new = [jnp.maximum(m_i[h], jnp.max(sn[h], axis=0, keepdims=True).astype(F32))
                 for h in R]
        corr = [jnp.exp(m_i[h] - m_new[h]) for h in R]
        pn = [jnp.exp(sn[h] - m_new[h].astype(BF16)) for h in R]
        pv = [jnp.dot(v_ext(n, h), pn[h], preferred_element_type=F32) for h in R]
        acc_new = [corr[h] * acc[h] + pv[h] for h in R]
        return m_new, acc_new

    _, acc_f = lax.fori_loop(0, blk, body, (m0, acc0))
    for h in R:
        out = acc_f[h][:DH] / acc_f[h][DH:DH + 1]
        o_ref[:, cols[h]] = out.T.astype(o_ref.dtype)


def _moba_attention(q, k, v_t, k_mean, batch, seq_len, heads=MOBA_HEADS_PER_STEP):
    m, hd = q.shape
    n_blocks = seq_len // MOBA_BLOCK
    top_k = max(1, min(MOBA_TOPK, n_blocks - 1))
    BLK, W = MOBA_BLOCK, heads * MOBA_HEAD_DIM
    return pl.pallas_call(
        functools.partial(_moba_kernel, n_blocks=n_blocks, top_k=top_k, heads=heads),
        grid=(batch, hd // W, n_blocks),
        in_specs=[pl.BlockSpec((BLK, W), lambda b, h, t: (b * n_blocks + t, h)),
                  pl.BlockSpec((seq_len, W), lambda b, h, t: (b, h)),
                  pl.BlockSpec((n_blocks, W, BLK), lambda b, h, t: (b, h, 0)),
                  pl.BlockSpec((1, n_blocks, W), lambda b, h, t: (b, 0, h))],
        out_specs=pl.BlockSpec((BLK, W), lambda b, h, t: (b * n_blocks + t, h)),
        out_shape=jax.ShapeDtypeStruct((m, hd), BF16),
        scratch_shapes=[pltpu.VMEM((heads, n_blocks, BLK), F32)],
        compiler_params=_params("parallel", "parallel", "arbitrary"),
        name="moba_attention",
    )(q, k, v_t, k_mean)


def _pad_lora(w_in, w_out):
    rank = w_in.shape[1]
    pad = -rank % LORA_PAD
    return (jnp.pad(w_in, ((0, 0), (0, pad))).astype(BF16),
            jnp.pad(w_out, ((0, pad), (0, 0))).astype(BF16))


def kernel(x, a_mix, a_w_r, a_w_k, a_w_v, a_w_o, a_w0, a_w1, a_w2, a_a0, a_a1, a_a2, a_g1, a_g2,
           a_k_k, a_k_a, a_r_k, a_lnx_w, a_lnx_b, kv_w_k, kv_w_v, b_w_q, b_w_o, ffn_w_gate,
           ffn_w_up, ffn_w_down, ln1_w, ln1_b, ln2_w, ln2_b):
    batch, seq_len, c = x.shape
    n_a = a_mix.shape[0]
    n_layers = ffn_w_gate.shape[0]
    assert seq_len % MOBA_BLOCK == 0 and seq_len % (2 * TM) == 0 and c == D_MODEL
    h = x.reshape(batch * seq_len, c)
    row = lambda vec: vec.reshape(1, c)
    k_all = v_t = k_mean = None

    for layer in range(n_layers):
        if layer < n_a:
            i = layer
            w1, w2 = _pad_lora(a_w1[i], a_w2[i])
            a1, a2 = _pad_lora(a_a1[i], a_a2[i])
            g1, g2 = _pad_lora(a_g1[i], a_g2[i])
            *ops, bonus, g, gamma = _rwkv_front(
                h, a_mix[i], a_w_r[i].astype(BF16), a_w_k[i].astype(BF16), a_w_v[i].astype(BF16),
                w1, w2, row(a_w0[i]), a1, a2, row(a_a0[i]), g1, g2, row(a_k_k[i]),
                row(a_k_a[i]), row(a_r_k[i]), seq_len, SCAN_CHUNK)
            y = _rwkv_scan(ops, gamma, batch, seq_len).reshape(batch * seq_len, c)
            h = _gn_proj_ln(y, bonus, g, row(a_lnx_w[i]), row(a_lnx_b[i]),
                            a_w_o[i].astype(BF16), h, row(ln1_w[layer]), row(ln1_b[layer]))
        else:
            j = layer - n_a
            q = _q_proj(h, b_w_q[j].astype(BF16))
            z = _moba_attention(q, k_all, v_t, k_mean, batch, seq_len)
            h = _proj_ln(z, b_w_o[j].astype(BF16), h, row(ln1_w[layer]), row(ln1_b[layer]))
        h = _ffn_ln(h, ffn_w_gate, ffn_w_up, ffn_w_down, layer, row(ln2_w[layer]),
                    row(ln2_b[layer]))
        if layer == n_a - 1:
            k_all, k_mean, v_t = _kv_proj(h, kv_w_k.astype(BF16), kv_w_v.T.astype(BF16))
            k_mean = k_mean.reshape(batch, seq_len // MOBA_BLOCK, -1)
    return h.reshape(batch, seq_len, c)
```
